```python
import jax, jax.numpy as jnp
from jax import lax
import numpy as np

D_MODEL = 1024
BATCH = 16
SEQ = 4096
DEPTH = 1
DEC_BATCH = 16
DEC_SEQ = 16
PAST_LEN = 2048

CHUNK = 64
RMS_EPS = 1e-6
L2_EPS = 1e-6
A_HEADS = 8
A_HEAD_DIM = 64
A_WIDTH = A_HEADS * A_HEAD_DIM
A_PREV_CHUNKS = 8
A_REACH = A_PREV_CHUNKS * CHUNK
A_BAND = A_REACH + CHUNK
MAX_REL = 128
B_HEADS = 4
B_DK = 128
B_DV = 128
B_KW = B_HEADS * B_DK
B_VW = B_HEADS * B_DV
B_CONV_CH = 2 * B_KW + B_VW
CONV_WIDTH = 4
D_FF = 4 * D_MODEL
IN_SPLIT_SIZES = (A_WIDTH, A_WIDTH, A_WIDTH, B_CONV_CH, B_VW, B_HEADS, B_HEADS, D_MODEL, D_MODEL)
IN_COLS = sum(IN_SPLIT_SIZES)

kernel_name = "hybrid_chunkband_attn_gated_deltanet_step"


def _rms_norm(x, gain):
    xf = x.astype(jnp.float32)
    y = xf * lax.rsqrt(jnp.mean(jnp.square(xf), axis=-1, keepdims=True) + RMS_EPS)
    return (y * gain.astype(jnp.float32)).astype(x.dtype)


def _l2_normalize(x):
    xf = x.astype(jnp.float32)
    return xf * lax.rsqrt(jnp.sum(jnp.square(xf), axis=-1, keepdims=True) + L2_EPS)


def _rel_bias(rel_bias, rel):
    idx = jnp.clip(rel, -MAX_REL, MAX_REL) + MAX_REL
    return jnp.take(rel_bias.astype(jnp.float32), idx, axis=1)


def _band_attention_prompt(q, k, v, rel_bias):
    b, s, h, d = q.shape
    nc = s // CHUNK
    zeros = jnp.zeros((b, A_REACH, h, d), k.dtype)
    k_pad = jnp.concatenate([zeros, k], axis=1)
    v_pad = jnp.concatenate([zeros, v], axis=1)
    q_chunks = jnp.moveaxis(q.reshape(b, nc, CHUNK, h, d), 1, 0)
    t = jnp.arange(CHUNK)[:, None]
    m = jnp.arange(A_BAND)[None, :]
    bias = _rel_bias(rel_bias, t + A_REACH - m)
    scale = A_HEAD_DIM ** -0.5

    def one_chunk(args):
        c, qc = args
        start = c * CHUNK
        kb = lax.dynamic_slice_in_dim(k_pad, start, A_BAND, axis=1)
        vb = lax.dynamic_slice_in_dim(v_pad, start, A_BAND, axis=1)
        valid = jnp.arange(A_BAND) >= A_REACH - start
        sc = jnp.einsum('blhd,bmhd->bhlm', qc, kb, preferred_element_type=jnp.float32) * scale + bias
        sc = jnp.where(valid, sc, -jnp.inf)
        p = jax.nn.softmax(sc, axis=-1)
        return jnp.einsum('bhlm,bmhd->blhd', p.astype(vb.dtype), vb)

    out = lax.map(one_chunk, (jnp.arange(nc), q_chunks))
    return jnp.moveaxis(out, 0, 1).reshape(b, s, h, d)


def _band_attention_sample(q, k_all, v_all, rel_bias):
    t_len = q.shape[1]
    m_len = k_all.shape[1]
    past = m_len - t_len
    rel = jnp.arange(t_len)[:, None] + past - jnp.arange(m_len)[None, :]
    bias = _rel_bias(rel_bias, rel)
    sc = jnp.einsum('blhd,bmhd->bhlm', q, k_all, preferred_element_type=jnp.float32) * (A_HEAD_DIM ** -0.5) + bias
    p = jax.nn.softmax(sc, axis=-1)
    return jnp.einsum('bhlm,bmhd->blhd', p.astype(v_all.dtype), v_all)


def _causal_conv_silu(x_ext, conv_w):
    t_len = x_ext.shape[1] - (CONV_WIDTH - 1)
    y = x_ext[:, 0:t_len] * conv_w[0]
    for i in range(1, CONV_WIDTH):
        y = y + x_ext[:, i:i + t_len] * conv_w[i]
    return jax.nn.silu(y)


def _gated_delta_rule(q, k, v, g, beta, s0):
    f32 = jnp.float32
    b, t_len, h, dk = q.shape
    dv = v.shape[-1]
    L = CHUNK if t_len % CHUNK == 0 else t_len
    nc = t_len // L

    def chunks(x):
        x = x.astype(f32).reshape((b, nc, L, h) + x.shape[3:])
        return jnp.moveaxis(x, 3, 2)

    q = chunks(q) * (dk ** -0.5)
    k = chunks(k)
    v = chunks(v)
    g = chunks(g)
    beta = chunks(beta)
    gc = jnp.cumsum(g, axis=-1)
    tril = jnp.tril(jnp.ones((L, L), bool))
    strict = jnp.tril(jnp.ones((L, L), bool), -1)
    diff = gc[..., :, None] - gc[..., None, :]
    decay = jnp.where(tril, jnp.exp(jnp.where(tril, diff, 0.0)), 0.0)
    kk = jnp.einsum('bnhid,bnhjd->bnhij', k, k)
    a_mat = jnp.where(strict, beta[..., None] * kk * decay, 0.0) + jnp.eye(L, dtype=f32)
    rhs = jnp.concatenate([v * beta[..., None], k * (beta * jnp.exp(gc))[..., None]], axis=-1)
    sol = lax.linalg.triangular_solve(a_mat, rhs, left_side=True, lower=True, unit_diagonal=True)
    u = sol[..., :dv]
    w = sol[..., dv:]
    qk = jnp.einsum('bnhid,bnhjd->bnhij', q, k) * decay
    q_dec = q * jnp.exp(gc)[..., None]
    k_dec = k * jnp.exp(gc[..., -1:] - gc)[..., None]
    g_last = jnp.exp(gc[..., -1])

    def step(s, inp):
        u_c, w_c, qk_c, qd_c, kd_c, gl_c = inp
        v_new = u_c - jnp.einsum('bhld,bhde->bhle', w_c, s)
        o = jnp.einsum('bhld,bhde->bhle', qd_c, s) + jnp.einsum('bhij,bhje->bhie', qk_c, v_new)
        s = s * gl_c[..., None, None] + jnp.einsum('bhld,bhle->bhde', kd_c, v_new)
        return s, o

    xs = tuple(jnp.moveaxis(a, 1, 0) for a in (u, w, qk, q_dec, k_dec, g_last))
    s_fin, o = lax.scan(step, s0.astype(f32), xs)
    o = o.transpose(1, 0, 3, 2, 4).reshape(b, t_len, h, dv)
    return o, s_fin


def _layer(x, cache_k, cache_v, conv_buf, delta_s0,
           norm_mix_pre, w_in, rel_bias, conv_w, a_log, dt_bias, delta_norm_w,
           w_proj_a, w_proj_b, w_out, norm_mix_post, norm_mlp_pre, w_up, w_down, norm_mlp_post):
    b, t_len, _ = x.shape
    h = _rms_norm(x, norm_mix_pre)
    p = h @ w_in
    split_at = [int(i) for i in np.cumsum(IN_SPLIT_SIZES)[:-1]]
    qa, ka, va, qkv_b, z_b, a_b, b_b, gate_a, gate_b = jnp.split(p, split_at, axis=-1)

    qa = qa.reshape(b, t_len, A_HEADS, A_HEAD_DIM)
    ka = ka.reshape(b, t_len, A_HEADS, A_HEAD_DIM)
    va = va.reshape(b, t_len, A_HEADS, A_HEAD_DIM)
    if cache_k is None:
        oa = _band_attention_prompt(qa, ka, va, rel_bias)
        keep = min(A_REACH, t_len)
        new_k, new_v = ka[:, t_len - keep:], va[:, t_len - keep:]
    else:
        k_all = jnp.concatenate([cache_k.astype(ka.dtype), ka], axis=1)
        v_all = jnp.concatenate([cache_v.astype(va.dtype), va], axis=1)
        oa = _band_attention_sample(qa, k_all, v_all, rel_bias)
        new_k, new_v = ka, va

    x_ext = jnp.concatenate([conv_buf.astype(qkv_b.dtype), qkv_b], axis=1)
    new_conv = x_ext[:, -(CONV_WIDTH - 1):]
    c = _causal_conv_silu(x_ext, conv_w)
    qb, kb, vb = jnp.split(c, [B_KW, 2 * B_KW], axis=-1)
    qb = _l2_normalize(qb.reshape(b, t_len, B_HEADS, B_DK))
    kb = _l2_normalize(kb.reshape(b, t_len, B_HEADS, B_DK))
    vb = vb.reshape(b, t_len, B_HEADS, B_DV)
    g = -jnp.exp(a_log.astype(jnp.float32)) * jax.nn.softplus(a_b.astype(jnp.float32) + dt_bias.astype(jnp.float32))
    beta = jax.nn.sigmoid(b_b.astype(jnp.float32))
    ob, s_new = _gated_delta_rule(qb, kb, vb, g, beta, delta_s0)
    zb = z_b.reshape(b, t_len, B_HEADS, B_DV).astype(jnp.float32)
    ob = (_rms_norm(ob, delta_norm_w) * jax.nn.silu(zb)).astype(x.dtype).reshape(b, t_len, B_VW)

    y_a = oa.reshape(b, t_len, A_WIDTH) @ w_proj_a
    y_b = ob @ w_proj_b
    merged = jax.nn.sigmoid(gate_a) * y_a + jax.nn.sigmoid(gate_b) * y_b
    x = x + _rms_norm(merged @ w_out, norm_mix_post)

    h2 = _rms_norm(x, norm_mlp_pre)
    f = jnp.square(jax.nn.relu(h2 @ w_up)) @ w_down
    x = x + _rms_norm(f, norm_mlp_post)
    return x, new_k, new_v, new_conv, s_new.astype(x.dtype)


def setup_inputs(seed: int = 0) -> dict:
    key = jax.random.key(seed)
    ks = jax.random.split(key, 24)
    f32 = jnp.float32
    cache_len = min(A_REACH, PAST_LEN)

    def nrm(k, shape, scale):
        return jax.random.normal(k, shape, f32) * scale

    def gain(k, shape):
        return 1.0 + 0.02 * jax.random.normal(k, shape, f32)

    return {
        'x_prompt': nrm(ks[0], (BATCH, SEQ, D_MODEL), 1.0),
        'x_sample': nrm(ks[1], (DEC_BATCH, DEC_SEQ, D_MODEL), 1.0),
        'cache_attn_k': nrm(ks[2], (DEPTH, DEC_BATCH, cache_len, A_HEADS, A_HEAD_DIM), 1.0),
        'cache_attn_v': nrm(ks[3], (DEPTH, DEC_BATCH, cache_len, A_HEADS, A_HEAD_DIM), 1.0),
        'state_conv': nrm(ks[4], (DEPTH, DEC_BATCH, CONV_WIDTH - 1, B_CONV_CH), 1.0),
        'state_delta': nrm(ks[5], (DEPTH, DEC_BATCH, B_HEADS, B_DK, B_DV), B_DK ** -0.5),
        'norm_mix_pre': gain(ks[6], (DEPTH, D_MODEL)),
        'w_in': nrm(ks[7], (DEPTH, D_MODEL, IN_COLS), D_MODEL ** -0.5),
        'rel_bias': nrm(ks[8], (DEPTH, A_HEADS, 2 * MAX_REL + 1), 0.2),
        'conv_w': nrm(ks[9], (DEPTH, CONV_WIDTH, B_CONV_CH), CONV_WIDTH ** -0.5),
        'a_log': jnp.log(jax.random.uniform(ks[10], (DEPTH, B_HEADS), f32, 1.0, 16.0)),
        'dt_bias': nrm(ks[11], (DEPTH, B_HEADS), 0.1),
        'delta_norm_w': gain(ks[12], (DEPTH, B_DV)),
        'w_proj_a': nrm(ks[13], (DEPTH, A_WIDTH, D_MODEL), A_WIDTH ** -0.5),
        'w_proj_b': nrm(ks[14], (DEPTH, B_VW, D_MODEL), B_VW ** -0.5),
        'w_out': nrm(ks[15], (DEPTH, D_MODEL, D_MODEL), D_MODEL ** -0.5),
        'norm_mix_post': gain(ks[16], (DEPTH, D_MODEL)),
        'norm_mlp_pre': gain(ks[17], (DEPTH, D_MODEL)),
        'w_up': nrm(ks[18], (DEPTH, D_MODEL, D_FF), D_MODEL ** -0.5),
        'w_down': nrm(ks[19], (DEPTH, D_FF, D_MODEL), D_FF ** -0.5),
        'norm_mlp_post': gain(ks[20], (DEPTH, D_MODEL)),
    }


def reference(x_prompt, x_sample, cache_attn_k, cache_attn_v, state_conv, state_delta,
              norm_mix_pre, w_in, rel_bias, conv_w, a_log, dt_bias, delta_norm_w,
              w_proj_a, w_proj_b, w_out, norm_mix_post, norm_mlp_pre, w_up, w_down, norm_mlp_post):
    b = x_prompt.shape[0]
    yp, ys = x_prompt, x_sample
    pk, pv, pc, ps = [], [], [], []
    sk, sv, sc, ss = [], [], [], []
    for l in range(DEPTH):
        weights = (norm_mix_pre[l], w_in[l], rel_bias[l], conv_w[l], a_log[l], dt_bias[l], delta_norm_w[l],
                   w_proj_a[l], w_proj_b[l], w_out[l], norm_mix_post[l], norm_mlp_pre[l], w_up[l], w_down[l],
                   norm_mlp_post[l])
        conv0 = jnp.zeros((b, CONV_WIDTH - 1, B_CONV_CH), yp.dtype)
        s0 = jnp.zeros((b, B_HEADS, B_DK, B_DV), jnp.float32)
        yp, k1, v1, c1, s1 = _layer(yp, None, None, conv0, s0, *weights)
        ys, k2, v2, c2, s2 = _layer(ys, cache_attn_k[l], cache_attn_v[l], state_conv[l], state_delta[l], *weights)
        pk.append(k1); pv.append(v1); pc.append(c1); ps.append(s1)
        sk.append(k2); sv.append(v2); sc.append(c2); ss.append(s2)
    return (yp, ys, jnp.stack(pk), jnp.stack(pv), jnp.stack(pc), jnp.stack(ps),
            jnp.stack(sk), jnp.stack(sv), jnp.stack(sc), jnp.stack(ss))
```

```python
import functools
import math

import jax
import jax.numpy as jnp
from jax import lax
from jax.experimental import pallas as pl
from jax.experimental.pallas import tpu as pltpu

F32 = jnp.float32
BF16 = jnp.bfloat16

D_MODEL = 1024
CHUNK = 64
RMS_EPS = 1e-6
L2_EPS = 1e-6
A_HEADS = 8
A_HEAD_DIM = 64
A_WIDTH = A_HEADS * A_HEAD_DIM
A_PREV_CHUNKS = 8
A_REACH = A_PREV_CHUNKS * CHUNK
MAX_REL = 128
B_HEADS = 4
B_DK = 128
B_DV = 128
B_KW = B_HEADS * B_DK
B_VW = B_HEADS * B_DV
B_CONV_CH = 2 * B_KW + B_VW
CONV_WIDTH = 4
D_FF = 4 * D_MODEL

LANES = 128
SUBLANES = 8
VMEM_LIMIT_BYTES = 56 * 1024 * 1024

Q_BLOCK = 256
N_KEY_BLOCKS = A_REACH // Q_BLOCK + 1
DELTA_BLOCK = 256
SAMPLE_PAD = 128
ROW_TILE = 512


def _dot(a, b):
    return jnp.dot(a, b, preferred_element_type=F32)


def _dot_nt(a, b):
    return lax.dot_general(a, b, (((1,), (1,)), ((), ())), preferred_element_type=F32)


def _sigmoid(x):
    return 1.0 / (1.0 + jnp.exp(-x))


def _rms(x, gain):
    ms = jnp.mean(x * x, axis=-1, keepdims=True)
    return x * lax.rsqrt(ms + RMS_EPS) * gain


def _params(sem):
    return pltpu.CompilerParams(dimension_semantics=sem, vmem_limit_bytes=VMEM_LIMIT_BYTES)


def _const_spec(shape):
    nd = len(shape)
    return pl.BlockSpec(shape, lambda *_: (0,) * nd, pipeline_mode=pl.Buffered(1))


def _inproj_kernel(x_ref, gain_ref, wq_ref, wk_ref, wv_ref, wcv_ref, wzab_ref, wg_ref,
                   q_ref, k_ref, v_ref, cv_ref, z_ref, ab_ref, g_ref, *tail_refs, prompt):
    x = x_ref[0]
    hb = _rms(x, gain_ref[...]).astype(BF16)
    tm = x.shape[0]

    q = _dot(hb, wq_ref[...])
    q_ref[0] = (q * (A_HEAD_DIM ** -0.5)).astype(BF16)

    k = _dot(hb, wk_ref[...])
    v = _dot(hb, wv_ref[...])
    if prompt:
        kf_ref, vf_ref, cvt_ref = tail_refs
        kt = k.T.astype(BF16)
        for j in range(tm // Q_BLOCK):
            k_ref[0, j] = kt[:, j * Q_BLOCK:(j + 1) * Q_BLOCK]
        v_ref[0] = v.astype(BF16)
        kf_ref[0] = k
        vf_ref[0] = v
    else:
        k_ref[0] = k
        v_ref[0] = v

    cv = _dot(hb, wcv_ref[...])
    if prompt:
        cv_ref[0] = cv.astype(BF16)
        cvt_ref[0] = cv[tm - SUBLANES:, :]
    else:
        cv_ref[0] = cv

    zab = _dot(hb, wzab_ref[...])
    z_ref[0] = zab[:, :B_VW].astype(BF16)
    ab_ref[0] = zab[:, B_VW:]

    g = _dot(hb, wg_ref[...])
    g_ref[0] = _sigmoid(g).astype(BF16)


def _inproj(x, gain, wq, wk, wv, wcv, wzab, wg, *, prompt):
    b, t, d = x.shape
    tm = min(ROW_TILE, t)
    nt = t // tm
    assert t % tm == 0
    grid = (b, nt)
    row = lambda w, dt: (jax.ShapeDtypeStruct((b, t, w), dt), pl.BlockSpec((1, tm, w), lambda i, j: (i, j, 0)))
    outs = [row(A_WIDTH, BF16)]
    if prompt:
        assert tm == A_REACH and tm % Q_BLOCK == 0
        nq = tm // Q_BLOCK
        outs.append((jax.ShapeDtypeStruct((b, t // Q_BLOCK, A_WIDTH, Q_BLOCK), BF16),
                     pl.BlockSpec((1, nq, A_WIDTH, Q_BLOCK), lambda i, j: (i, j, 0, 0))))
        outs.append(row(A_WIDTH, BF16))
        outs.append(row(B_CONV_CH, BF16))
    else:
        outs.append(row(A_WIDTH, F32))
        outs.append(row(A_WIDTH, F32))
        outs.append(row(B_CONV_CH, F32))
    outs.append(row(B_VW, BF16))
    outs.append(row(LANES, F32))
    outs.append(row(2 * D_MODEL, BF16))
    if prompt:
        stream = lambda r, w: (jax.ShapeDtypeStruct((b, r, w), F32), pl.BlockSpec((1, r, w), lambda i, j: (i, 0, 0)))
        outs.append(stream(tm, A_WIDTH))
        outs.append(stream(tm, A_WIDTH))
        outs.append(stream(SUBLANES, B_CONV_CH))
    in_specs = [pl.BlockSpec((1, tm, d), lambda i, j: (i, j, 0)), _const_spec(gain.shape)]
    in_specs += [_const_spec(w.shape) for w in (wq, wk, wv, wcv, wzab, wg)]
    return pl.pallas_call(
        functools.partial(_inproj_kernel, prompt=prompt),
        grid=grid,
        in_specs=in_specs,
        out_specs=[o[1] for o in outs],
        out_shape=[o[0] for o in outs],
        compiler_params=_params(("arbitrary", "arbitrary")),
        name="inproj_prompt" if prompt else "inproj_sample",
    )(x, gain, wq, wk, wv, wcv, wzab, wg)


def _softmax_pv(scores, values):
    m = None
    for s in scores:
        mj = jnp.max(s, axis=-1, keepdims=True)
        m = mj if m is None else jnp.maximum(m, mj)
    denom = None
    acc = None
    for s, vals in zip(scores, values):
        p = jnp.exp(s - m)
        dj = jnp.sum(p, axis=-1, keepdims=True)
        denom = dj if denom is None else denom + dj
        pv = _dot(p.astype(BF16), vals)
        acc = pv if acc is None else acc + pv
    return acc / denom


def _attn_prompt_kernel(q_ref, kt_ref, v_ref, bias_ref, o_ref):
    qi = pl.program_id(1)
    lane = lax.broadcasted_iota(jnp.int32, (Q_BLOCK, LANES), 1)
    low = lane < A_HEAD_DIM
    for hp in range(A_HEADS // 2):
        cols = slice(hp * LANES, (hp + 1) * LANES)
        qp = q_ref[0, :, cols]
        pair_out = []
        for e in range(2):
            h = 2 * hp + e
            qm = jnp.where(low if e == 0 else jnp.logical_not(low), qp, jnp.zeros_like(qp))
            scores, values = [], []
            for j in range(N_KEY_BLOCKS):
                kraw = qi - (N_KEY_BLOCKS - 1) + j
                kb = jnp.maximum(kraw, 0)
                bsel = jnp.where(kraw >= 0, j, N_KEY_BLOCKS)
                scores.append(_dot(qm, kt_ref[0, kb, cols, :]) + bias_ref[h, bsel])
                values.append(v_ref[0, pl.ds(pl.multiple_of(kb * Q_BLOCK, Q_BLOCK), Q_BLOCK), cols])
            pair_out.append(_softmax_pv(scores, values))
        o_ref[0, :, cols] = jnp.where(low, pair_out[0], pair_out[1]).astype(BF16)


def _prompt_bias_table(rel_bias):
    width = N_KEY_BLOCKS * Q_BLOCK
    t = jnp.arange(Q_BLOCK)[:, None]
    c = jnp.arange(width)[None, :]
    rel = t + A_REACH - c
    idx = jnp.clip(rel, -MAX_REL, MAX_REL) + MAX_REL
    bias = jnp.take(rel_bias.astype(F32), idx, axis=1)
    tc, cc = t // CHUNK, c // CHUNK
    band = (cc >= tc) & (cc <= tc + A_PREV_CHUNKS)
    bias = jnp.where(band[None], bias, -jnp.inf)
    bias = bias.reshape(A_HEADS, Q_BLOCK, N_KEY_BLOCKS, Q_BLOCK).transpose(0, 2, 1, 3)
    masked = jnp.full((A_HEADS, 1, Q_BLOCK, Q_BLOCK), -jnp.inf, F32)
    return jnp.concatenate([bias, masked], axis=1)


def _attn_prompt(q, kt, v, bias):
    b, t, _ = q.shape
    return pl.pallas_call(
        _attn_prompt_kernel,
        grid=(b, t // Q_BLOCK),
        in_specs=[
            pl.BlockSpec((1, Q_BLOCK, A_WIDTH), lambda i, j: (i, j, 0)),
            pl.BlockSpec((1, t // Q_BLOCK, A_WIDTH, Q_BLOCK), lambda i, j: (i, 0, 0, 0)),
            pl.BlockSpec((1, t, A_WIDTH), lambda i, j: (i, 0, 0)),
            _const_spec(bias.shape),
        ],
        out_specs=pl.BlockSpec((1, Q_BLOCK, A_WIDTH), lambda i, j: (i, j, 0)),
        out_shape=jax.ShapeDtypeStruct((b, t, A_WIDTH), BF16),
        compiler_params=_params(("arbitrary", "arbitrary")),
        name="attn_prompt",
    )(q, kt, v, bias)


def _attn_sample_kernel(q_ref, ck_ref, cv_ref, nk_ref, nv_ref, bc_ref, bn_ref, o_ref):
    rows = q_ref.shape[1]
    lane = lax.broadcasted_iota(jnp.int32, (rows, LANES), 1)
    low = lane < A_HEAD_DIM
    for hp in range(A_HEADS // 2):
        cols = slice(hp * LANES, (hp + 1) * LANES)
        qp = q_ref[0, :, cols]
        ck = ck_ref[0, :, cols].astype(BF16)
        cv = cv_ref[0, :, cols].astype(BF16)
        nk = nk_ref[0, :, cols].astype(BF16)
        nv = nv_ref[0, :, cols].astype(BF16)
        pair_out = []
        for e in range(2):
            h = 2 * hp + e
            qm = jnp.where(low if e == 0 else jnp.logical_not(low), qp, jnp.zeros_like(qp))
            scores = [_dot_nt(qm, ck) + bc_ref[h], _dot_nt(qm, nk) + bn_ref[h]]
            pair_out.append(_softmax_pv(scores, [cv, nv]))
        o_ref[0, :, cols] = jnp.where(low, pair_out[0], pair_out[1]).astype(BF16)


def _attn_sample(q, cache_k, cache_v, new_k, new_v, rel_bias):
    b, t, _ = q.shape
    c = cache_k.shape[1]
    rel = jnp.arange(t)[:, None] + c - jnp.arange(c + SAMPLE_PAD)[None, :]
    idx = jnp.clip(rel, -MAX_REL, MAX_REL) + MAX_REL
    bias = jnp.take(rel_bias.astype(F32), idx, axis=1)
    bias = jnp.where((jnp.arange(c + SAMPLE_PAD) < c + t)[None, None, :], bias, -jnp.inf)
    bias_c, bias_n = bias[:, :, :c], bias[:, :, c:]
    pad = ((0, 0), (0, SAMPLE_PAD - t), (0, 0))
    nk, nv = jnp.pad(new_k, pad), jnp.pad(new_v, pad)
    blk = lambda r: pl.BlockSpec((1, r, A_WIDTH), lambda i: (i, 0, 0))
    return pl.pallas_call(
        _attn_sample_kernel,
        grid=(b,),
        in_specs=[blk(t), blk(c), blk(c), blk(SAMPLE_PAD), blk(SAMPLE_PAD),
                  _const_spec(bias_c.shape), _const_spec(bias_n.shape)],
        out_specs=blk(t),
        out_shape=jax.ShapeDtypeStruct((b, t, A_WIDTH), BF16),
        compiler_params=_params(("arbitrary",)),
        name="attn_sample",
    )(q, cache_k, cache_v, nk, nv, bias_c, bias_n)


def _delta_kernel(cv_ref, z_ref, ab_ref, convw_ref, prev_ref, s0_ref, alog_ref, dtb_ref, nw_ref,
                  o_ref, s_ref, ext_ref, rows_ref, *, n_valid):
    tt = cv_ref.shape[1]
    step = pl.program_id(1)

    @pl.when(step == 0)
    def _():
        ext_ref[0:SUBLANES, :] = prev_ref[0]
        s_ref[0] = s0_ref[0]

    ext_ref[SUBLANES:SUBLANES + tt, :] = cv_ref[0].astype(F32)
    base = SUBLANES - (CONV_WIDTH - 1)
    acc = ext_ref[base:base + tt, :] * convw_ref[0:1, :]
    for i in range(1, CONV_WIDTH):
        acc = acc + ext_ref[base + i:base + i + tt, :] * convw_ref[i:i + 1, :]
    y = acc * _sigmoid(acc)
    ext_ref[0:SUBLANES, :] = ext_ref[tt:tt + SUBLANES, :]

    ab_t = ab_ref[0].T
    rows8 = ab_t[0:2 * B_HEADS, :]
    x = rows8 + dtb_ref[:, 0:1]
    softplus = jnp.maximum(x, 0.0) + jnp.log(1.0 + jnp.exp(-jnp.abs(x)))
    g8 = -jnp.exp(alog_ref[:, 0:1]) * softplus
    beta8 = _sigmoid(rows8)
    tok = lax.broadcasted_iota(jnp.int32, (2 * B_HEADS, tt), 1)
    if n_valid < tt:
        g8 = jnp.where(tok < n_valid, g8, 0.0)
        beta8 = jnp.where(tok < n_valid, beta8, 0.0)
    gc8 = g8
    shift = 1
    while shift < tt:
        gc8 = gc8 + jnp.where(tok >= shift, pltpu.roll(gc8, shift, 1), 0.0)
        shift *= 2
    gl8 = gc8[:, tt - 1:tt]
    rows_ref[...] = jnp.zeros_like(rows_ref)
    rows_ref[0:8, :] = gc8
    rows_ref[8:16, :] = beta8
    rows_ref[16:24, :] = jnp.exp(gc8)
    rows_ref[24:32, :] = jnp.exp(gl8 - gc8)
    cols = rows_ref[...].T

    ri = lax.broadcasted_iota(jnp.int32, (tt, tt), 0)
    ci = lax.broadcasted_iota(jnp.int32, (tt, tt), 1)
    tril = ri >= ci
    strict = ri > ci
    eye = (ri == ci).astype(F32)
    n_double = int(math.log2(tt)) - 1

    for h in range(B_HEADS):
        hs = slice(h * B_DK, (h + 1) * B_DK)
        qh = y[:, hs]
        kh = y[:, B_KW + h * B_DK:B_KW + (h + 1) * B_DK]
        vh = y[:, 2 * B_KW + h * B_DV:2 * B_KW + (h + 1) * B_DV]
        qn = qh * lax.rsqrt(jnp.sum(qh * qh, axis=-1, keepdims=True) + L2_EPS) * (B_DK ** -0.5)
        kn = kh * lax.rsqrt(jnp.sum(kh * kh, axis=-1, keepdims=True) + L2_EPS)
        gc_c = cols[:, h:h + 1]
        beta_c = cols[:, 8 + B_HEADS + h:9 + B_HEADS + h]
        egc_c = cols[:, 16 + h:17 + h]
        edec_c = cols[:, 24 + h:25 + h]
        gc_r = gc8[h:h + 1, :]
        decay = jnp.exp(jnp.where(tril, gc_c - gc_r, -jnp.inf))

        knb = kn.astype(BF16)
        kk = _dot_nt(knb, knb)
        qk = _dot_nt(qn.astype(BF16), knb)

        m = jnp.where(strict, -(beta_c * kk * decay), 0.0)
        t_inv = eye + m
        p = m
        for _ in range(n_double):
            pb = p.astype(BF16)
            p = _dot(pb, pb)
            t_inv = t_inv + _dot(t_inv.astype(BF16), p.astype(BF16))

        rhs = jnp.concatenate([vh * beta_c, kn * (beta_c * egc_c)], axis=1).astype(BF16)
        sol = _dot(t_inv.astype(BF16), rhs)
        u = sol[:, :B_DV]
        w = sol[:, B_DV:]

        s = s_ref[0, h]
        sb = s.astype(BF16)
        v_new = u - _dot(w.astype(BF16), sb)
        vnb = v_new.astype(BF16)
        o = _dot((qn * egc_c).astype(BF16), sb) + _dot((qk * decay).astype(BF16), vnb)
        k_dec_t = (kn * edec_c).T.astype(BF16)
        s_ref[0, h] = s * jnp.exp(gl8[h:h + 1, :]) + _dot(k_dec_t, vnb)

        zh = z_ref[0, :, hs].astype(F32)
        o_ref[0, :, hs] = (_rms(o, nw_ref[...]) * (zh * _sigmoid(zh))).astype(BF16)


def _delta(cv, z, ab, conv_w, prev, s0, a_log, dt_bias, norm_w, *, block, n_valid):
    b, t, _ = cv.shape
    assert t % block == 0
    alog8 = jnp.broadcast_to(jnp.tile(a_log.astype(F32), 2)[:, None], (2 * B_HEADS, LANES))
    dtb8 = jnp.broadcast_to(jnp.tile(dt_bias.astype(F32), 2)[:, None], (2 * B_HEADS, LANES))
    row = lambda w: pl.BlockSpec((1, block, w), lambda i, j: (i, j, 0))
    state = pl.BlockSpec((1, B_HEADS, B_DK, B_DV), lambda i, j: (i, 0, 0, 0))
    return pl.pallas_call(
        functools.partial(_delta_kernel, n_valid=n_valid),
        grid=(b, t // block),
        in_specs=[row(B_CONV_CH), row(B_VW), row(LANES),
                  _const_spec(conv_w.shape),
                  pl.BlockSpec((1, SUBLANES, B_CONV_CH), lambda i, j: (i, 0, 0)),
                  state, _const_spec(alog8.shape), _const_spec(dtb8.shape), _const_spec((1, B_DV))],
        out_specs=[row(B_VW), state],
        out_shape=[jax.ShapeDtypeStruct((b, t, B_VW), BF16),
                   jax.ShapeDtypeStruct((b, B_HEADS, B_DK, B_DV), F32)],
        scratch_shapes=[pltpu.VMEM((block + SUBLANES, B_CONV_CH), F32),
                        pltpu.VMEM((LANES, block), F32)],
        compiler_params=_params(("arbitrary", "arbitrary")),
        name="delta_%d" % block,
    )(cv, z, ab, conv_w, prev, s0, alog8, dtb8, norm_w.reshape(1, B_DV).astype(F32))


def _merge_kernel(x_ref, oa_ref, ob_ref, g_ref, wpa_ref, wpb_ref, wout_ref, gain_ref, o_ref):
    ya = _dot(oa_ref[...], wpa_ref[...])
    yb = _dot(ob_ref[...], wpb_ref[...])
    merged = g_ref[:, :D_MODEL].astype(F32) * ya + g_ref[:, D_MODEL:].astype(F32) * yb
    y = _dot(merged.astype(BF16), wout_ref[...])
    o_ref[...] = x_ref[...] + _rms(y, gain_ref[...])


def _merge(x, oa, ob, gates, wpa, wpb, wout, gain):
    n, d = x.shape
    tm = min(ROW_TILE, n)
    row = lambda w: pl.BlockSpec((tm, w), lambda i: (i, 0))
    return pl.pallas_call(
        _merge_kernel,
        grid=(n // tm,),
        in_specs=[row(d), row(A_WIDTH), row(B_VW), row(2 * d),
                  _const_spec(wpa.shape), _const_spec(wpb.shape), _const_spec(wout.shape), _const_spec(gain.shape)],
        out_specs=row(d),
        out_shape=jax.ShapeDtypeStruct((n, d), F32),
        compiler_params=_params(("arbitrary",)),
        name="merge_%d" % n,
    )(x, oa, ob, gates, wpa, wpb, wout, gain)


FF_SPLIT = 4


def _mlp_kernel(x_ref, gpre_ref, wup_ref, wdown_ref, gpost_ref, o_ref):
    x = x_ref[...]
    hb = _rms(x, gpre_ref[...]).astype(BF16)
    wcols = D_FF // FF_SPLIT
    f = None
    for c in range(FF_SPLIT):
        cs = slice(c * wcols, (c + 1) * wcols)
        up = jnp.maximum(_dot(hb, wup_ref[:, cs]), 0.0)
        part = _dot((up * up).astype(BF16), wdown_ref[cs, :])
        f = part if f is None else f + part
    o_ref[...] = x + _rms(f, gpost_ref[...])


def _mlp(x, gpre, wup, wdown, gpost):
    n, d = x.shape
    tm = min(ROW_TILE, n)
    row = pl.BlockSpec((tm, d), lambda i: (i, 0))
    return pl.pallas_call(
        _mlp_kernel,
        grid=(n // tm,),
        in_specs=[row, _const_spec(gpre.shape), _const_spec(wup.shape), _const_spec(wdown.shape),
                  _const_spec(gpost.shape)],
        out_specs=row,
        out_shape=jax.ShapeDtypeStruct((n, d), F32),
        compiler_params=_params(("arbitrary",)),
        name="mlp_%d" % n,
    )(x, gpre, wup, wdown, gpost)


def _layer_weights(norm_mix_pre, w_in, conv_w, w_proj_a, w_proj_b, w_out, norm_mix_post,
                   norm_mlp_pre, w_up, w_down, norm_mlp_post):
    o_cv = 3 * A_WIDTH
    o_z = o_cv + B_CONV_CH
    o_ab = o_z + B_VW
    o_g = o_ab + 2 * B_HEADS
    wb = w_in.astype(BF16)
    wzab = jnp.concatenate([wb[:, o_z:o_g], jnp.zeros((D_MODEL, LANES - 2 * B_HEADS), BF16)], axis=1)
    vec = lambda g: g.reshape(1, -1).astype(F32)
    return dict(
        gain_in=vec(norm_mix_pre),
        wq=wb[:, :A_WIDTH], wk=wb[:, A_WIDTH:2 * A_WIDTH], wv=wb[:, 2 * A_WIDTH:o_cv],
        wcv=wb[:, o_cv:o_z], wzab=wzab, wg=wb[:, o_g:],
        conv_w=conv_w.astype(F32),
        wpa=w_proj_a.astype(BF16), wpb=w_proj_b.astype(BF16), wout=w_out.astype(BF16),
        gain_mix=vec(norm_mix_post), gain_pre=vec(norm_mlp_pre), gain_post=vec(norm_mlp_post),
        wup=w_up.astype(BF16), wdown=w_down.astype(BF16),
    )


def _tail(x, w, oa, ob, gates):
    b, t, d = x.shape
    x1 = _merge(x.reshape(b * t, d), oa.reshape(b * t, A_WIDTH), ob.reshape(b * t, B_VW),
                gates.reshape(b * t, 2 * d), w["wpa"], w["wpb"], w["wout"], w["gain_mix"])
    return _mlp(x1, w["gain_pre"], w["wup"], w["wdown"], w["gain_post"]).reshape(b, t, d)


def _prompt_layer(x, w, rel_bias, a_log, dt_bias, delta_norm_w):
    b, t, _ = x.shape
    q, kt, v, cv, z, ab, gates, kf, vf, cvt = _inproj(
        x, w["gain_in"], w["wq"], w["wk"], w["wv"], w["wcv"], w["wzab"], w["wg"], prompt=True)
    oa = _attn_prompt(q, kt, v, _prompt_bias_table(rel_bias))
    prev = jnp.zeros((b, SUBLANES, B_CONV_CH), F32)
    s0 = jnp.zeros((b, B_HEADS, B_DK, B_DV), F32)
    ob, s_new = _delta(cv, z, ab, w["conv_w"], prev, s0, a_log, dt_bias, delta_norm_w,
                       block=DELTA_BLOCK, n_valid=DELTA_BLOCK)
    y = _tail(x, w, oa, ob, gates)
    keep = min(A_REACH, t)
    new_k = kf.reshape(b, keep, A_HEADS, A_HEAD_DIM)
    new_v = vf.reshape(b, keep, A_HEADS, A_HEAD_DIM)
    new_conv = cvt[:, SUBLANES - (CONV_WIDTH - 1):, :]
    return y, new_k, new_v, new_conv, s_new


def _sample_layer(x, cache_k, cache_v, conv_state, delta_state, w, rel_bias, a_log, dt_bias, delta_norm_w):
    b, t, d = x.shape
    c = cache_k.shape[1]
    q, k, v, cv, z, ab, gates = _inproj(
        x.reshape(1, b * t, d), w["gain_in"], w["wq"], w["wk"], w["wv"], w["wcv"], w["wzab"], w["wg"],
        prompt=False)
    per_stream = lambda a: a.reshape(b, t, a.shape[-1])
    q, k, v, cv, z, ab, gates = map(per_stream, (q, k, v, cv, z, ab, gates))
    oa = _attn_sample(q, cache_k.reshape(b, c, A_WIDTH), cache_v.reshape(b, c, A_WIDTH), k, v, rel_bias)
    pad = ((0, 0), (0, SAMPLE_PAD - t), (0, 0))
    prev = jnp.pad(conv_state.astype(F32), ((0, 0), (SUBLANES - (CONV_WIDTH - 1), 0), (0, 0)))
    ob, s_new = _delta(jnp.pad(cv, pad), jnp.pad(z, pad), jnp.pad(ab, pad), w["conv_w"], prev,
                       delta_state.astype(F32), a_log, dt_bias, delta_norm_w,
                       block=SAMPLE_PAD, n_valid=t)
    y = _tail(x, w, oa, ob[:, :t], gates)
    new_conv = jnp.concatenate([conv_state.astype(F32), cv], axis=1)[:, -(CONV_WIDTH - 1):]
    return (y, k.reshape(b, t, A_HEADS, A_HEAD_DIM), v.reshape(b, t, A_HEADS, A_HEAD_DIM), new_conv, s_new)


def kernel(x_prompt, x_sample, cache_attn_k, cache_attn_v, state_conv, state_delta, norm_mix_pre, w_in,
           rel_bias, conv_w, a_log, dt_bias, delta_norm_w, w_proj_a, w_proj_b, w_out, norm_mix_post,
           norm_mlp_pre, w_up, w_down, norm_mlp_post):
    depth = w_in.shape[0]
    yp, ys = x_prompt, x_sample
    outs = [[] for _ in range(8)]
    for l in range(depth):
        w = _layer_weights(norm_mix_pre[l], w_in[l], conv_w[l], w_proj_a[l], w_proj_b[l], w_out[l],
                           norm_mix_post[l], norm_mlp_pre[l], w_up[l], w_down[l], norm_mlp_post[l])
        yp, k1, v1, c1, s1 = _prompt_layer(yp, w, rel_bias[l], a_log[l], dt_bias[l], delta_norm_w[l])
        ys, k2, v2, c2, s2 = _sample_layer(ys, cache_attn_k[l], cache_attn_v[l], state_conv[l], state_delta[l],
                                           w, rel_bias[l], a_log[l], dt_bias[l], delta_norm_w[l])
        for lst, val in zip(outs, (k1, v1, c1, s1, k2, v2, c2, s2)):
            lst.append(val)
    return (yp, ys) + tuple(jnp.stack(o) for o in outs)
```

```python
import functools

import numpy as np
import jax
import jax.numpy as jnp
from jax import lax
from jax.experimental import pallas as pl
from jax.experimental.pallas import tpu as pltpu

F32 = jnp.float32
BF16 = jnp.bfloat16

D_MODEL = 1024
CHUNK = 64
RMS_EPS = 1e-6
L2_EPS = 1e-6
A_HEADS = 8
A_HEAD_DIM = 64
A_WIDTH = A_HEADS * A_HEAD_DIM
A_PREV_CHUNKS = 8
A_REACH = A_PREV_CHUNKS * CHUNK
MAX_REL = 128
B_HEADS = 4
B_DK = 128
B_DV = 128
B_KW = B_HEADS * B_DK
B_VW = B_HEADS * B_DV
B_CONV_CH = 2 * B_KW + B_VW
CONV_WIDTH = 4
D_FF = 4 * D_MODEL

LANES = 128
SUBLANES = 8
VMEM_LIMIT_BYTES = 56 * 1024 * 1024

Q_BLOCK = 256
N_KEY_BLOCKS = A_REACH // Q_BLOCK + 1
DELTA_CHUNK = 128
DELTA_BLOCK = 512
ROW_TILE = 512

assert B_DK == LANES and B_DV == LANES and B_HEADS % 2 == 0


def _dot(a, b):
    return jnp.dot(a, b, preferred_element_type=F32)


def _dot_nt(a, b):
    return lax.dot_general(a, b, (((1,), (1,)), ((), ())), preferred_element_type=F32)


def _sigmoid(x):
    return 1.0 / (1.0 + jnp.exp(-x))


def _rms(x, gain):
    ms = jnp.mean(x * x, axis=-1, keepdims=True)
    return x * lax.rsqrt(ms + RMS_EPS) * gain


def _params(sem):
    return pltpu.CompilerParams(dimension_semantics=sem, vmem_limit_bytes=VMEM_LIMIT_BYTES)


def _const_spec(shape):
    nd = len(shape)
    return pl.BlockSpec(shape, lambda *_: (0,) * nd, pipeline_mode=pl.Buffered(1))


def _conv_silu_l2norm(ext_ref, convw_ref, n):
    base = SUBLANES - (CONV_WIDTH - 1)
    acc = ext_ref[base:base + n, :] * convw_ref[0:1, :]
    for i in range(1, CONV_WIDTH):
        acc = acc + ext_ref[base + i:base + i + n, :] * convw_ref[i:i + 1, :]
    y = acc * _sigmoid(acc)
    qs, ks = [], []
    for h in range(B_HEADS):
        qh = y[:, h * B_DK:(h + 1) * B_DK]
        kh = y[:, B_KW + h * B_DK:B_KW + (h + 1) * B_DK]
        qs.append(qh * lax.rsqrt(jnp.sum(qh * qh, axis=-1, keepdims=True) + L2_EPS) * (B_DK ** -0.5))
        ks.append(kh * lax.rsqrt(jnp.sum(kh * kh, axis=-1, keepdims=True) + L2_EPS))
    return jnp.concatenate(qs, axis=1), jnp.concatenate(ks, axis=1), y[:, 2 * B_KW:]


def _inproj_kernel(*refs, prompt):
    x_ref, gain_ref, wq_ref, wk_ref, wv_ref, wcv_ref, wzab_ref, wg_ref = refs[:8]
    if prompt:
        (convw_ref, q_ref, kt_ref, v_ref, dq_ref, dk_ref, dv_ref, z_ref, ab_ref, g_ref,
         kf_ref, vf_ref, cvt_ref, ext_ref) = refs[8:]
    else:
        q_ref, kf_ref, vf_ref, cv_ref, z_ref, ab_ref, g_ref = refs[8:]
    x = x_ref[0]
    hb = _rms(x, gain_ref[...]).astype(BF16)
    tm = x.shape[0]

    q = _dot(hb, wq_ref[...])
    q_ref[0] = (q * (A_HEAD_DIM ** -0.5)).astype(BF16)

    k = _dot(hb, wk_ref[...])
    v = _dot(hb, wv_ref[...])
    kf_ref[0] = k
    vf_ref[0] = v
    if prompt:
        kt = k.T.astype(BF16)
        for j in range(tm // Q_BLOCK):
            kt_ref[0, j] = kt[:, j * Q_BLOCK:(j + 1) * Q_BLOCK]
        v_ref[0] = v.astype(BF16)

    cv = _dot(hb, wcv_ref[...])
    if prompt:
        @pl.when(pl.program_id(1) == 0)
        def _():
            ext_ref[0:SUBLANES, :] = jnp.zeros((SUBLANES, B_CONV_CH), F32)

        ext_ref[SUBLANES:SUBLANES + tm, :] = cv
        dq, dk, dv = _conv_silu_l2norm(ext_ref, convw_ref, tm)
        dq_ref[0] = dq.astype(BF16)
        dk_ref[0] = dk.astype(BF16)
        dv_ref[0] = dv.astype(BF16)
        ext_ref[0:SUBLANES, :] = cv[tm - SUBLANES:, :]
        cvt_ref[0] = cv[tm - SUBLANES:, :]
    else:
        cv_ref[0] = cv

    zab = _dot(hb, wzab_ref[...])
    z_ref[0] = zab[:, :B_VW].astype(BF16)
    ab_ref[0] = zab[:, B_VW:]

    g = _dot(hb, wg_ref[...])
    g_ref[0] = _sigmoid(g).astype(BF16)


def _inproj(x, gain, wq, wk, wv, wcv, wzab, wg, conv_w=None, *, prompt):
    b, t, d = x.shape
    tm = min(ROW_TILE, t)
    assert t % tm == 0
    row = lambda w, dt: (jax.ShapeDtypeStruct((b, t, w), dt), pl.BlockSpec((1, tm, w), lambda i, j: (i, j, 0)))
    stream = lambda r, w: (jax.ShapeDtypeStruct((b, r, w), F32), pl.BlockSpec((1, r, w), lambda i, j: (i, 0, 0)))
    outs = [row(A_WIDTH, BF16)]
    if prompt:
        assert tm == A_REACH and tm % Q_BLOCK == 0
        nq = tm // Q_BLOCK
        outs.append((jax.ShapeDtypeStruct((b, t // Q_BLOCK, A_WIDTH, Q_BLOCK), BF16),
                     pl.BlockSpec((1, nq, A_WIDTH, Q_BLOCK), lambda i, j: (i, j, 0, 0))))
        outs += [row(A_WIDTH, BF16), row(B_KW, BF16), row(B_KW, BF16), row(B_VW, BF16)]
    else:
        outs += [row(A_WIDTH, F32), row(A_WIDTH, F32), row(B_CONV_CH, F32)]
    outs += [row(B_VW, BF16), row(LANES, F32), row(2 * D_MODEL, BF16)]
    weights = [wq, wk, wv, wcv, wzab, wg]
    scratch = []
    if prompt:
        outs += [stream(tm, A_WIDTH), stream(tm, A_WIDTH), stream(SUBLANES, B_CONV_CH)]
        weights.append(conv_w)
        scratch.append(pltpu.VMEM((tm + SUBLANES, B_CONV_CH), F32))
    in_specs = [pl.BlockSpec((1, tm, d), lambda i, j: (i, j, 0)), _const_spec(gain.shape)]
    in_specs += [_const_spec(w.shape) for w in weights]
    return pl.pallas_call(
        functools.partial(_inproj_kernel, prompt=prompt),
        grid=(b, t // tm),
        in_specs=in_specs,
        out_specs=[o[1] for o in outs],
        out_shape=[o[0] for o in outs],
        scratch_shapes=scratch,
        compiler_params=_params(("arbitrary", "arbitrary")),
        name="inproj_prompt" if prompt else "inproj_sample",
    )(x, gain, *weights)


def _conv_sample_kernel(cv_ref, prev_ref, convw_ref, dq_ref, dk_ref, dv_ref, ext_ref):
    n = cv_ref.shape[1]
    ext_ref[0:SUBLANES, :] = prev_ref[0]
    ext_ref[SUBLANES:SUBLANES + n, :] = cv_ref[0]
    dq, dk, dv = _conv_silu_l2norm(ext_ref, convw_ref, n)
    dq_ref[0] = dq.astype(BF16)
    dk_ref[0] = dk.astype(BF16)
    dv_ref[0] = dv.astype(BF16)


def _conv_sample(cv, prev, conv_w):
    b, t, _ = cv.shape
    blk = lambda r, w: pl.BlockSpec((1, r, w), lambda i: (i, 0, 0))
    out = jax.ShapeDtypeStruct((b, t, B_KW), BF16)
    return pl.pallas_call(
        _conv_sample_kernel,
        grid=(b,),
        in_specs=[blk(t, B_CONV_CH), blk(SUBLANES, B_CONV_CH), _const_spec(conv_w.shape)],
        out_specs=[blk(t, B_KW)] * 3,
        out_shape=[out] * 3,
        scratch_shapes=[pltpu.VMEM((t + SUBLANES, B_CONV_CH), F32)],
        compiler_params=_params(("arbitrary",)),
        name="conv_sample",
    )(cv, prev, conv_w)


def _toeplitz_bias(rel_bias, rows, width, offset):
    pitch = width + rows
    period = pitch + 1
    j = np.arange(period)
    rel = np.where(j < width, offset - j, offset + period - j)
    idx = np.clip(rel, -MAX_REL, MAX_REL) + MAX_REL
    vec = rel_bias.astype(F32)[:, idx]
    tiled = jnp.tile(vec, (1, rows))[:, :rows * pitch]
    return tiled.reshape(rel_bias.shape[0], rows, pitch)[:, :, :width]


def _prompt_bias_table(rel_bias):
    width = N_KEY_BLOCKS * Q_BLOCK
    bias = _toeplitz_bias(rel_bias, Q_BLOCK, width, A_REACH)
    tc = np.arange(Q_BLOCK)[:, None] // CHUNK
    cc = np.arange(width)[None, :] // CHUNK
    band = (cc >= tc) & (cc <= tc + A_PREV_CHUNKS)
    bias = jnp.where(band[None], bias, -jnp.inf)
    bias = bias.reshape(A_HEADS, Q_BLOCK, N_KEY_BLOCKS, Q_BLOCK).transpose(0, 2, 1, 3)
    masked = jnp.full((A_HEADS, 1, Q_BLOCK, Q_BLOCK), -jnp.inf, F32)
    return jnp.concatenate([bias, masked], axis=1)


def _softmax_pv(scores, values):
    m = None
    for s in scores:
        mj = jnp.max(s, axis=-1, keepdims=True)
        m = mj if m is None else jnp.maximum(m, mj)
    denom = None
    acc = None
    for s, vals in zip(scores, values):
        p = jnp.exp(s - m)
        dj = jnp.sum(p, axis=-1, keepdims=True)
        denom = dj if denom is None else denom + dj
        pv = _dot(p.astype(BF16), vals)
        acc = pv if acc is None else acc + pv
    return acc / denom


def _attn_prompt_kernel(q_ref, kt_ref, v_ref, bias_ref, o_ref):
    qi = pl.program_id(1)
    lane = lax.broadcasted_iota(jnp.int32, (Q_BLOCK, LANES), 1)
    low = lane < A_HEAD_DIM
    for hp in range(A_HEADS // 2):
        cols = slice(hp * LANES, (hp + 1) * LANES)
        qp = q_ref[0, :, cols]
        pair_out = []
        for e in range(2):
            h = 2 * hp + e
            qm = jnp.where(low if e == 0 else jnp.logical_not(low), qp, jnp.zeros_like(qp))
            scores, values = [], []
            for j in range(N_KEY_BLOCKS):
                kraw = qi - (N_KEY_BLOCKS - 1) + j
                kb = jnp.maximum(kraw, 0)
                bsel = jnp.where(kraw >= 0, j, N_KEY_BLOCKS)
                scores.append(_dot(qm, kt_ref[0, kb, cols, :]) + bias_ref[h, bsel])
                values.append(v_ref[0, pl.ds(pl.multiple_of(kb * Q_BLOCK, Q_BLOCK), Q_BLOCK), cols])
            pair_out.append(_softmax_pv(scores, values))
        o_ref[0, :, cols] = jnp.where(low, pair_out[0], pair_out[1]).astype(BF16)


def _attn_prompt(q, kt, v, bias):
    b, t, _ = q.shape
    return pl.pallas_call(
        _attn_prompt_kernel,
        grid=(b, t // Q_BLOCK),
        in_specs=[
            pl.BlockSpec((1, Q_BLOCK, A_WIDTH), lambda i, j: (i, j, 0)),
            pl.BlockSpec((1, t // Q_BLOCK, A_WIDTH, Q_BLOCK), lambda i, j: (i, 0, 0, 0)),
            pl.BlockSpec((1, t, A_WIDTH), lambda i, j: (i, 0, 0)),
            _const_spec(bias.shape),
        ],
        out_specs=pl.BlockSpec((1, Q_BLOCK, A_WIDTH), lambda i, j: (i, j, 0)),
        out_shape=jax.ShapeDtypeStruct((b, t, A_WIDTH), BF16),
        compiler_params=_params(("arbitrary", "arbitrary")),
        name="attn_prompt",
    )(q, kt, v, bias)


def _attn_sample_kernel(q_ref, ck_ref, cv_ref, nk_ref, nv_ref, bc_ref, bn_ref, o_ref):
    rows = q_ref.shape[1]
    lane = lax.broadcasted_iota(jnp.int32, (rows, LANES), 1)
    low = lane < A_HEAD_DIM
    for hp in range(A_HEADS // 2):
        cols = slice(hp * LANES, (hp + 1) * LANES)
        qp = q_ref[0, :, cols]
        ck = ck_ref[0, :, cols].astype(BF16)
        cv = cv_ref[0, :, cols].astype(BF16)
        nk = nk_ref[0, :, cols].astype(BF16)
        nv = nv_ref[0, :, cols].astype(BF16)
        pair_out = []
        for e in range(2):
            h = 2 * hp + e
            qm = jnp.where(low if e == 0 else jnp.logical_not(low), qp, jnp.zeros_like(qp))
            scores = [_dot_nt(qm, ck) + bc_ref[h], _dot_nt(qm, nk) + bn_ref[h]]
            pair_out.append(_softmax_pv(scores, [cv, nv]))
        o_ref[0, :, cols] = jnp.where(low, pair_out[0], pair_out[1]).astype(BF16)


def _attn_sample(q, cache_k, cache_v, new_k, new_v, rel_bias):
    b, t, _ = q.shape
    c = cache_k.shape[1]
    bias = _toeplitz_bias(rel_bias, t, c + LANES, c)
    bias = jnp.where((np.arange(c + LANES) < c + t)[None, None, :], bias, -jnp.inf)
    bias_c, bias_n = bias[:, :, :c], bias[:, :, c:]
    pad = ((0, 0), (0, LANES - t), (0, 0))
    nk, nv = jnp.pad(new_k, pad), jnp.pad(new_v, pad)
    blk = lambda r: pl.BlockSpec((1, r, A_WIDTH), lambda i: (i, 0, 0))
    return pl.pallas_call(
        _attn_sample_kernel,
        grid=(b,),
        in_specs=[blk(t), blk(c), blk(c), blk(LANES), blk(LANES),
                  _const_spec(bias_c.shape), _const_spec(bias_n.shape)],
        out_specs=blk(t),
        out_shape=jax.ShapeDtypeStruct((b, t, A_WIDTH), BF16),
        compiler_params=_params(("arbitrary",)),
        name="attn_sample",
    )(q, cache_k, cache_v, nk, nv, bias_c, bias_n)


def _block_diag(x2):
    half = x2.shape[1] // 2
    z = jnp.zeros_like(x2[:, :half])
    top = jnp.concatenate([x2[:, :half], z], axis=1)
    bot = jnp.concatenate([z, x2[:, half:]], axis=1)
    return jnp.concatenate([top, bot], axis=0)


def _delta_kernel(q_ref, k_ref, v_ref, z_ref, ab_ref, s0_ref, alog_ref, dtb_ref, nw_ref,
                  o_ref, s_ref, rows_ref, *, n_valid):
    L = DELTA_CHUNK
    tb = q_ref.shape[1]
    nch = tb // L
    n_pairs = B_HEADS // 2
    n_levels = L.bit_length() - 2

    @pl.when(pl.program_id(1) == 0)
    def _():
        s_ref[0] = s0_ref[0]

    ab_t = ab_ref[0].T
    rows8 = ab_t[0:2 * B_HEADS, :]
    x = rows8 + dtb_ref[:, 0:1]
    softplus = jnp.maximum(x, 0.0) + jnp.log(1.0 + jnp.exp(-jnp.abs(x)))
    g8 = -jnp.exp(alog_ref[:, 0:1]) * softplus
    beta8 = _sigmoid(rows8)
    tok = lax.broadcasted_iota(jnp.int32, (2 * B_HEADS, tb), 1)
    if n_valid < tb:
        g8 = jnp.where(tok < n_valid, g8, 0.0)
        beta8 = jnp.where(tok < n_valid, beta8, 0.0)
    pos = tok & (L - 1)
    gc8 = g8
    shift = 1
    while shift < L:
        gc8 = gc8 + jnp.where(pos >= shift, pltpu.roll(gc8, shift, 1), 0.0)
        shift *= 2
    gl8 = [gc8[:, (c + 1) * L - 1:(c + 1) * L] for c in range(nch)]
    glb8 = jnp.concatenate([jnp.broadcast_to(g, (2 * B_HEADS, L)) for g in gl8], axis=1)
    rows_ref[...] = jnp.zeros_like(rows_ref)
    rows_ref[0:8, :] = gc8
    rows_ref[8:16, :] = beta8
    rows_ref[16:24, :] = jnp.exp(gc8)
    rows_ref[24:32, :] = jnp.exp(glb8 - gc8)
    cols = rows_ref[...].T

    ri = lax.broadcasted_iota(jnp.int32, (L, 2 * L), 0)
    ci = lax.broadcasted_iota(jnp.int32, (L, 2 * L), 1) & (L - 1)
    tril = ri >= ci
    strict = ri > ci
    eye = (ri == ci).astype(F32)

    chains = [(c, p) for c in range(nch) for p in range(n_pairs)]

    def pair_cols(c, i):
        r = slice(c * L, (c + 1) * L)
        return [jnp.broadcast_to(cols[r, i + e:i + e + 1], (L, L)) for e in range(2)]

    st = {}
    for c, p in chains:
        r = slice(c * L, (c + 1) * L)
        ls = slice(p * 2 * L, (p + 1) * 2 * L)
        h0 = 2 * p
        q2 = q_ref[0, r, ls]
        k2 = k_ref[0, r, ls]
        gc_c = jnp.concatenate(pair_cols(c, h0), axis=1)
        beta_c = pair_cols(c, 8 + B_HEADS + h0)
        egc_c = pair_cols(c, 16 + h0)
        edec_c = pair_cols(c, 24 + h0)
        gc_r = jnp.concatenate([jnp.broadcast_to(gc8[h0 + e:h0 + e + 1, r], (L, L)) for e in range(2)], axis=1)
        decay = jnp.exp(jnp.where(tril, gc_c - gc_r, -jnp.inf))
        sc = _dot_nt(jnp.concatenate([q2, k2], axis=0), _block_diag(k2))
        m = jnp.where(strict, -(jnp.concatenate(beta_c, axis=1) * sc[L:] * decay), 0.0)
        st[c, p] = dict(k2=k2.astype(F32), q2=q2.astype(F32), v2=v_ref[0, r, ls].astype(F32),
                        beta=beta_c, egc=egc_c, edec=edec_c, qkd=sc[:L] * decay, t=eye + m, m=m)

    for key in chains:
        mb = st[key]["m"].astype(BF16)
        st[key]["p"] = _dot(mb, _block_diag(mb))
    for lvl in range(n_levels):
        last = lvl == n_levels - 1
        for key in chains:
            d = st[key]
            pb = d["p"].astype(BF16)
            tb16 = d["t"].astype(BF16)
            if last:
                d["t"] = d["t"] + _dot(tb16, _block_diag(pb))
            else:
                res = _dot(jnp.concatenate([tb16, pb], axis=0), _block_diag(pb))
                d["t"] = d["t"] + res[:L]
                d["p"] = res[L:]

    for key in chains:
        d = st[key]
        t16 = d["t"].astype(BF16)
        qkd16 = d["qkd"].astype(BF16)
        phi, psi, qt, oin = [], [], [], []
        for e in range(2):
            hs = slice(e * L, (e + 1) * L)
            kh = d["k2"][:, hs]
            rhs = jnp.concatenate([d["v2"][:, hs] * d["beta"][e], kh * (d["beta"][e] * d["egc"][e])], axis=1)
            sol = _dot(t16[:, hs], rhs.astype(BF16)).astype(BF16)
            kdt = (kh * d["edec"][e]).T.astype(BF16)
            r2 = _dot(jnp.concatenate([qkd16[:, hs], kdt], axis=0), sol)
            oin.append(r2[:L, :B_DV])
            qt.append(d["q2"][:, hs] * d["egc"][e] - r2[:L, B_DV:])
            psi.append(r2[L:, :B_DV])
            phi.append(-r2[L:, B_DV:])
        d["lhs"] = jnp.concatenate([jnp.concatenate(phi, axis=1), jnp.concatenate(qt, axis=1)], axis=0).astype(BF16)
        d["psi"] = jnp.concatenate(psi, axis=1)
        d["oin"] = jnp.concatenate(oin, axis=1)

    nw = nw_ref[...]
    for p in range(n_pairs):
        ls = slice(p * 2 * L, (p + 1) * 2 * L)
        s2 = s_ref[0, :, ls]
        for c in range(nch):
            d = st[c, p]
            r = slice(c * L, (c + 1) * L)
            res = _dot(d["lhs"], _block_diag(s2.astype(BF16)))
            o2 = res[L:] + d["oin"]
            egl = jnp.concatenate([jnp.broadcast_to(jnp.exp(gl8[c][2 * p + e:2 * p + e + 1, :]), (1, B_DV))
                                   for e in range(2)], axis=1)
            s2 = s2 * egl + res[:L] + d["psi"]
            for e in range(2):
                hs = slice((2 * p + e) * B_DV, (2 * p + e + 1) * B_DV)
                zh = z_ref[0, r, hs].astype(F32)
                o_ref[0, r, hs] = (_rms(o2[:, e * B_DV:(e + 1) * B_DV], nw) * (zh * _sigmoid(zh))).astype(BF16)
        s_ref[0, :, ls] = s2


def _delta(q, k, v, z, ab, s0, a_log, dt_bias, norm_w, *, block, n_valid):
    b, t, _ = q.shape
    assert t % block == 0 and block % DELTA_CHUNK == 0
    alog8 = jnp.broadcast_to(jnp.tile(a_log.astype(F32), 2)[:, None], (2 * B_HEADS, LANES))
    dtb8 = jnp.broadcast_to(jnp.tile(dt_bias.astype(F32), 2)[:, None], (2 * B_HEADS, LANES))
    row = lambda w: pl.BlockSpec((1, block, w), lambda i, j: (i, j, 0))
    state = pl.BlockSpec((1, B_DK, B_HEADS * B_DV), lambda i, j: (i, 0, 0))
    s0 = s0.astype(F32).transpose(0, 2, 1, 3).reshape(b, B_DK, B_HEADS * B_DV)
    o, s = pl.pallas_call(
        functools.partial(_delta_kernel, n_valid=n_valid),
        grid=(b, t // block),
        in_specs=[row(B_KW), row(B_KW), row(B_VW), row(B_VW), row(LANES), state,
                  _const_spec(alog8.shape), _const_spec(dtb8.shape), _const_spec((1, B_DV))],
        out_specs=[row(B_VW), state],
        out_shape=[jax.ShapeDtypeStruct((b, t, B_VW), BF16),
                   jax.ShapeDtypeStruct((b, B_DK, B_HEADS * B_DV), F32)],
        scratch_shapes=[pltpu.VMEM((LANES, block), F32)],
        compiler_params=_params(("arbitrary", "arbitrary")),
        name="delta_%d" % block,
    )(q, k, v, z, ab, s0, alog8, dtb8, norm_w.reshape(1, B_DV).astype(F32))
    return o, s.reshape(b, B_DK, B_HEADS, B_DV).transpose(0, 2, 1, 3)


def _merge_kernel(x_ref, oa_ref, ob_ref, g_ref, wpa_ref, wpb_ref, wout_ref, gain_ref, o_ref):
    ya = _dot(oa_ref[...], wpa_ref[...])
    yb = _dot(ob_ref[...], wpb_ref[...])
    merged = g_ref[:, :D_MODEL].astype(F32) * ya + g_ref[:, D_MODEL:].astype(F32) * yb
    y = _dot(merged.astype(BF16), wout_ref[...])
    o_ref[...] = x_ref[...] + _rms(y, gain_ref[...])


def _merge(x, oa, ob, gates, wpa, wpb, wout, gain):
    n, d = x.shape
    tm = min(ROW_TILE, n)
    row = lambda w: pl.BlockSpec((tm, w), lambda i: (i, 0))
    return pl.pallas_call(
        _merge_kernel,
        grid=(n // tm,),
        in_specs=[row(d), row(A_WIDTH), row(B_VW), row(2 * d),
                  _const_spec(wpa.shape), _const_spec(wpb.shape), _const_spec(wout.shape), _const_spec(gain.shape)],
        out_specs=row(d),
        out_shape=jax.ShapeDtypeStruct((n, d), F32),
        compiler_params=_params(("arbitrary",)),
        name="merge_%d" % n,
    )(x, oa, ob, gates, wpa, wpb, wout, gain)


FF_SPLIT = 4


def _mlp_kernel(x_ref, gpre_ref, wup_ref, wdown_ref, gpost_ref, o_ref):
    x = x_ref[...]
    hb = _rms(x, gpre_ref[...]).astype(BF16)
    wcols = D_FF // FF_SPLIT
    f = None
    for c in range(FF_SPLIT):
        cs = slice(c * wcols, (c + 1) * wcols)
        up = jnp.maximum(_dot(hb, wup_ref[:, cs]), 0.0)
        part = _dot((up * up).astype(BF16), wdown_ref[cs, :])
        f = part if f is None else f + part
    o_ref[...] = x + _rms(f, gpost_ref[...])


def _mlp(x, gpre, wup, wdown, gpost):
    n, d = x.shape
    tm = min(ROW_TILE, n)
    row = pl.BlockSpec((tm, d), lambda i: (i, 0))
    return pl.pallas_call(
        _mlp_kernel,
        grid=(n // tm,),
        in_specs=[row, _const_spec(gpre.shape), _const_spec(wup.shape), _const_spec(wdown.shape),
                  _const_spec(gpost.shape)],
        out_specs=row,
        out_shape=jax.ShapeDtypeStruct((n, d), F32),
        compiler_params=_params(("arbitrary",)),
        name="mlp_%d" % n,
    )(x, gpre, wup, wdown, gpost)


def _layer_weights(norm_mix_pre, w_in, conv_w, w_proj_a, w_proj_b, w_out, norm_mix_post,
                   norm_mlp_pre, w_up, w_down, norm_mlp_post):
    o_cv = 3 * A_WIDTH
    o_z = o_cv + B_CONV_CH
    o_ab = o_z + B_VW
    o_g = o_ab + 2 * B_HEADS
    wb = w_in.astype(BF16)
    wzab = jnp.concatenate([wb[:, o_z:o_g], jnp.zeros((D_MODEL, LANES - 2 * B_HEADS), BF16)], axis=1)
    vec = lambda g: g.reshape(1, -1).astype(F32)
    return dict(
        gain_in=vec(norm_mix_pre),
        wq=wb[:, :A_WIDTH], wk=wb[:, A_WIDTH:2 * A_WIDTH], wv=wb[:, 2 * A_WIDTH:o_cv],
        wcv=wb[:, o_cv:o_z], wzab=wzab, wg=wb[:, o_g:],
        conv_w=conv_w.astype(F32),
        wpa=w_proj_a.astype(BF16), wpb=w_proj_b.astype(BF16), wout=w_out.astype(BF16),
        gain_mix=vec(norm_mix_post), gain_pre=vec(norm_mlp_pre), gain_post=vec(norm_mlp_post),
        wup=w_up.astype(BF16), wdown=w_down.astype(BF16),
    )


def _tail(x, w, oa, ob, gates):
    b, t, d = x.shape
    x1 = _merge(x.reshape(b * t, d), oa.reshape(b * t, A_WIDTH), ob.reshape(b * t, B_VW),
                gates.reshape(b * t, 2 * d), w["wpa"], w["wpb"], w["wout"], w["gain_mix"])
    return _mlp(x1, w["gain_pre"], w["wup"], w["wdown"], w["gain_post"]).reshape(b, t, d)


def _prompt_layer(x, w, rel_bias, a_log, dt_bias, delta_norm_w):
    b, t, _ = x.shape
    q, kt, v, dq, dk, dv, z, ab, gates, kf, vf, cvt = _inproj(
        x, w["gain_in"], w["wq"], w["wk"], w["wv"], w["wcv"], w["wzab"], w["wg"], w["conv_w"], prompt=True)
    oa = _attn_prompt(q, kt, v, _prompt_bias_table(rel_bias))
    s0 = jnp.zeros((b, B_HEADS, B_DK, B_DV), F32)
    ob, s_new = _delta(dq, dk, dv, z, ab, s0, a_log, dt_bias, delta_norm_w,
                       block=DELTA_BLOCK, n_valid=DELTA_BLOCK)
    y = _tail(x, w, oa, ob, gates)
    keep = min(A_REACH, t)
    new_k = kf.reshape(b, keep, A_HEADS, A_HEAD_DIM)
    new_v = vf.reshape(b, keep, A_HEADS, A_HEAD_DIM)
    new_conv = cvt[:, SUBLANES - (CONV_WIDTH - 1):, :]
    return y, new_k, new_v, new_conv, s_new


def _sample_layer(x, cache_k, cache_v, conv_state, delta_state, w, rel_bias, a_log, dt_bias, delta_norm_w):
    b, t, d = x.shape
    c = cache_k.shape[1]
    q, k, v, cv, z, ab, gates = _inproj(
        x.reshape(1, b * t, d), w["gain_in"], w["wq"], w["wk"], w["wv"], w["wcv"], w["wzab"], w["wg"],
        prompt=False)
    per_stream = lambda a: a.reshape(b, t, a.shape[-1])
    q, k, v, cv, z, ab, gates = map(per_stream, (q, k, v, cv, z, ab, gates))
    oa = _attn_sample(q, cache_k.reshape(b, c, A_WIDTH), cache_v.reshape(b, c, A_WIDTH), k, v, rel_bias)
    prev = jnp.pad(conv_state.astype(F32), ((0, 0), (SUBLANES - (CONV_WIDTH - 1), 0), (0, 0)))
    dq, dk, dv = _conv_sample(cv, prev, w["conv_w"])
    pad = lambda a: jnp.pad(a, ((0, 0), (0, DELTA_CHUNK - t), (0, 0)))
    ob, s_new = _delta(pad(dq), pad(dk), pad(dv), pad(z), pad(ab), delta_state, a_log, dt_bias, delta_norm_w,
                       block=DELTA_CHUNK, n_valid=t)
    y = _tail(x, w, oa, ob[:, :t], gates)
    new_conv = jnp.concatenate([conv_state.astype(F32), cv], axis=1)[:, -(CONV_WIDTH - 1):]
    return (y, k.reshape(b, t, A_HEADS, A_HEAD_DIM), v.reshape(b, t, A_HEADS, A_HEAD_DIM), new_conv, s_new)


def kernel(x_prompt, x_sample, cache_attn_k, cache_attn_v, state_conv, state_delta, norm_mix_pre, w_in,
           rel_bias, conv_w, a_log, dt_bias, delta_norm_w, w_proj_a, w_proj_b, w_out, norm_mix_post,
           norm_mlp_pre, w_up, w_down, norm_mlp_post):
    depth = w_in.shape[0]
    yp, ys = x_prompt, x_sample
    outs = [[] for _ in range(8)]
    for l in range(depth):
        w = _layer_weights(norm_mix_pre[l], w_in[l], conv_w[l], w_proj_a[l], w_proj_b[l], w_out[l],
                           norm_mix_post[l], norm_mlp_pre[l], w_up[l], w_down[l], norm_mlp_post[l])
        yp, k1, v1, c1, s1 = _prompt_layer(yp, w, rel_bias[l], a_log[l], dt_bias[l], delta_norm_w[l])
        ys, k2, v2, c2, s2 = _sample_layer(ys, cache_attn_k[l], cache_attn_v[l], state_conv[l], state_delta[l],
                                           w, rel_bias[l], a_log[l], dt_bias[l], delta_norm_w[l])
        for lst, val in zip(outs, (k1, v1, c1, s1, k2, v2, c2, s2)):
            lst.append(val)
    return (yp, ys) + tuple(jnp.stack(o) for o in outs)
```

```python
import functools

import numpy as np
import jax
import jax.numpy as jnp
from jax import lax
from jax.experimental import pallas as pl
from jax.experimental.pallas import tpu as pltpu

F32 = jnp.float32
BF16 = jnp.bfloat16

D_MODEL = 1024
CHUNK = 64
RMS_EPS = 1e-6
L2_EPS = 1e-6
A_HEADS = 8
A_HEAD_DIM = 64
A_WIDTH = A_HEADS * A_HEAD_DIM
A_PREV_CHUNKS = 8
A_REACH = A_PREV_CHUNKS * CHUNK
MAX_REL = 128
B_HEADS = 4
B_DK = 128
B_DV = 128
B_KW = B_HEADS * B_DK
B_VW = B_HEADS * B_DV
B_CONV_CH = 2 * B_KW + B_VW
CONV_WIDTH = 4
D_FF = 4 * D_MODEL

LANES = 128
SUBLANES = 8
VMEM_LIMIT_BYTES = 56 * 1024 * 1024

Q_BLOCK = 256
N_KEY_BLOCKS = A_REACH // Q_BLOCK + 1
SCORE_LOOKAHEAD = 2
DELTA_CHUNK = 128
DELTA_BLOCK = 512
ROW_TILE = 512
CONV_ROWS = 64
MERGE_ROWS = 1024

assert B_DK == LANES and B_DV == LANES and B_HEADS % 2 == 0


def _dot(a, b):
    return jnp.dot(a, b, preferred_element_type=F32)


def _dot_nt(a, b):
    return lax.dot_general(a, b, (((1,), (1,)), ((), ())), preferred_element_type=F32)


def _sigmoid(x):
    return 0.5 * jnp.tanh(0.5 * x) + 0.5


def _rms(x, gain):
    ms = jnp.mean(x * x, axis=-1, keepdims=True)
    return x * lax.rsqrt(ms + RMS_EPS) * gain


def _params(sem):
    return pltpu.CompilerParams(dimension_semantics=sem, vmem_limit_bytes=VMEM_LIMIT_BYTES)


def _const_spec(shape):
    nd = len(shape)
    return pl.BlockSpec(shape, lambda *_: (0,) * nd, pipeline_mode=pl.Buffered(1))


def _conv_silu_l2norm(hist, x, convw_ref):
    n = x.shape[0]
    full = jnp.concatenate([hist, x], axis=0)
    rb = min(CONV_ROWS, n)
    qs, ks, vs = [], [], []
    for r0 in range(0, n, rb):
        piece = full[r0:r0 + rb + SUBLANES, :]
        z = piece * convw_ref[0:1, :]
        for i in range(1, CONV_WIDTH):
            z = pltpu.roll(z, 1, 0) + piece * convw_ref[i:i + 1, :]
        acc = z[SUBLANES:, :]
        y = acc * _sigmoid(acc)
        qh, kh = [], []
        for h in range(B_HEADS):
            qv = y[:, h * B_DK:(h + 1) * B_DK]
            kv = y[:, B_KW + h * B_DK:B_KW + (h + 1) * B_DK]
            qh.append(qv * (lax.rsqrt(jnp.sum(qv * qv, axis=-1, keepdims=True) + L2_EPS) * (B_DK ** -0.5)))
            kh.append(kv * lax.rsqrt(jnp.sum(kv * kv, axis=-1, keepdims=True) + L2_EPS))
        qs.append(jnp.concatenate(qh, axis=1))
        ks.append(jnp.concatenate(kh, axis=1))
        vs.append(y[:, 2 * B_KW:])
    cat = lambda parts: parts[0] if len(parts) == 1 else jnp.concatenate(parts, axis=0)
    return cat(qs), cat(ks), cat(vs)


def _inproj_kernel(*refs, prompt):
    x_ref, gain_ref, wq_ref, wk_ref, wv_ref, wcv_ref, wzab_ref, wg_ref = refs[:8]
    if prompt:
        (convw_ref, q_ref, k_ref, vt_ref, dq_ref, dk_ref, dv_ref, z_ref, ab_ref, g_ref,
         kf_ref, vf_ref, cvt_ref, hist_ref) = refs[8:]
    else:
        q_ref, kf_ref, vf_ref, cv_ref, z_ref, ab_ref, g_ref = refs[8:]
    if prompt:
        @pl.when(pl.program_id(1) == 0)
        def _():
            hist_ref[...] = jnp.zeros((SUBLANES, B_CONV_CH), F32)

    x = x_ref[0]
    hb = _rms(x, gain_ref[...]).astype(BF16)
    tm = x.shape[0]

    cv = _dot(hb, wcv_ref[...])
    g = _dot(hb, wg_ref[...])
    if prompt:
        dq, dk, dv = _conv_silu_l2norm(hist_ref[...], cv, convw_ref)
        dq_ref[0] = dq.astype(BF16)
        dk_ref[0] = dk.astype(BF16)
        dv_ref[0] = dv.astype(BF16)
        hist_ref[...] = cv[tm - SUBLANES:, :]
        cvt_ref[0] = cv[tm - SUBLANES:, :]
    else:
        cv_ref[0] = cv

    q = _dot(hb, wq_ref[...]) * (A_HEAD_DIM ** -0.5)
    k = _dot(hb, wk_ref[...])
    v = _dot(hb, wv_ref[...])
    kf_ref[0] = k
    vf_ref[0] = v
    if prompt:
        qt = q.T.astype(BF16)
        vt = v.T.astype(BF16)
        for j in range(tm // Q_BLOCK):
            q_ref[0, j] = qt[:, j * Q_BLOCK:(j + 1) * Q_BLOCK]
            vt_ref[0, j] = vt[:, j * Q_BLOCK:(j + 1) * Q_BLOCK]
        k_ref[0] = k.astype(BF16)
    else:
        q_ref[0] = q.astype(BF16)

    zab = _dot(hb, wzab_ref[...])
    z_ref[0] = zab[:, :B_VW].astype(BF16)
    ab_ref[0] = zab[:, B_VW:]

    g_ref[0] = _sigmoid(g).astype(BF16)


def _inproj(x, gain, wq, wk, wv, wcv, wzab, wg, conv_w=None, *, prompt):
    b, t, d = x.shape
    tm = min(ROW_TILE, t)
    assert t % tm == 0
    row = lambda w, dt: (jax.ShapeDtypeStruct((b, t, w), dt), pl.BlockSpec((1, tm, w), lambda i, j: (i, j, 0)))
    stream = lambda r, w: (jax.ShapeDtypeStruct((b, r, w), F32), pl.BlockSpec((1, r, w), lambda i, j: (i, 0, 0)))
    if prompt:
        assert tm == A_REACH and tm % Q_BLOCK == 0
        nq = tm // Q_BLOCK
        feat = (jax.ShapeDtypeStruct((b, t // Q_BLOCK, A_WIDTH, Q_BLOCK), BF16),
                pl.BlockSpec((1, nq, A_WIDTH, Q_BLOCK), lambda i, j: (i, j, 0, 0)))
        outs = [feat, row(A_WIDTH, BF16), feat, row(B_KW, BF16), row(B_KW, BF16), row(B_VW, BF16)]
    else:
        outs = [row(A_WIDTH, BF16), row(A_WIDTH, F32), row(A_WIDTH, F32), row(B_CONV_CH, F32)]
    outs += [row(B_VW, BF16), row(LANES, F32), row(2 * D_MODEL, BF16)]
    weights = [wq, wk, wv, wcv, wzab, wg]
    scratch = []
    if prompt:
        outs += [stream(tm, A_WIDTH), stream(tm, A_WIDTH), stream(SUBLANES, B_CONV_CH)]
        weights.append(conv_w)
        scratch.append(pltpu.VMEM((SUBLANES, B_CONV_CH), F32))
    in_specs = [pl.BlockSpec((1, tm, d), lambda i, j: (i, j, 0)), _const_spec(gain.shape)]
    in_specs += [_const_spec(w.shape) for w in weights]
    return pl.pallas_call(
        functools.partial(_inproj_kernel, prompt=prompt),
        grid=(b, t // tm),
        in_specs=in_specs,
        out_specs=[o[1] for o in outs],
        out_shape=[o[0] for o in outs],
        scratch_shapes=scratch,
        compiler_params=_params(("arbitrary", "arbitrary")),
        name="inproj_prompt" if prompt else "inproj_sample",
    )(x, gain, *weights)


def _conv_sample_kernel(cv_ref, prev_ref, convw_ref, dq_ref, dk_ref, dv_ref):
    dq, dk, dv = _conv_silu_l2norm(prev_ref[0], cv_ref[0], convw_ref)
    dq_ref[0] = dq.astype(BF16)
    dk_ref[0] = dk.astype(BF16)
    dv_ref[0] = dv.astype(BF16)


def _conv_sample(cv, prev, conv_w):
    b, t, _ = cv.shape
    blk = lambda r, w: pl.BlockSpec((1, r, w), lambda i: (i, 0, 0))
    out = jax.ShapeDtypeStruct((b, t, B_KW), BF16)
    return pl.pallas_call(
        _conv_sample_kernel,
        grid=(b,),
        in_specs=[blk(t, B_CONV_CH), blk(SUBLANES, B_CONV_CH), _const_spec(conv_w.shape)],
        out_specs=[blk(t, B_KW)] * 3,
        out_shape=[out] * 3,
        compiler_params=_params(("arbitrary",)),
        name="conv_sample",
    )(cv, prev, conv_w)


def _toeplitz_bias(rel_bias, rows, width, offset):
    pitch = width + rows
    period = pitch + 1
    j = np.arange(period)
    rel = np.where(j < width, offset - j, offset + period - j)
    idx = np.clip(rel, -MAX_REL, MAX_REL) + MAX_REL
    vec = rel_bias.astype(F32)[:, idx]
    tiled = jnp.tile(vec, (1, rows))[:, :rows * pitch]
    return tiled.reshape(rel_bias.shape[0], rows, pitch)[:, :, :width]


def _prompt_bias_table(rel_bias):
    width = N_KEY_BLOCKS * Q_BLOCK
    bias = _toeplitz_bias(rel_bias, Q_BLOCK, width, A_REACH)
    tc = np.arange(Q_BLOCK)[:, None] // CHUNK
    cc = np.arange(width)[None, :] // CHUNK
    band = (cc >= tc) & (cc <= tc + A_PREV_CHUNKS)
    bias = jnp.where(band[None], bias, -jnp.inf)
    bias = bias.transpose(0, 2, 1).reshape(A_HEADS, N_KEY_BLOCKS, Q_BLOCK, Q_BLOCK)
    masked = jnp.full((A_HEADS, 1, Q_BLOCK, Q_BLOCK), -jnp.inf, F32)
    return jnp.concatenate([bias, masked], axis=1)


def _softmax_pv(scores, values):
    m = None
    for s in scores:
        mj = jnp.max(s, axis=-1, keepdims=True)
        m = mj if m is None else jnp.maximum(m, mj)
    denom = None
    acc = None
    for s, vals in zip(scores, values):
        p = jnp.exp(s - m)
        dj = jnp.sum(p, axis=-1, keepdims=True)
        denom = dj if denom is None else denom + dj
        pv = _dot(p.astype(BF16), vals)
        acc = pv if acc is None else acc + pv
    return acc / denom


def _attn_prompt_kernel(qt_ref, k_ref, vt_ref, bias_ref, o_ref):
    qi = pl.program_id(1)
    row = lax.broadcasted_iota(jnp.int32, (LANES, Q_BLOCK), 0)
    low = row < A_HEAD_DIM
    kbs, bsels = [], []
    for j in range(N_KEY_BLOCKS):
        kraw = qi - (N_KEY_BLOCKS - 1) + j
        kbs.append(jnp.maximum(kraw, 0))
        bsels.append(jnp.where(kraw >= 0, j, N_KEY_BLOCKS))
    n_qcols = Q_BLOCK // LANES

    def scores(h):
        cols = slice((h // 2) * LANES, (h // 2 + 1) * LANES)
        qp = qt_ref[0, 0, cols, :]
        qm = jnp.where(low if h % 2 == 0 else jnp.logical_not(low), qp, jnp.zeros_like(qp))
        out = []
        for j in range(N_KEY_BLOCKS):
            kblk = k_ref[0, pl.ds(pl.multiple_of(kbs[j] * Q_BLOCK, Q_BLOCK), Q_BLOCK), cols]
            out.append(_dot(kblk, qm) + bias_ref[h, bsels[j]])
        return out

    def softmax_pv(h, s):
        cols = slice((h // 2) * LANES, (h // 2 + 1) * LANES)
        p_cols, inv_cols = [], []
        for qc in range(n_qcols):
            ls = slice(qc * LANES, (qc + 1) * LANES)
            lo, hi = qc * LANES, qc * LANES + A_REACH + LANES
            spans = [(max(lo - j * Q_BLOCK, 0), min(hi - j * Q_BLOCK, Q_BLOCK)) for j in range(N_KEY_BLOCKS)]
            parts = [s[j][a:b, ls] for j, (a, b) in enumerate(spans)]
            m = functools.reduce(jnp.maximum, [jnp.max(x, axis=0, keepdims=True) for x in parts])
            ps = [jnp.exp(x - m) for x in parts]
            denom = functools.reduce(jnp.add, [jnp.sum(x, axis=0, keepdims=True) for x in ps])
            inv_cols.append(1.0 / denom)
            padded = []
            for x, (a, b) in zip(ps, spans):
                pieces = ([jnp.zeros((a, LANES), F32)] if a else []) + [x]
                pieces += [jnp.zeros((Q_BLOCK - b, LANES), F32)] if b < Q_BLOCK else []
                padded.append(pieces[0] if len(pieces) == 1 else jnp.concatenate(pieces, axis=0))
            p_cols.append(padded)
        acc = None
        for j in range(N_KEY_BLOCKS):
            pt = jnp.concatenate([p_cols[qc][j] for qc in range(n_qcols)], axis=1).astype(BF16)
            pv = _dot(vt_ref[0, kbs[j], cols, :], pt)
            acc = pv if acc is None else acc + pv
        return acc * jnp.concatenate(inv_cols, axis=1)

    ready = [scores(h) for h in range(SCORE_LOOKAHEAD)]
    pair_out = []
    for h in range(A_HEADS):
        if h + SCORE_LOOKAHEAD < A_HEADS:
            ready.append(scores(h + SCORE_LOOKAHEAD))
        pair_out.append(softmax_pv(h, ready.pop(0)))
        if h % 2 == 1:
            cols = slice((h // 2) * LANES, (h // 2 + 1) * LANES)
            o_pair = jnp.concatenate([pair_out[0][:A_HEAD_DIM], pair_out[1][A_HEAD_DIM:]], axis=0)
            o_ref[0, :, cols] = o_pair.T.astype(BF16)
            pair_out = []


def _attn_prompt(qt, k, vt, bias):
    b, t, _ = k.shape
    return pl.pallas_call(
        _attn_prompt_kernel,
        grid=(b, t // Q_BLOCK),
        in_specs=[
            pl.BlockSpec((1, 1, A_WIDTH, Q_BLOCK), lambda i, j: (i, j, 0, 0)),
            pl.BlockSpec((1, t, A_WIDTH), lambda i, j: (i, 0, 0)),
            pl.BlockSpec((1, t // Q_BLOCK, A_WIDTH, Q_BLOCK), lambda i, j: (i, 0, 0, 0)),
            _const_spec(bias.shape),
        ],
        out_specs=pl.BlockSpec((1, Q_BLOCK, A_WIDTH), lambda i, j: (i, j, 0)),
        out_shape=jax.ShapeDtypeStruct((b, t, A_WIDTH), BF16),
        compiler_params=_params(("arbitrary", "arbitrary")),
        name="attn_prompt",
    )(qt, k, vt, bias)


def _attn_sample_kernel(q_ref, ck_ref, cv_ref, nk_ref, nv_ref, bc_ref, bn_ref, o_ref):
    rows = q_ref.shape[1]
    lane = lax.broadcasted_iota(jnp.int32, (rows, LANES), 1)
    low = lane < A_HEAD_DIM
    for hp in range(A_HEADS // 2):
        cols = slice(hp * LANES, (hp + 1) * LANES)
        qp = q_ref[0, :, cols]
        ck = ck_ref[0, :, cols].astype(BF16)
        cv = cv_ref[0, :, cols].astype(BF16)
        nk = nk_ref[0, :, cols].astype(BF16)
        nv = nv_ref[0, :, cols].astype(BF16)
        pair_out = []
        for e in range(2):
            h = 2 * hp + e
            qm = jnp.where(low if e == 0 else jnp.logical_not(low), qp, jnp.zeros_like(qp))
            scores = [_dot_nt(qm, ck) + bc_ref[h], _dot_nt(qm, nk) + bn_ref[h]]
            pair_out.append(_softmax_pv(scores, [cv, nv]))
        o_ref[0, :, cols] = jnp.where(low, pair_out[0], pair_out[1]).astype(BF16)


def _attn_sample(q, cache_k, cache_v, new_k, new_v, rel_bias):
    b, t, _ = q.shape
    c = cache_k.shape[1]
    bias = _toeplitz_bias(rel_bias, t, c + LANES, c)
    bias = jnp.where((np.arange(c + LANES) < c + t)[None, None, :], bias, -jnp.inf)
    bias_c, bias_n = bias[:, :, :c], bias[:, :, c:]
    pad = ((0, 0), (0, LANES - t), (0, 0))
    nk, nv = jnp.pad(new_k, pad), jnp.pad(new_v, pad)
    blk = lambda r: pl.BlockSpec((1, r, A_WIDTH), lambda i: (i, 0, 0))
    return pl.pallas_call(
        _attn_sample_kernel,
        grid=(b,),
        in_specs=[blk(t), blk(c), blk(c), blk(LANES), blk(LANES),
                  _const_spec(bias_c.shape), _const_spec(bias_n.shape)],
        out_specs=blk(t),
        out_shape=jax.ShapeDtypeStruct((b, t, A_WIDTH), BF16),
        compiler_params=_params(("arbitrary",)),
        name="attn_sample",
    )(q, cache_k, cache_v, nk, nv, bias_c, bias_n)


def _block_diag(x2):
    half = x2.shape[1] // 2
    z = jnp.zeros_like(x2[:, :half])
    top = jnp.concatenate([x2[:, :half], z], axis=1)
    bot = jnp.concatenate([z, x2[:, half:]], axis=1)
    return jnp.concatenate([top, bot], axis=0)


def _delta_kernel(q_ref, k_ref, v_ref, z_ref, ab_ref, s0_ref, alog_ref, dtb_ref, nw_ref,
                  o_ref, s_ref, rows_ref, *, n_valid):
    L = DELTA_CHUNK
    tb = q_ref.shape[1]
    nch = tb // L
    n_pairs = B_HEADS // 2
    n_levels = L.bit_length() - 2

    @pl.when(pl.program_id(1) == 0)
    def _():
        s_ref[0] = s0_ref[0]

    chains = [(c, p) for c in range(nch) for p in range(n_pairs)]

    st = {}
    for c, p in chains:
        r = slice(c * L, (c + 1) * L)
        ls = slice(p * 2 * L, (p + 1) * 2 * L)
        q2 = q_ref[0, r, ls]
        k2 = k_ref[0, r, ls]
        sc = _dot_nt(jnp.concatenate([q2, k2], axis=0), _block_diag(k2))
        st[c, p] = dict(q2=q2.astype(F32), k2=k2.astype(F32), qk=sc[:L], kk=sc[L:])

    ab_t = ab_ref[0].T
    rows8 = ab_t[0:2 * B_HEADS, :]
    x = rows8 + dtb_ref[:, 0:1]
    softplus = jnp.maximum(x, 0.0) + jnp.log(1.0 + jnp.exp(-jnp.abs(x)))
    g8 = -jnp.exp(alog_ref[:, 0:1]) * softplus
    beta8 = _sigmoid(rows8)
    tok = lax.broadcasted_iota(jnp.int32, (2 * B_HEADS, tb), 1)
    if n_valid < tb:
        g8 = jnp.where(tok < n_valid, g8, 0.0)
        beta8 = jnp.where(tok < n_valid, beta8, 0.0)
    pos = tok & (L - 1)
    gc8 = g8
    shift = 1
    while shift < L:
        gc8 = gc8 + jnp.where(pos >= shift, pltpu.roll(gc8, shift, 1), 0.0)
        shift *= 2
    gl8 = [gc8[:, (c + 1) * L - 1:(c + 1) * L] for c in range(nch)]
    glb8 = jnp.concatenate([jnp.broadcast_to(g, (2 * B_HEADS, L)) for g in gl8], axis=1)
    rows_ref[...] = jnp.zeros_like(rows_ref)
    rows_ref[0:8, :] = gc8
    rows_ref[8:16, :] = beta8
    rows_ref[16:24, :] = jnp.exp(gc8)
    rows_ref[24:32, :] = jnp.exp(glb8 - gc8)
    cols = rows_ref[...].T

    ri = lax.broadcasted_iota(jnp.int32, (L, 2 * L), 0)
    ci = lax.broadcasted_iota(jnp.int32, (L, 2 * L), 1) & (L - 1)
    tril = ri >= ci
    strict = ri > ci
    eye = (ri == ci).astype(F32)

    def pair_cols(c, i):
        r = slice(c * L, (c + 1) * L)
        return [jnp.broadcast_to(cols[r, i + e:i + e + 1], (L, L)) for e in range(2)]

    for c, p in chains:
        r = slice(c * L, (c + 1) * L)
        ls = slice(p * 2 * L, (p + 1) * 2 * L)
        h0 = 2 * p
        d = st[c, p]
        gc_c = jnp.concatenate(pair_cols(c, h0), axis=1)
        beta_c = pair_cols(c, 8 + B_HEADS + h0)
        gc_r = jnp.concatenate([jnp.broadcast_to(gc8[h0 + e:h0 + e + 1, r], (L, L)) for e in range(2)], axis=1)
        decay = jnp.exp(jnp.where(tril, gc_c - gc_r, -jnp.inf))
        m = jnp.where(strict, -(jnp.concatenate(beta_c, axis=1) * d.pop("kk") * decay), 0.0)
        d.update(v2=v_ref[0, r, ls].astype(F32), beta=beta_c, egc=pair_cols(c, 16 + h0),
                 edec=pair_cols(c, 24 + h0), qkd=d.pop("qk") * decay, t=eye + m, m=m)

    for key in chains:
        mb = st[key]["m"].astype(BF16)
        st[key]["p"] = _dot(mb, _block_diag(mb))
    for lvl in range(n_levels):
        last = lvl == n_levels - 1
        for key in chains:
            d = st[key]
            pb = d["p"].astype(BF16)
            tb16 = d["t"].astype(BF16)
            if last:
                d["t"] = d["t"] + _dot(tb16, _block_diag(pb))
            else:
                res = _dot(jnp.concatenate([tb16, pb], axis=0), _block_diag(pb))
                d["t"] = d["t"] + res[:L]
                d["p"] = res[L:]

    for key in chains:
        d = st[key]
        t16 = d["t"].astype(BF16)
        qkd16 = d["qkd"].astype(BF16)
        phi, psi, qt, oin = [], [], [], []
        for e in range(2):
            hs = slice(e * L, (e + 1) * L)
            kh = d["k2"][:, hs]
            rhs = jnp.concatenate([d["v2"][:, hs] * d["beta"][e], kh * (d["beta"][e] * d["egc"][e])], axis=1)
            sol = _dot(t16[:, hs], rhs.astype(BF16)).astype(BF16)
            kdt = (kh * d["edec"][e]).T.astype(BF16)
            r2 = _dot(jnp.concatenate([qkd16[:, hs], kdt], axis=0), sol)
            oin.append(r2[:L, :B_DV])
            qt.append(d["q2"][:, hs] * d["egc"][e] - r2[:L, B_DV:])
            psi.append(r2[L:, :B_DV])
            phi.append(-r2[L:, B_DV:])
        d["lhs"] = jnp.concatenate([jnp.concatenate(phi, axis=1), jnp.concatenate(qt, axis=1)], axis=0).astype(BF16)
        d["psi"] = jnp.concatenate(psi, axis=1)
        d["oin"] = jnp.concatenate(oin, axis=1)

    lanes = [slice(p * 2 * L, (p + 1) * 2 * L) for p in range(n_pairs)]
    s2 = [s_ref[0, :, ls] for ls in lanes]
    o2 = {}
    for c in range(nch):
        for p in range(n_pairs):
            d = st[c, p]
            res = _dot(d["lhs"], _block_diag(s2[p].astype(BF16)))
            o2[c, p] = res[L:] + d["oin"]
            egl = jnp.concatenate([jnp.broadcast_to(jnp.exp(gl8[c][2 * p + e:2 * p + e + 1, :]), (1, B_DV))
                                   for e in range(2)], axis=1)
            s2[p] = s2[p] * egl + res[:L] + d["psi"]
    for p in range(n_pairs):
        s_ref[0, :, lanes[p]] = s2[p]
    nw = nw_ref[...]
    for c, p in chains:
        r = slice(c * L, (c + 1) * L)
        for e in range(2):
            hs = slice((2 * p + e) * B_DV, (2 * p + e + 1) * B_DV)
            zh = z_ref[0, r, hs].astype(F32)
            o_ref[0, r, hs] = (_rms(o2[c, p][:, e * B_DV:(e + 1) * B_DV], nw) * (zh * _sigmoid(zh))).astype(BF16)


def _delta(q, k, v, z, ab, s0, a_log, dt_bias, norm_w, *, block, n_valid):
    b, t, _ = q.shape
    assert t % block == 0 and block % DELTA_CHUNK == 0
    alog8 = jnp.broadcast_to(jnp.tile(a_log.astype(F32), 2)[:, None], (2 * B_HEADS, LANES))
    dtb8 = jnp.broadcast_to(jnp.tile(dt_bias.astype(F32), 2)[:, None], (2 * B_HEADS, LANES))
    row = lambda w: pl.BlockSpec((1, block, w), lambda i, j: (i, j, 0))
    state = pl.BlockSpec((1, B_DK, B_HEADS * B_DV), lambda i, j: (i, 0, 0))
    s0 = s0.astype(F32).transpose(0, 2, 1, 3).reshape(b, B_DK, B_HEADS * B_DV)
    o, s = pl.pallas_call(
        functools.partial(_delta_kernel, n_valid=n_valid),
        grid=(b, t // block),
        in_specs=[row(B_KW), row(B_KW), row(B_VW), row(B_VW), row(LANES), state,
                  _const_spec(alog8.shape), _const_spec(dtb8.shape), _const_spec((1, B_DV))],
        out_specs=[row(B_VW), state],
        out_shape=[jax.ShapeDtypeStruct((b, t, B_VW), BF16),
                   jax.ShapeDtypeStruct((b, B_DK, B_HEADS * B_DV), F32)],
        scratch_shapes=[pltpu.VMEM((LANES, block), F32)],
        compiler_params=_params(("arbitrary", "arbitrary")),
        name="delta_%d" % block,
    )(q, k, v, z, ab, s0, alog8, dtb8, norm_w.reshape(1, B_DV).astype(F32))
    return o, s.reshape(b, B_DK, B_HEADS, B_DV).transpose(0, 2, 1, 3)


def _merge_kernel(x_ref, oa_ref, ob_ref, g_ref, wpa_ref, wpb_ref, wout_ref, gain_ref, o_ref):
    tm = x_ref.shape[0]
    sub = min(ROW_TILE, tm)
    blocks = [slice(r, r + sub) for r in range(0, tm, sub)]

    def branches(rs):
        return _dot(oa_ref[rs, :], wpa_ref[...]), _dot(ob_ref[rs, :], wpb_ref[...])

    nxt = branches(blocks[0])
    for i, rs in enumerate(blocks):
        ya, yb = nxt
        if i + 1 < len(blocks):
            nxt = branches(blocks[i + 1])
        merged = g_ref[rs, :D_MODEL].astype(F32) * ya + g_ref[rs, D_MODEL:].astype(F32) * yb
        y = _dot(merged.astype(BF16), wout_ref[...])
        o_ref[rs, :] = x_ref[rs, :] + _rms(y, gain_ref[...])


def _merge(x, oa, ob, gates, wpa, wpb, wout, gain):
    n, d = x.shape
    tm = min(MERGE_ROWS, n)
    row = lambda w: pl.BlockSpec((tm, w), lambda i: (i, 0))
    return pl.pallas_call(
        _merge_kernel,
        grid=(n // tm,),
        in_specs=[row(d), row(A_WIDTH), row(B_VW), row(2 * d),
                  _const_spec(wpa.shape), _const_spec(wpb.shape), _const_spec(wout.shape), _const_spec(gain.shape)],
        out_specs=row(d),
        out_shape=jax.ShapeDtypeStruct((n, d), F32),
        compiler_params=_params(("arbitrary",)),
        name="merge_%d" % n,
    )(x, oa, ob, gates, wpa, wpb, wout, gain)


FF_SPLIT = 4


def _mlp_kernel(x_ref, gpre_ref, wup_ref, wdown_ref, gpost_ref, o_ref):
    x = x_ref[...]
    hb = _rms(x, gpre_ref[...]).astype(BF16)
    wcols = D_FF // FF_SPLIT
    f = None
    for c in range(FF_SPLIT):
        cs = slice(c * wcols, (c + 1) * wcols)
        up = jnp.maximum(_dot(hb, wup_ref[:, cs]), 0.0)
        part = _dot((up * up).astype(BF16), wdown_ref[cs, :])
        f = part if f is None else f + part
    o_ref[...] = x + _rms(f, gpost_ref[...])


def _mlp(x, gpre, wup, wdown, gpost):
    n, d = x.shape
    tm = min(ROW_TILE, n)
    row = pl.BlockSpec((tm, d), lambda i: (i, 0))
    return pl.pallas_call(
        _mlp_kernel,
        grid=(n // tm,),
        in_specs=[row, _const_spec(gpre.shape), _const_spec(wup.shape), _const_spec(wdown.shape),
                  _const_spec(gpost.shape)],
        out_specs=row,
        out_shape=jax.ShapeDtypeStruct((n, d), F32),
        compiler_params=_params(("arbitrary",)),
        name="mlp_%d" % n,
    )(x, gpre, wup, wdown, gpost)


def _layer_weights(norm_mix_pre, w_in, conv_w, w_proj_a, w_proj_b, w_out, norm_mix_post,
                   norm_mlp_pre, w_up, w_down, norm_mlp_post):
    o_cv = 3 * A_WIDTH
    o_z = o_cv + B_CONV_CH
    o_ab = o_z + B_VW
    o_g = o_ab + 2 * B_HEADS
    wb = w_in.astype(BF16)
    wzab = jnp.concatenate([wb[:, o_z:o_g], jnp.zeros((D_MODEL, LANES - 2 * B_HEADS), BF16)], axis=1)
    vec = lambda g: g.reshape(1, -1).astype(F32)
    return dict(
        gain_in=vec(norm_mix_pre),
        wq=wb[:, :A_WIDTH], wk=wb[:, A_WIDTH:2 * A_WIDTH], wv=wb[:, 2 * A_WIDTH:o_cv],
        wcv=wb[:, o_cv:o_z], wzab=wzab, wg=wb[:, o_g:],
        conv_w=conv_w.astype(F32),
        wpa=w_proj_a.astype(BF16), wpb=w_proj_b.astype(BF16), wout=w_out.astype(BF16),
        gain_mix=vec(norm_mix_post), gain_pre=vec(norm_mlp_pre), gain_post=vec(norm_mlp_post),
        wup=w_up.astype(BF16), wdown=w_down.astype(BF16),
    )


def _tail(x, w, oa, ob, gates):
    b, t, d = x.shape
    x1 = _merge(x.reshape(b * t, d), oa.reshape(b * t, A_WIDTH), ob.reshape(b * t, B_VW),
                gates.reshape(b * t, 2 * d), w["wpa"], w["wpb"], w["wout"], w["gain_mix"])
    return _mlp(x1, w["gain_pre"], w["wup"], w["wdown"], w["gain_post"]).reshape(b, t, d)


def _prompt_layer(x, w, rel_bias, a_log, dt_bias, delta_norm_w):
    b, t, _ = x.shape
    qt, k, vt, dq, dk, dv, z, ab, gates, kf, vf, cvt = _inproj(
        x, w["gain_in"], w["wq"], w["wk"], w["wv"], w["wcv"], w["wzab"], w["wg"], w["conv_w"], prompt=True)
    oa = _attn_prompt(qt, k, vt, _prompt_bias_table(rel_bias))
    s0 = jnp.zeros((b, B_HEADS, B_DK, B_DV), F32)
    ob, s_new = _delta(dq, dk, dv, z, ab, s0, a_log, dt_bias, delta_norm_w,
                       block=DELTA_BLOCK, n_valid=DELTA_BLOCK)
    y = _tail(x, w, oa, ob, gates)
    keep = min(A_REACH, t)
    new_k = kf.reshape(b, keep, A_HEADS, A_HEAD_DIM)
    new_v = vf.reshape(b, keep, A_HEADS, A_HEAD_DIM)
    new_conv = cvt[:, SUBLANES - (CONV_WIDTH - 1):, :]
    return y, new_k, new_v, new_conv, s_new


def _sample_layer(x, cache_k, cache_v, conv_state, delta_state, w, rel_bias, a_log, dt_bias, delta_norm_w):
    b, t, d = x.shape
    c = cache_k.shape[1]
    q, k, v, cv, z, ab, gates = _inproj(
        x.reshape(1, b * t, d), w["gain_in"], w["wq"], w["wk"], w["wv"], w["wcv"], w["wzab"], w["wg"],
        prompt=False)
    per_stream = lambda a: a.reshape(b, t, a.shape[-1])
    q, k, v, cv, z, ab, gates = map(per_stream, (q, k, v, cv, z, ab, gates))
    oa = _attn_sample(q, cache_k.reshape(b, c, A_WIDTH), cache_v.reshape(b, c, A_WIDTH), k, v, rel_bias)
    prev = jnp.pad(conv_state.astype(F32), ((0, 0), (SUBLANES - (CONV_WIDTH - 1), 0), (0, 0)))
    dq, dk, dv = _conv_sample(cv, prev, w["conv_w"])
    pad = lambda a: jnp.pad(a, ((0, 0), (0, DELTA_CHUNK - t), (0, 0)))
    ob, s_new = _delta(pad(dq), pad(dk), pad(dv), pad(z), pad(ab), delta_state, a_log, dt_bias, delta_norm_w,
                       block=DELTA_CHUNK, n_valid=t)
    y = _tail(x, w, oa, ob[:, :t], gates)
    new_conv = jnp.concatenate([conv_state.astype(F32), cv], axis=1)[:, -(CONV_WIDTH - 1):]
    return (y, k.reshape(b, t, A_HEADS, A_HEAD_DIM), v.reshape(b, t, A_HEADS, A_HEAD_DIM), new_conv, s_new)


def kernel(x_prompt, x_sample, cache_attn_k, cache_attn_v, state_conv, state_delta, norm_mix_pre, w_in,
           rel_bias, conv_w, a_log, dt_bias, delta_norm_w, w_proj_a, w_proj_b, w_out, norm_mix_post,
           norm_mlp_pre, w_up, w_down, norm_mlp_post):
    depth = w_in.shape[0]
    yp, ys = x_prompt, x_sample
    outs = [[] for _ in range(8)]
    for l in range(depth):
        w = _layer_weights(norm_mix_pre[l], w_in[l], conv_w[l], w_proj_a[l], w_proj_b[l], w_out[l],
                           norm_mix_post[l], norm_mlp_pre[l], w_up[l], w_down[l], norm_mlp_post[l])
        yp, k1, v1, c1, s1 = _prompt_layer(yp, w, rel_bias[l], a_log[l], dt_bias[l], delta_norm_w[l])
        ys, k2, v2, c2, s2 = _sample_layer(ys, cache_attn_k[l], cache_attn_v[l], state_conv[l], state_delta[l],
                                           w, rel_bias[l], a_log[l], dt_bias[l], delta_norm_w[l])
        for lst, val in zip(outs, (k1, v1, c1, s1, k2, v2, c2, s2)):
            lst.append(val)
    return (yp, ys) + tuple(jnp.stack(o) for o in outs)
```

```python
import functools

import numpy as np
import jax
import jax.numpy as jnp
from jax import lax
from jax.experimental import pallas as pl
from jax.experimental.pallas import tpu as pltpu

F32 = jnp.float32
BF16 = jnp.bfloat16

D_MODEL = 1024
CHUNK = 64
RMS_EPS = 1e-6
L2_EPS = 1e-6
A_HEADS = 8
A_HEAD_DIM = 64
A_WIDTH = A_HEADS * A_HEAD_DIM
A_PREV_CHUNKS = 8
A_REACH = A_PREV_CHUNKS * CHUNK
MAX_REL = 128
B_HEADS = 4
B_DK = 128
B_DV = 128
B_KW = B_HEADS * B_DK
B_VW = B_HEADS * B_DV
B_CONV_CH = 2 * B_KW + B_VW
CONV_WIDTH = 4
D_FF = 4 * D_MODEL

LANES = 128
SUBLANES = 8
VMEM_LIMIT_BYTES = 56 * 1024 * 1024

Q_BLOCK = 256
N_KEY_BLOCKS = A_REACH // Q_BLOCK + 1
SCORE_LOOKAHEAD = 2
DELTA_CHUNK = 128
DELTA_BLOCK = 512
DELTA_STREAMS = 2
SAMPLE_DELTA_STREAMS = 8
CHAIN_STARTS_PER_ROUND = 2
ROW_TILE = 512
CONV_ROWS = 64
GATE_COLS = 512
MERGE_ROWS = 1024

assert B_DK == LANES and B_DV == LANES and B_HEADS % 2 == 0


def _dot(a, b):
    return jnp.dot(a, b, preferred_element_type=F32)


def _dot_nt(a, b):
    return lax.dot_general(a, b, (((1,), (1,)), ((), ())), preferred_element_type=F32)


def _sigmoid(x):
    return 0.5 * jnp.tanh(0.5 * x) + 0.5


def _rms(x, gain):
    ms = jnp.mean(x * x, axis=-1, keepdims=True)
    return x * lax.rsqrt(ms + RMS_EPS) * gain


def _params(sem):
    return pltpu.CompilerParams(dimension_semantics=sem, vmem_limit_bytes=VMEM_LIMIT_BYTES)


def _const_spec(shape):
    nd = len(shape)
    return pl.BlockSpec(shape, lambda *_: (0,) * nd, pipeline_mode=pl.Buffered(1))


def _conv_silu(hist, x, w):
    n = x.shape[0]
    full = jnp.concatenate([hist, x], axis=0)
    rb = min(CONV_ROWS, n)
    out = []
    for r0 in range(0, n, rb):
        piece = full[r0:r0 + rb + SUBLANES, :]
        z = piece * w[0:1, :]
        for i in range(1, CONV_WIDTH):
            z = pltpu.roll(z, 1, 0) + piece * w[i:i + 1, :]
        acc = z[SUBLANES:, :]
        out.append(acc * _sigmoid(acc))
    return out[0] if len(out) == 1 else jnp.concatenate(out, axis=0)


def _l2norm_heads(y, scale):
    heads = []
    for h in range(y.shape[1] // B_DK):
        yh = y[:, h * B_DK:(h + 1) * B_DK]
        heads.append(yh * (lax.rsqrt(jnp.sum(yh * yh, axis=-1, keepdims=True) + L2_EPS) * scale))
    return jnp.concatenate(heads, axis=1)


def _conv_silu_l2norm(hist, x, convw_ref):
    y = _conv_silu(hist, x, convw_ref[...])
    return (_l2norm_heads(y[:, :B_KW], B_DK ** -0.5), _l2norm_heads(y[:, B_KW:2 * B_KW], 1.0), y[:, 2 * B_KW:])


def _inproj_kernel(*refs, prompt):
    x_ref, gain_ref, wq_ref, wk_ref, wv_ref, wcv_ref, wzab_ref, wg_ref = refs[:8]
    if prompt:
        (convw_ref, q_ref, k_ref, vt_ref, dq_ref, dk_ref, dv_ref, z_ref, ab_ref, g_ref,
         kf_ref, vf_ref, cvt_ref, hist_ref) = refs[8:]
    else:
        q_ref, kf_ref, vf_ref, cv_ref, z_ref, ab_ref, g_ref = refs[8:]
    if prompt:
        @pl.when(pl.program_id(1) == 0)
        def _():
            hist_ref[...] = jnp.zeros((SUBLANES, B_CONV_CH), F32)

    x = x_ref[0]
    hb = _rms(x, gain_ref[...]).astype(BF16)
    tm = x.shape[0]

    stages = []

    def stage(issue):
        return lambda consume: stages.append((issue, consume))

    if prompt:
        norms = (B_DK ** -0.5, 1.0, None)
        for part, out_ref in enumerate((dq_ref, dk_ref, dv_ref)):
            cs = slice(part * B_KW, (part + 1) * B_KW)

            @stage(lambda cs=cs: _dot(hb, wcv_ref[:, cs]))
            def _(cv, cs=cs, out_ref=out_ref, scale=norms[part]):
                y = _conv_silu(hist_ref[:, cs], cv, convw_ref[:, cs])
                out_ref[0] = (y if scale is None else _l2norm_heads(y, scale)).astype(BF16)
                hist_ref[:, cs] = cv[tm - SUBLANES:, :]
                cvt_ref[0, :, cs] = cv[tm - SUBLANES:, :]
    else:
        @stage(lambda: _dot(hb, wcv_ref[...]))
        def _(cv):
            cv_ref[0] = cv

    def feature_major(val, out_ref):
        vt = val.T.astype(BF16)
        for j in range(tm // Q_BLOCK):
            out_ref[0, j] = vt[:, j * Q_BLOCK:(j + 1) * Q_BLOCK]

    @stage(lambda: _dot(hb, wq_ref[...]))
    def _(q):
        q = q * (A_HEAD_DIM ** -0.5)
        if prompt:
            feature_major(q, q_ref)
        else:
            q_ref[0] = q.astype(BF16)

    @stage(lambda: _dot(hb, wk_ref[...]))
    def _(k):
        kf_ref[0] = k
        if prompt:
            k_ref[0] = k.astype(BF16)

    @stage(lambda: _dot(hb, wv_ref[...]))
    def _(v):
        vf_ref[0] = v
        if prompt:
            feature_major(v, vt_ref)

    @stage(lambda: _dot(hb, wzab_ref[...]))
    def _(zab):
        z_ref[0] = zab[:, :B_VW].astype(BF16)
        ab_ref[0] = zab[:, B_VW:]

    for c0 in range(0, 2 * D_MODEL, GATE_COLS):
        cs = slice(c0, c0 + GATE_COLS)

        @stage(lambda cs=cs: _dot(hb, wg_ref[:, cs]))
        def _(g, cs=cs):
            g_ref[0, :, cs] = _sigmoid(g).astype(BF16)

    pending = None
    for issue, consume in stages:
        val = issue()
        if pending is not None:
            pending[1](pending[0])
        pending = (val, consume)
    pending[1](pending[0])


def _inproj(x, gain, wq, wk, wv, wcv, wzab, wg, conv_w=None, *, prompt):
    b, t, d = x.shape
    tm = min(ROW_TILE, t)
    assert t % tm == 0
    row = lambda w, dt: (jax.ShapeDtypeStruct((b, t, w), dt), pl.BlockSpec((1, tm, w), lambda i, j: (i, j, 0)))
    stream = lambda r, w: (jax.ShapeDtypeStruct((b, r, w), F32), pl.BlockSpec((1, r, w), lambda i, j: (i, 0, 0)))
    if prompt:
        assert tm == A_REACH and tm % Q_BLOCK == 0
        nq = tm // Q_BLOCK
        feat = (jax.ShapeDtypeStruct((b, t // Q_BLOCK, A_WIDTH, Q_BLOCK), BF16),
                pl.BlockSpec((1, nq, A_WIDTH, Q_BLOCK), lambda i, j: (i, j, 0, 0)))
        outs = [feat, row(A_WIDTH, BF16), feat, row(B_KW, BF16), row(B_KW, BF16), row(B_VW, BF16)]
    else:
        outs = [row(A_WIDTH, BF16), row(A_WIDTH, F32), row(A_WIDTH, F32), row(B_CONV_CH, F32)]
    outs += [row(B_VW, BF16), row(LANES, F32), row(2 * D_MODEL, BF16)]
    weights = [wq, wk, wv, wcv, wzab, wg]
    scratch = []
    if prompt:
        outs += [stream(tm, A_WIDTH), stream(tm, A_WIDTH), stream(SUBLANES, B_CONV_CH)]
        weights.append(conv_w)
        scratch.append(pltpu.VMEM((SUBLANES, B_CONV_CH), F32))
    in_specs = [pl.BlockSpec((1, tm, d), lambda i, j: (i, j, 0)), _const_spec(gain.shape)]
    in_specs += [_const_spec(w.shape) for w in weights]
    return pl.pallas_call(
        functools.partial(_inproj_kernel, prompt=prompt),
        grid=(b, t // tm),
        in_specs=in_specs,
        out_specs=[o[1] for o in outs],
        out_shape=[o[0] for o in outs],
        scratch_shapes=scratch,
        compiler_params=_params(("arbitrary", "arbitrary")),
        name="inproj_prompt" if prompt else "inproj_sample",
    )(x, gain, *weights)


def _conv_sample_kernel(cv_ref, prev_ref, convw_ref, dq_ref, dk_ref, dv_ref):
    dq, dk, dv = _conv_silu_l2norm(prev_ref[0], cv_ref[0], convw_ref)
    dq_ref[0] = dq.astype(BF16)
    dk_ref[0] = dk.astype(BF16)
    dv_ref[0] = dv.astype(BF16)


def _conv_sample(cv, prev, conv_w):
    b, t, _ = cv.shape
    blk = lambda r, w: pl.BlockSpec((1, r, w), lambda i: (i, 0, 0))
    out = jax.ShapeDtypeStruct((b, t, B_KW), BF16)
    return pl.pallas_call(
        _conv_sample_kernel,
        grid=(b,),
        in_specs=[blk(t, B_CONV_CH), blk(SUBLANES, B_CONV_CH), _const_spec(conv_w.shape)],
        out_specs=[blk(t, B_KW)] * 3,
        out_shape=[out] * 3,
        compiler_params=_params(("arbitrary",)),
        name="conv_sample",
    )(cv, prev, conv_w)


def _toeplitz_bias(rel_bias, rows, width, offset):
    pitch = width + rows
    period = pitch + 1
    j = np.arange(period)
    rel = np.where(j < width, offset - j, offset + period - j)
    idx = np.clip(rel, -MAX_REL, MAX_REL) + MAX_REL
    vec = rel_bias.astype(F32)[:, idx]
    tiled = jnp.tile(vec, (1, rows))[:, :rows * pitch]
    return tiled.reshape(rel_bias.shape[0], rows, pitch)[:, :, :width]


def _prompt_bias_table(rel_bias):
    width = N_KEY_BLOCKS * Q_BLOCK
    bias = _toeplitz_bias(rel_bias, Q_BLOCK, width, A_REACH)
    tc = np.arange(Q_BLOCK)[:, None] // CHUNK
    cc = np.arange(width)[None, :] // CHUNK
    band = (cc >= tc) & (cc <= tc + A_PREV_CHUNKS)
    bias = jnp.where(band[None], bias, -jnp.inf)
    bias = bias.transpose(0, 2, 1).reshape(A_HEADS, N_KEY_BLOCKS, Q_BLOCK, Q_BLOCK)
    masked = jnp.full((A_HEADS, 1, Q_BLOCK, Q_BLOCK), -jnp.inf, F32)
    return jnp.concatenate([bias, masked], axis=1)


def _softmax_pv(scores, values):
    m = None
    for s in scores:
        mj = jnp.max(s, axis=-1, keepdims=True)
        m = mj if m is None else jnp.maximum(m, mj)
    denom = None
    acc = None
    for s, vals in zip(scores, values):
        p = jnp.exp(s - m)
        dj = jnp.sum(p, axis=-1, keepdims=True)
        denom = dj if denom is None else denom + dj
        pv = _dot(p.astype(BF16), vals)
        acc = pv if acc is None else acc + pv
    return acc / denom


def _attn_prompt_kernel(qt_ref, k_ref, vt_ref, bias_ref, o_ref):
    qi = pl.program_id(1)
    row = lax.broadcasted_iota(jnp.int32, (LANES, Q_BLOCK), 0)
    low = row < A_HEAD_DIM
    kbs, bsels = [], []
    for j in range(N_KEY_BLOCKS):
        kraw = qi - (N_KEY_BLOCKS - 1) + j
        kbs.append(jnp.maximum(kraw, 0))
        bsels.append(jnp.where(kraw >= 0, j, N_KEY_BLOCKS))
    n_qcols = Q_BLOCK // LANES

    def scores(h):
        cols = slice((h // 2) * LANES, (h // 2 + 1) * LANES)
        qp = qt_ref[0, 0, cols, :]
        qm = jnp.where(low if h % 2 == 0 else jnp.logical_not(low), qp, jnp.zeros_like(qp))
        out = []
        for j in range(N_KEY_BLOCKS):
            kblk = k_ref[0, pl.ds(pl.multiple_of(kbs[j] * Q_BLOCK, Q_BLOCK), Q_BLOCK), cols]
            out.append(_dot(kblk, qm) + bias_ref[h, bsels[j]])
        return out

    def softmax_pv(h, s):
        cols = slice((h // 2) * LANES, (h // 2 + 1) * LANES)
        p_cols, inv_cols = [], []
        for qc in range(n_qcols):
            ls = slice(qc * LANES, (qc + 1) * LANES)
            lo, hi = qc * LANES, qc * LANES + A_REACH + LANES
            spans = [(max(lo - j * Q_BLOCK, 0), min(hi - j * Q_BLOCK, Q_BLOCK)) for j in range(N_KEY_BLOCKS)]
            parts = [s[j][a:b, ls] for j, (a, b) in enumerate(spans)]
            m = functools.reduce(jnp.maximum, [jnp.max(x, axis=0, keepdims=True) for x in parts])
            ps = [jnp.exp(x - m) for x in parts]
            denom = functools.reduce(jnp.add, [jnp.sum(x, axis=0, keepdims=True) for x in ps])
            inv_cols.append(1.0 / denom)
            padded = []
            for x, (a, b) in zip(ps, spans):
                pieces = ([jnp.zeros((a, LANES), F32)] if a else []) + [x]
                pieces += [jnp.zeros((Q_BLOCK - b, LANES), F32)] if b < Q_BLOCK else []
                padded.append(pieces[0] if len(pieces) == 1 else jnp.concatenate(pieces, axis=0))
            p_cols.append(padded)
        acc = None
        for j in range(N_KEY_BLOCKS):
            pt = jnp.concatenate([p_cols[qc][j] for qc in range(n_qcols)], axis=1).astype(BF16)
            pv = _dot(vt_ref[0, kbs[j], cols, :], pt)
            acc = pv if acc is None else acc + pv
        return acc * jnp.concatenate(inv_cols, axis=1)

    ready = [scores(h) for h in range(SCORE_LOOKAHEAD)]
    pair_out = []
    for h in range(A_HEADS):
        if h + SCORE_LOOKAHEAD < A_HEADS:
            ready.append(scores(h + SCORE_LOOKAHEAD))
        pair_out.append(softmax_pv(h, ready.pop(0)))
        if h % 2 == 1:
            cols = slice((h // 2) * LANES, (h // 2 + 1) * LANES)
            o_pair = jnp.concatenate([pair_out[0][:A_HEAD_DIM], pair_out[1][A_HEAD_DIM:]], axis=0)
            o_ref[0, :, cols] = o_pair.T.astype(BF16)
            pair_out = []


def _attn_prompt(qt, k, vt, bias):
    b, t, _ = k.shape
    return pl.pallas_call(
        _attn_prompt_kernel,
        grid=(b, t // Q_BLOCK),
        in_specs=[
            pl.BlockSpec((1, 1, A_WIDTH, Q_BLOCK), lambda i, j: (i, j, 0, 0)),
            pl.BlockSpec((1, t, A_WIDTH), lambda i, j: (i, 0, 0)),
            pl.BlockSpec((1, t // Q_BLOCK, A_WIDTH, Q_BLOCK), lambda i, j: (i, 0, 0, 0)),
            _const_spec(bias.shape),
        ],
        out_specs=pl.BlockSpec((1, Q_BLOCK, A_WIDTH), lambda i, j: (i, j, 0)),
        out_shape=jax.ShapeDtypeStruct((b, t, A_WIDTH), BF16),
        compiler_params=_params(("arbitrary", "arbitrary")),
        name="attn_prompt",
    )(qt, k, vt, bias)


def _attn_sample_kernel(q_ref, ck_ref, cv_ref, nk_ref, nv_ref, bc_ref, bn_ref, o_ref):
    rows = q_ref.shape[1]
    lane = lax.broadcasted_iota(jnp.int32, (rows, LANES), 1)
    low = lane < A_HEAD_DIM
    pair_cols = [slice(hp * LANES, (hp + 1) * LANES) for hp in range(A_HEADS // 2)]
    scores = []
    for h in range(A_HEADS):
        cols = pair_cols[h // 2]
        qp = q_ref[0, :, cols]
        qm = jnp.where(low if h % 2 == 0 else jnp.logical_not(low), qp, jnp.zeros_like(qp))
        scores.append([_dot_nt(qm, ck_ref[0, :, cols].astype(BF16)) + bc_ref[h],
                       _dot_nt(qm, nk_ref[0, :, cols].astype(BF16)) + bn_ref[h]])
    outs = [_softmax_pv(scores[h], [cv_ref[0, :, pair_cols[h // 2]].astype(BF16),
                                    nv_ref[0, :, pair_cols[h // 2]].astype(BF16)]) for h in range(A_HEADS)]
    for hp, cols in enumerate(pair_cols):
        o_ref[0, :, cols] = jnp.where(low, outs[2 * hp], outs[2 * hp + 1]).astype(BF16)


def _attn_sample(q, cache_k, cache_v, new_k, new_v, rel_bias):
    b, t, _ = q.shape
    c = cache_k.shape[1]
    bias = _toeplitz_bias(rel_bias, t, c + LANES, c)
    bias = jnp.where((np.arange(c + LANES) < c + t)[None, None, :], bias, -jnp.inf)
    bias_c, bias_n = bias[:, :, :c], bias[:, :, c:]
    pad = ((0, 0), (0, LANES - t), (0, 0))
    nk, nv = jnp.pad(new_k, pad), jnp.pad(new_v, pad)
    blk = lambda r: pl.BlockSpec((1, r, A_WIDTH), lambda i: (i, 0, 0))
    return pl.pallas_call(
        _attn_sample_kernel,
        grid=(b,),
        in_specs=[blk(t), blk(c), blk(c), blk(LANES), blk(LANES),
                  _const_spec(bias_c.shape), _const_spec(bias_n.shape)],
        out_specs=blk(t),
        out_shape=jax.ShapeDtypeStruct((b, t, A_WIDTH), BF16),
        compiler_params=_params(("arbitrary",)),
        name="attn_sample",
    )(q, cache_k, cache_v, nk, nv, bias_c, bias_n)


def _block_diag(x2):
    half = x2.shape[1] // 2
    z = jnp.zeros_like(x2[:, :half])
    top = jnp.concatenate([x2[:, :half], z], axis=1)
    bot = jnp.concatenate([z, x2[:, half:]], axis=1)
    return jnp.concatenate([top, bot], axis=0)


def _quad_diag(x4):
    n = x4.shape[0]
    group = lax.broadcasted_iota(jnp.int32, x4.shape, 1) // n
    zero = jnp.zeros_like(x4)
    return jnp.concatenate([jnp.where(group == g, x4, zero) for g in range(4)], axis=0)


def _unit_lower_inverse(m, eye_half):
    L = m.shape[0]
    h = L // 2
    lane = lax.broadcasted_iota(jnp.int32, (h, 2 * L), 1)
    first_half = (lane & (L - 1)) < h
    left_head = lane < L
    diag = jnp.where(first_half, m[:h], m[h:])
    b16 = jnp.where(first_half, m[h:], 0.0).astype(BF16)
    t = eye_half + diag
    d16 = diag.astype(BF16)
    p = _dot(d16, _quad_diag(d16))
    yield
    n_levels = h.bit_length() - 2
    for lvl in range(n_levels):
        p16 = p.astype(BF16)
        t16 = t.astype(BF16)
        if lvl == n_levels - 1:
            t = t + _dot(t16, _quad_diag(p16))
        else:
            res = _dot(jnp.concatenate([t16, p16], axis=0), _quad_diag(p16))
            t = t + res[:h]
            p = res[h:]
        yield
    t16 = t.astype(BF16)
    zero16 = jnp.zeros_like(t16)
    ta_only = jnp.where(first_half, t16, zero16)
    w_a = jnp.concatenate([jnp.where(left_head, ta_only, zero16), zero16,
                           jnp.where(left_head, zero16, ta_only), zero16], axis=0)
    y16 = _dot(b16, w_a).astype(BF16)
    yield
    w_y = jnp.concatenate([zero16, jnp.where(left_head, y16, zero16),
                           zero16, jnp.where(left_head, zero16, y16)], axis=0)
    t_b = _dot(t16, w_y)
    return jnp.where(first_half, t, 0.0), jnp.where(first_half, t_b, t)


def _round_robin(generators):
    results = [None] * len(generators)
    active = list(enumerate(generators))
    while active:
        still = []
        for i, g in active:
            try:
                next(g)
                still.append((i, g))
            except StopIteration as stop:
                results[i] = stop.value
        active = still
    return results


def _delta_kernel(q_ref, k_ref, v_ref, z_ref, ab_ref, s0_ref, alog_ref, dtb_ref, nw_ref,
                  o_ref, s_ref, rows_ref, *, n_valid):
    L = DELTA_CHUNK
    ns, tb = q_ref.shape[0], q_ref.shape[1]
    nch = tb // L
    n_pairs = B_HEADS // 2

    @pl.when(pl.program_id(1) == 0)
    def _():
        s_ref[...] = s0_ref[...]

    chains = [(s, c, p) for s in range(ns) for c in range(nch) for p in range(n_pairs)]

    st = {}
    for s, c, p in chains:
        r = slice(c * L, (c + 1) * L)
        ls = slice(p * 2 * L, (p + 1) * 2 * L)
        q2 = q_ref[s, r, ls]
        k2 = k_ref[s, r, ls]
        sc = _dot_nt(jnp.concatenate([q2, k2], axis=0), _block_diag(k2))
        st[s, c, p] = dict(q2=q2.astype(F32), k2=k2.astype(F32), qk=sc[:L], kk=sc[L:])

    scal = _round_robin([_delta_scalars(ab_ref, alog_ref, dtb_ref, rows_ref, s, n_valid) for s in range(ns)])
    _delta_chunks(st, chains, scal, v_ref, z_ref, nw_ref, o_ref, s_ref)


def _delta_scalars(ab_ref, alog_ref, dtb_ref, rows_ref, s, n_valid):
    L = DELTA_CHUNK
    tb = ab_ref.shape[1]
    nch = tb // L
    ab_t = ab_ref[s].T
    rows8 = ab_t[0:2 * B_HEADS, :]
    x = rows8 + dtb_ref[:, 0:1]
    softplus = jnp.maximum(x, 0.0) + jnp.log(1.0 + jnp.exp(-jnp.abs(x)))
    g8 = -jnp.exp(alog_ref[:, 0:1]) * softplus
    beta8 = _sigmoid(rows8)
    tok = lax.broadcasted_iota(jnp.int32, (2 * B_HEADS, tb), 1)
    if n_valid < tb:
        g8 = jnp.where(tok < n_valid, g8, 0.0)
        beta8 = jnp.where(tok < n_valid, beta8, 0.0)
    pos = tok & (L - 1)
    gc8 = g8
    shift = 1
    while shift < L:
        yield
        gc8 = gc8 + jnp.where(pos >= shift, pltpu.roll(gc8, shift, 1), 0.0)
        shift *= 2
    gl8 = [gc8[:, (c + 1) * L - 1:(c + 1) * L] for c in range(nch)]
    glb8 = jnp.concatenate([jnp.broadcast_to(g, (2 * B_HEADS, L)) for g in gl8], axis=1)
    rows_ref[s] = jnp.zeros(rows_ref.shape[1:], F32)
    rows_ref[s, 0:8, :] = gc8
    rows_ref[s, 8:16, :] = beta8
    rows_ref[s, 16:24, :] = jnp.exp(gc8)
    rows_ref[s, 24:32, :] = jnp.exp(glb8 - gc8)
    cols = rows_ref[s].T
    return gc8, gl8, cols


def _delta_chunks(st, chains, scal, v_ref, z_ref, nw_ref, o_ref, s_ref):
    L = DELTA_CHUNK
    ns, tb = v_ref.shape[0], v_ref.shape[1]
    nch = tb // L
    n_pairs = B_HEADS // 2
    ri = lax.broadcasted_iota(jnp.int32, (L, 2 * L), 0)
    ci = lax.broadcasted_iota(jnp.int32, (L, 2 * L), 1) & (L - 1)
    tril = ri >= ci
    strict = ri > ci
    rh = lax.broadcasted_iota(jnp.int32, (L // 2, 2 * L), 0)
    ch = lax.broadcasted_iota(jnp.int32, (L // 2, 2 * L), 1) & (L // 2 - 1)
    eye_half = (rh == ch).astype(F32)

    def chain_ops(s, c, p):
        gc8, _, cols = scal[s]
        r = slice(c * L, (c + 1) * L)
        ls = slice(p * 2 * L, (p + 1) * 2 * L)
        h0 = 2 * p
        d = st[s, c, p]
        pair_cols = lambda i: [jnp.broadcast_to(cols[r, i + e:i + e + 1], (L, L)) for e in range(2)]
        gc_c = jnp.concatenate(pair_cols(h0), axis=1)
        beta = pair_cols(8 + B_HEADS + h0)
        egc = pair_cols(16 + h0)
        edec = pair_cols(24 + h0)
        gc_r = jnp.concatenate([jnp.broadcast_to(gc8[h0 + e:h0 + e + 1, r], (L, L)) for e in range(2)], axis=1)
        decay = jnp.exp(jnp.where(tril, gc_c - gc_r, -jnp.inf))
        m = jnp.where(strict, -(jnp.concatenate(beta, axis=1) * d.pop("kk") * decay), 0.0)
        qkd16 = (d.pop("qk") * decay).astype(BF16)
        v2 = v_ref[s, r, ls].astype(F32)
        yield
        top, bot = yield from _unit_lower_inverse(m, eye_half)
        yield
        t16 = jnp.concatenate([top, bot], axis=0).astype(BF16)
        sols = []
        for e in range(2):
            hs = slice(e * L, (e + 1) * L)
            kh = d["k2"][:, hs]
            rhs = jnp.concatenate([v2[:, hs] * beta[e], kh * (beta[e] * egc[e])], axis=1)
            sols.append(_dot(t16[:, hs], rhs.astype(BF16)).astype(BF16))
        yield
        phi, psi, qt, oin = [], [], [], []
        for e in range(2):
            hs = slice(e * L, (e + 1) * L)
            kdt = (d["k2"][:, hs] * edec[e]).T.astype(BF16)
            r2 = _dot(jnp.concatenate([qkd16[:, hs], kdt], axis=0), sols[e])
            oin.append(r2[:L, :B_DV])
            qt.append(d["q2"][:, hs] * egc[e] - r2[:L, B_DV:])
            psi.append(r2[L:, :B_DV])
            phi.append(-r2[L:, B_DV:])
        d["lhs"] = jnp.concatenate([jnp.concatenate(phi, axis=1), jnp.concatenate(qt, axis=1)], axis=0).astype(BF16)
        d["psi"] = jnp.concatenate(psi, axis=1)
        d["oin"] = jnp.concatenate(oin, axis=1)

    lanes = [slice(p * 2 * L, (p + 1) * 2 * L) for p in range(n_pairs)]
    carried = [(s, p) for s in range(ns) for p in range(n_pairs)]
    s2 = {(s, p): s_ref[s, :, lanes[p]] for s, p in carried}
    nw = nw_ref[...]
    pending = []

    def emit(c, o2):
        r = slice(c * L, (c + 1) * L)
        for (s, p), o in o2.items():
            for e in range(2):
                hs = slice((2 * p + e) * B_DV, (2 * p + e + 1) * B_DV)
                zh = z_ref[s, r, hs].astype(F32)
                o_ref[s, r, hs] = (_rms(o[:, e * B_DV:(e + 1) * B_DV], nw) * (zh * _sigmoid(zh))).astype(BF16)

    def carry_state(c):
        o2 = {}
        for s, p in carried:
            d = st[s, c, p]
            gl = scal[s][1][c]
            res = _dot(d["lhs"], _block_diag(s2[s, p].astype(BF16)))
            o2[s, p] = res[L:] + d["oin"]
            egl = jnp.concatenate([jnp.broadcast_to(jnp.exp(gl[2 * p + e:2 * p + e + 1, :]), (1, B_DV))
                                   for e in range(2)], axis=1)
            s2[s, p] = s2[s, p] * egl + res[:L] + d["psi"]
        if pending:
            emit(*pending.pop())
        pending.append((c, o2))

    waiting = [(key, chain_ops(*key)) for key in sorted(chains, key=lambda k: (k[1], k[0], k[2]))]
    live, finished, next_chunk = [], [0] * nch, 0
    while waiting or live:
        live += waiting[:CHAIN_STARTS_PER_ROUND]
        waiting = waiting[CHAIN_STARTS_PER_ROUND:]
        still = []
        for key, g in live:
            try:
                next(g)
                still.append((key, g))
            except StopIteration:
                finished[key[1]] += 1
        live = still
        while next_chunk < nch and finished[next_chunk] == len(carried):
            carry_state(next_chunk)
            next_chunk += 1
    emit(*pending.pop())
    for s, p in carried:
        s_ref[s, :, lanes[p]] = s2[s, p]


def _delta(q, k, v, z, ab, s0, a_log, dt_bias, norm_w, *, block, streams, n_valid):
    b, t, _ = q.shape
    streams = min(streams, b)
    assert t % block == 0 and block % DELTA_CHUNK == 0 and b % streams == 0
    alog8 = jnp.broadcast_to(jnp.tile(a_log.astype(F32), 2)[:, None], (2 * B_HEADS, LANES))
    dtb8 = jnp.broadcast_to(jnp.tile(dt_bias.astype(F32), 2)[:, None], (2 * B_HEADS, LANES))
    row = lambda w: pl.BlockSpec((streams, block, w), lambda i, j: (i, j, 0))
    state = pl.BlockSpec((streams, B_DK, B_HEADS * B_DV), lambda i, j: (i, 0, 0))
    s0 = s0.astype(F32).transpose(0, 2, 1, 3).reshape(b, B_DK, B_HEADS * B_DV)
    o, s = pl.pallas_call(
        functools.partial(_delta_kernel, n_valid=n_valid),
        grid=(b // streams, t // block),
        in_specs=[row(B_KW), row(B_KW), row(B_VW), row(B_VW), row(LANES), state,
                  _const_spec(alog8.shape), _const_spec(dtb8.shape), _const_spec((1, B_DV))],
        out_specs=[row(B_VW), state],
        out_shape=[jax.ShapeDtypeStruct((b, t, B_VW), BF16),
                   jax.ShapeDtypeStruct((b, B_DK, B_HEADS * B_DV), F32)],
        scratch_shapes=[pltpu.VMEM((streams, LANES, block), F32)],
        compiler_params=_params(("arbitrary", "arbitrary")),
        name="delta_%d" % block,
    )(q, k, v, z, ab, s0, alog8, dtb8, norm_w.reshape(1, B_DV).astype(F32))
    return o, s.reshape(b, B_DK, B_HEADS, B_DV).transpose(0, 2, 1, 3)


def _merge_kernel(x_ref, oa_ref, ob_ref, g_ref, wpa_ref, wpb_ref, wout_ref, gain_ref, o_ref):
    tm = x_ref.shape[0]
    sub = min(ROW_TILE, tm)
    blocks = [slice(r, r + sub) for r in range(0, tm, sub)]

    def branches(rs):
        return _dot(oa_ref[rs, :], wpa_ref[...]), _dot(ob_ref[rs, :], wpb_ref[...])

    nxt = branches(blocks[0])
    for i, rs in enumerate(blocks):
        ya, yb = nxt
        if i + 1 < len(blocks):
            nxt = branches(blocks[i + 1])
        merged = g_ref[rs, :D_MODEL].astype(F32) * ya + g_ref[rs, D_MODEL:].astype(F32) * yb
        y = _dot(merged.astype(BF16), wout_ref[...])
        o_ref[rs, :] = x_ref[rs, :] + _rms(y, gain_ref[...])


def _merge(x, oa, ob, gates, wpa, wpb, wout, gain):
    n, d = x.shape
    tm = min(MERGE_ROWS, n)
    row = lambda w: pl.BlockSpec((tm, w), lambda i: (i, 0))
    return pl.pallas_call(
        _merge_kernel,
        grid=(n // tm,),
        in_specs=[row(d), row(A_WIDTH), row(B_VW), row(2 * d),
                  _const_spec(wpa.shape), _const_spec(wpb.shape), _const_spec(wout.shape), _const_spec(gain.shape)],
        out_specs=row(d),
        out_shape=jax.ShapeDtypeStruct((n, d), F32),
        compiler_params=_params(("arbitrary",)),
        name="merge_%d" % n,
    )(x, oa, ob, gates, wpa, wpb, wout, gain)


FF_SPLIT = 4


def _mlp_kernel(x_ref, gpre_ref, wup_ref, wdown_ref, gpost_ref, o_ref):
    x = x_ref[...]
    hb = _rms(x, gpre_ref[...]).astype(BF16)
    wcols = D_FF // FF_SPLIT
    f = None
    for c in range(FF_SPLIT):
        cs = slice(c * wcols, (c + 1) * wcols)
        up = jnp.maximum(_dot(hb, wup_ref[:, cs]), 0.0)
        part = _dot((up * up).astype(BF16), wdown_ref[cs, :])
        f = part if f is None else f + part
    o_ref[...] = x + _rms(f, gpost_ref[...])


def _mlp(x, gpre, wup, wdown, gpost):
    n, d = x.shape
    tm = min(ROW_TILE, n)
    row = pl.BlockSpec((tm, d), lambda i: (i, 0))
    return pl.pallas_call(
        _mlp_kernel,
        grid=(n // tm,),
        in_specs=[row, _const_spec(gpre.shape), _const_spec(wup.shape), _const_spec(wdown.shape),
                  _const_spec(gpost.shape)],
        out_specs=row,
        out_shape=jax.ShapeDtypeStruct((n, d), F32),
        compiler_params=_params(("arbitrary",)),
        name="mlp_%d" % n,
    )(x, gpre, wup, wdown, gpost)


def _layer_weights(norm_mix_pre, w_in, conv_w, w_proj_a, w_proj_b, w_out, norm_mix_post,
                   norm_mlp_pre, w_up, w_down, norm_mlp_post):
    o_cv = 3 * A_WIDTH
    o_z = o_cv + B_CONV_CH
    o_ab = o_z + B_VW
    o_g = o_ab + 2 * B_HEADS
    wb = w_in.astype(BF16)
    wzab = jnp.concatenate([wb[:, o_z:o_g], jnp.zeros((D_MODEL, LANES - 2 * B_HEADS), BF16)], axis=1)
    vec = lambda g: g.reshape(1, -1).astype(F32)
    return dict(
        gain_in=vec(norm_mix_pre),
        wq=wb[:, :A_WIDTH], wk=wb[:, A_WIDTH:2 * A_WIDTH], wv=wb[:, 2 * A_WIDTH:o_cv],
        wcv=wb[:, o_cv:o_z], wzab=wzab, wg=wb[:, o_g:],
        conv_w=conv_w.astype(F32),
        wpa=w_proj_a.astype(BF16), wpb=w_proj_b.astype(BF16), wout=w_out.astype(BF16),
        gain_mix=vec(norm_mix_post), gain_pre=vec(norm_mlp_pre), gain_post=vec(norm_mlp_post),
        wup=w_up.astype(BF16), wdown=w_down.astype(BF16),
    )


def _tail(x, w, oa, ob, gates):
    b, t, d = x.shape
    x1 = _merge(x.reshape(b * t, d), oa.reshape(b * t, A_WIDTH), ob.reshape(b * t, B_VW),
                gates.reshape(b * t, 2 * d), w["wpa"], w["wpb"], w["wout"], w["gain_mix"])
    return _mlp(x1, w["gain_pre"], w["wup"], w["wdown"], w["gain_post"]).reshape(b, t, d)


def _prompt_layer(x, w, rel_bias, a_log, dt_bias, delta_norm_w):
    b, t, _ = x.shape
    qt, k, vt, dq, dk, dv, z, ab, gates, kf, vf, cvt = _inproj(
        x, w["gain_in"], w["wq"], w["wk"], w["wv"], w["wcv"], w["wzab"], w["wg"], w["conv_w"], prompt=True)
    oa = _attn_prompt(qt, k, vt, _prompt_bias_table(rel_bias))
    s0 = jnp.zeros((b, B_HEADS, B_DK, B_DV), F32)
    ob, s_new = _delta(dq, dk, dv, z, ab, s0, a_log, dt_bias, delta_norm_w,
                       block=DELTA_BLOCK, streams=DELTA_STREAMS, n_valid=DELTA_BLOCK)
    y = _tail(x, w, oa, ob, gates)
    keep = min(A_REACH, t)
    new_k = kf.reshape(b, keep, A_HEADS, A_HEAD_DIM)
    new_v = vf.reshape(b, keep, A_HEADS, A_HEAD_DIM)
    new_conv = cvt[:, SUBLANES - (CONV_WIDTH - 1):, :]
    return y, new_k, new_v, new_conv, s_new


def _sample_layer(x, cache_k, cache_v, conv_state, delta_state, w, rel_bias, a_log, dt_bias, delta_norm_w):
    b, t, d = x.shape
    c = cache_k.shape[1]
    q, k, v, cv, z, ab, gates = _inproj(
        x.reshape(1, b * t, d), w["gain_in"], w["wq"], w["wk"], w["wv"], w["wcv"], w["wzab"], w["wg"],
        prompt=False)
    per_stream = lambda a: a.reshape(b, t, a.shape[-1])
    q, k, v, cv, z, ab, gates = map(per_stream, (q, k, v, cv, z, ab, gates))
    oa = _attn_sample(q, cache_k.reshape(b, c, A_WIDTH), cache_v.reshape(b, c, A_WIDTH), k, v, rel_bias)
    prev = jnp.pad(conv_state.astype(F32), ((0, 0), (SUBLANES - (CONV_WIDTH - 1), 0), (0, 0)))
    dq, dk, dv = _conv_sample(cv, prev, w["conv_w"])
    pad = lambda a: jnp.pad(a, ((0, 0), (0, DELTA_CHUNK - t), (0, 0)))
    ob, s_new = _delta(pad(dq), pad(dk), pad(dv), pad(z), pad(ab), delta_state, a_log, dt_bias, delta_norm_w,
                       block=DELTA_CHUNK, streams=SAMPLE_DELTA_STREAMS, n_valid=t)
    y = _tail(x, w, oa, ob[:, :t], gates)
    new_conv = jnp.concatenate([conv_state.astype(F32), cv], axis=1)[:, -(CONV_WIDTH - 1):]
    return (y, k.reshape(b, t, A_HEADS, A_HEAD_DIM), v.reshape(b, t, A_HEADS, A_HEAD_DIM), new_conv, s_new)


def kernel(x_prompt, x_sample, cache_attn_k, cache_attn_v, state_conv, state_delta, norm_mix_pre, w_in,
           rel_bias, conv_w, a_log, dt_bias, delta_norm_w, w_proj_a, w_proj_b, w_out, norm_mix_post,
           norm_mlp_pre, w_up, w_down, norm_mlp_post):
    depth = w_in.shape[0]
    yp, ys = x_prompt, x_sample
    outs = [[] for _ in range(8)]
    for l in range(depth):
        w = _layer_weights(norm_mix_pre[l], w_in[l], conv_w[l], w_proj_a[l], w_proj_b[l], w_out[l],
                           norm_mix_post[l], norm_mlp_pre[l], w_up[l], w_down[l], norm_mlp_post[l])
        yp, k1, v1, c1, s1 = _prompt_layer(yp, w, rel_bias[l], a_log[l], dt_bias[l], delta_norm_w[l])
        ys, k2, v2, c2, s2 = _sample_layer(ys, cache_attn_k[l], cache_attn_v[l], state_conv[l], state_delta[l],
                                           w, rel_bias[l], a_log[l], dt_bias[l], delta_norm_w[l])
        for lst, val in zip(outs, (k1, v1, c1, s1, k2, v2, c2, s2)):
            lst.append(val)
    return (yp, ys) + tuple(jnp.stack(o) for o in outs)
```

```python
import functools

import numpy as np
import jax
import jax.numpy as jnp
from jax import lax
from jax.experimental import pallas as pl
from jax.experimental.pallas import tpu as pltpu

F32 = jnp.float32
BF16 = jnp.bfloat16

D_MODEL = 1024
CHUNK = 64
RMS_EPS = 1e-6
L2_EPS = 1e-6
A_HEADS = 8
A_HEAD_DIM = 64
A_WIDTH = A_HEADS * A_HEAD_DIM
A_PREV_CHUNKS = 8
A_REACH = A_PREV_CHUNKS * CHUNK
MAX_REL = 128
B_HEADS = 4
B_DK = 128
B_DV = 128
B_KW = B_HEADS * B_DK
B_VW = B_HEADS * B_DV
B_CONV_CH = 2 * B_KW + B_VW
CONV_WIDTH = 4
D_FF = 4 * D_MODEL

LANES = 128
SUBLANES = 8
VMEM_LIMIT_BYTES = 56 * 1024 * 1024

Q_BLOCK = 256
N_KEY_BLOCKS = A_REACH // Q_BLOCK + 1
SCORE_LOOKAHEAD = 2
ATTN_Q_BLOCKS_PER_STEP = 4
ONES_ROWS = 16
LOG2_E = 1.4426950408889634
DELTA_CHUNK = 128
DELTA_BLOCK = 512
DELTA_STREAMS = 2
SAMPLE_DELTA_STREAMS = 8
CHAIN_STARTS_PER_ROUND = 2
ROW_TILE = 512
CONV_ROWS = 64
GATE_COLS = 512
MERGE_ROWS = 1024

assert B_DK == LANES and B_DV == LANES and B_HEADS % 2 == 0


def _dot(a, b):
    return jnp.dot(a, b, preferred_element_type=F32)


def _dot_nt(a, b):
    return lax.dot_general(a, b, (((1,), (1,)), ((), ())), preferred_element_type=F32)


def _sigmoid_of_twice(h):
    return 0.5 * jnp.tanh(h) + 0.5


def _sigmoid(x):
    return _sigmoid_of_twice(0.5 * x)


def _silu_of_twice(h):
    return h * jnp.tanh(h) + h


def _rms(x, gain):
    ms = jnp.mean(x * x, axis=-1, keepdims=True)
    return x * lax.rsqrt(ms + RMS_EPS) * gain


def _params(sem):
    return pltpu.CompilerParams(dimension_semantics=sem, vmem_limit_bytes=VMEM_LIMIT_BYTES)


def _const_spec(shape):
    nd = len(shape)
    return pl.BlockSpec(shape, lambda *_: (0,) * nd, pipeline_mode=pl.Buffered(1))


def _conv_silu(hist, x, w):
    n = x.shape[0]
    full = jnp.concatenate([hist, x], axis=0)
    rb = min(CONV_ROWS, n)
    half_w = 0.5 * w
    out = []
    for r0 in range(0, n, rb):
        piece = full[r0:r0 + rb + SUBLANES, :]
        z = piece * half_w[0:1, :]
        for i in range(1, CONV_WIDTH):
            z = pltpu.roll(z, 1, 0) + piece * half_w[i:i + 1, :]
        out.append(_silu_of_twice(z[SUBLANES:, :]))
    return out[0] if len(out) == 1 else jnp.concatenate(out, axis=0)


def _l2norm_heads(y, scale):
    heads = []
    for h in range(y.shape[1] // B_DK):
        yh = y[:, h * B_DK:(h + 1) * B_DK]
        heads.append(yh * (lax.rsqrt(jnp.sum(yh * yh, axis=-1, keepdims=True) + L2_EPS) * scale))
    return jnp.concatenate(heads, axis=1)


def _conv_silu_l2norm(hist, x, convw_ref):
    y = _conv_silu(hist, x, convw_ref[...])
    return (_l2norm_heads(y[:, :B_KW], B_DK ** -0.5), _l2norm_heads(y[:, B_KW:2 * B_KW], 1.0), y[:, 2 * B_KW:])


def _inproj_kernel(*refs, prompt):
    x_ref, gain_ref, wq_ref, wk_ref, wv_ref, wcv_ref, wzab_ref, wg_ref = refs[:8]
    if prompt:
        (convw_ref, q_ref, k_ref, vt_ref, dq_ref, dk_ref, dv_ref, z_ref, ab_ref, g_ref,
         kf_ref, vf_ref, cvt_ref, hist_ref) = refs[8:]
    else:
        q_ref, kf_ref, vf_ref, cv_ref, z_ref, ab_ref, g_ref = refs[8:]
    if prompt:
        @pl.when(pl.program_id(1) == 0)
        def _():
            hist_ref[...] = jnp.zeros((SUBLANES, B_CONV_CH), F32)

    x = x_ref[0]
    hb = _rms(x, gain_ref[...]).astype(BF16)
    tm = x.shape[0]

    stages = []

    def stage(issue):
        return lambda consume: stages.append((issue, consume))

    if prompt:
        norms = (B_DK ** -0.5, 1.0, None)
        for part, out_ref in enumerate((dq_ref, dk_ref, dv_ref)):
            cs = slice(part * B_KW, (part + 1) * B_KW)

            @stage(lambda cs=cs: _dot(hb, wcv_ref[:, cs]))
            def _(cv, cs=cs, out_ref=out_ref, scale=norms[part]):
                y = _conv_silu(hist_ref[:, cs], cv, convw_ref[:, cs])
                out_ref[0] = (y if scale is None else _l2norm_heads(y, scale)).astype(BF16)
                hist_ref[:, cs] = cv[tm - SUBLANES:, :]
                cvt_ref[0, :, cs] = cv[tm - SUBLANES:, :]
    else:
        @stage(lambda: _dot(hb, wcv_ref[...]))
        def _(cv):
            cv_ref[0] = cv

    def feature_major(val, out_ref):
        vt = val.T.astype(BF16)
        for j in range(tm // Q_BLOCK):
            out_ref[0, j] = vt[:, j * Q_BLOCK:(j + 1) * Q_BLOCK]

    @stage(lambda: _dot(hb, wq_ref[...]))
    def _(q):
        if prompt:
            feature_major(q * (A_HEAD_DIM ** -0.5 * LOG2_E), q_ref)
        else:
            q_ref[0] = (q * (A_HEAD_DIM ** -0.5)).astype(BF16)

    @stage(lambda: _dot(hb, wk_ref[...]))
    def _(k):
        kf_ref[0] = k
        if prompt:
            k_ref[0] = k.astype(BF16)

    @stage(lambda: _dot(hb, wv_ref[...]))
    def _(v):
        vf_ref[0] = v
        if prompt:
            feature_major(v, vt_ref)

    @stage(lambda: _dot(hb, wzab_ref[...]))
    def _(zab):
        z_ref[0] = zab[:, :B_VW].astype(BF16)
        ab_ref[0] = zab[:, B_VW:]

    for c0 in range(0, 2 * D_MODEL, GATE_COLS):
        cs = slice(c0, c0 + GATE_COLS)

        @stage(lambda cs=cs: _dot(hb, wg_ref[:, cs]))
        def _(g, cs=cs):
            g_ref[0, :, cs] = _sigmoid_of_twice(g).astype(BF16)

    pending = None
    for issue, consume in stages:
        val = issue()
        if pending is not None:
            pending[1](pending[0])
        pending = (val, consume)
    pending[1](pending[0])


def _inproj(x, gain, wq, wk, wv, wcv, wzab, wg, conv_w=None, *, prompt):
    b, t, d = x.shape
    tm = min(ROW_TILE, t)
    assert t % tm == 0
    row = lambda w, dt: (jax.ShapeDtypeStruct((b, t, w), dt), pl.BlockSpec((1, tm, w), lambda i, j: (i, j, 0)))
    stream = lambda r, w: (jax.ShapeDtypeStruct((b, r, w), F32), pl.BlockSpec((1, r, w), lambda i, j: (i, 0, 0)))
    if prompt:
        assert tm == A_REACH and tm % Q_BLOCK == 0
        nq = tm // Q_BLOCK
        feat = (jax.ShapeDtypeStruct((b, t // Q_BLOCK, A_WIDTH, Q_BLOCK), BF16),
                pl.BlockSpec((1, nq, A_WIDTH, Q_BLOCK), lambda i, j: (i, j, 0, 0)))
        outs = [feat, row(A_WIDTH, BF16), feat, row(B_KW, BF16), row(B_KW, BF16), row(B_VW, BF16)]
    else:
        outs = [row(A_WIDTH, BF16), row(A_WIDTH, F32), row(A_WIDTH, F32), row(B_CONV_CH, F32)]
    outs += [row(B_VW, BF16), row(LANES, F32), row(2 * D_MODEL, BF16)]
    weights = [wq, wk, wv, wcv, wzab, wg]
    scratch = []
    if prompt:
        outs += [stream(tm, A_WIDTH), stream(tm, A_WIDTH), stream(SUBLANES, B_CONV_CH)]
        weights.append(conv_w)
        scratch.append(pltpu.VMEM((SUBLANES, B_CONV_CH), F32))
    in_specs = [pl.BlockSpec((1, tm, d), lambda i, j: (i, j, 0)), _const_spec(gain.shape)]
    in_specs += [_const_spec(w.shape) for w in weights]
    return pl.pallas_call(
        functools.partial(_inproj_kernel, prompt=prompt),
        grid=(b, t // tm),
        in_specs=in_specs,
        out_specs=[o[1] for o in outs],
        out_shape=[o[0] for o in outs],
        scratch_shapes=scratch,
        compiler_params=_params(("arbitrary", "arbitrary")),
        name="inproj_prompt" if prompt else "inproj_sample",
    )(x, gain, *weights)


def _conv_sample_kernel(cv_ref, prev_ref, convw_ref, dq_ref, dk_ref, dv_ref):
    dq, dk, dv = _conv_silu_l2norm(prev_ref[0], cv_ref[0], convw_ref)
    dq_ref[0] = dq.astype(BF16)
    dk_ref[0] = dk.astype(BF16)
    dv_ref[0] = dv.astype(BF16)


def _conv_sample(cv, prev, conv_w):
    b, t, _ = cv.shape
    blk = lambda r, w: pl.BlockSpec((1, r, w), lambda i: (i, 0, 0))
    out = jax.ShapeDtypeStruct((b, t, B_KW), BF16)
    return pl.pallas_call(
        _conv_sample_kernel,
        grid=(b,),
        in_specs=[blk(t, B_CONV_CH), blk(SUBLANES, B_CONV_CH), _const_spec(conv_w.shape)],
        out_specs=[blk(t, B_KW)] * 3,
        out_shape=[out] * 3,
        compiler_params=_params(("arbitrary",)),
        name="conv_sample",
    )(cv, prev, conv_w)


def _toeplitz_bias(rel_bias, rows, width, offset, sign=1, scale=1.0):
    pitch = width + rows
    period = pitch + 1
    j = np.arange(period)
    rel = sign * np.where(j < width, offset - j, offset + period - j)
    idx = np.clip(rel, -MAX_REL, MAX_REL) + MAX_REL
    vec = rel_bias.astype(F32)[:, idx] * scale
    tiled = jnp.tile(vec, (1, rows))[:, :rows * pitch]
    return tiled.reshape(rel_bias.shape[0], rows, pitch)[:, :, :width]


def _prompt_bias_table(rel_bias):
    kc = np.arange(Q_BLOCK)[:, None] // CHUNK
    qc = np.arange(Q_BLOCK)[None, :] // CHUNK
    blocks = []
    for jb in range(N_KEY_BLOCKS):
        bias = _toeplitz_bias(rel_bias, Q_BLOCK, Q_BLOCK, jb * Q_BLOCK - A_REACH, sign=-1, scale=LOG2_E)
        kcw = kc + jb * (Q_BLOCK // CHUNK)
        band = (kcw >= qc) & (kcw <= qc + A_PREV_CHUNKS)
        blocks.append(jnp.where(band[None], bias, -jnp.inf))
    blocks.append(jnp.full((A_HEADS, Q_BLOCK, Q_BLOCK), -jnp.inf, F32))
    return jnp.stack(blocks, axis=1)


def _softmax_pv(scores, values):
    m = None
    for s in scores:
        mj = jnp.max(s, axis=-1, keepdims=True)
        m = mj if m is None else jnp.maximum(m, mj)
    denom = None
    acc = None
    for s, vals in zip(scores, values):
        p = jnp.exp(s - m)
        dj = jnp.sum(p, axis=-1, keepdims=True)
        denom = dj if denom is None else denom + dj
        pv = _dot(p.astype(BF16), vals)
        acc = pv if acc is None else acc + pv
    return acc / denom


def _attn_prompt_kernel(qt_ref, k_ref, vt_ref, bias_ref, o_ref):
    n_sub = qt_ref.shape[1]
    row = lax.broadcasted_iota(jnp.int32, (LANES, Q_BLOCK), 0)
    low = row < A_HEAD_DIM
    kbs, bsels = {}, {}
    for u in range(n_sub):
        qi = pl.program_id(1) * n_sub + u
        for j in range(N_KEY_BLOCKS):
            kraw = qi - (N_KEY_BLOCKS - 1) + j
            kbs[u, j] = jnp.maximum(kraw, 0)
            bsels[u, j] = jnp.where(kraw >= 0, j, N_KEY_BLOCKS)
    n_qcols = Q_BLOCK // LANES

    def scores(u, h):
        cols = slice((h // 2) * LANES, (h // 2 + 1) * LANES)
        qp = qt_ref[0, u, cols, :]
        qm = jnp.where(low if h % 2 == 0 else jnp.logical_not(low), qp, jnp.zeros_like(qp))
        out = []
        for j in range(N_KEY_BLOCKS):
            kblk = k_ref[0, pl.ds(pl.multiple_of(kbs[u, j] * Q_BLOCK, Q_BLOCK), Q_BLOCK), cols]
            out.append(_dot(kblk, qm) + bias_ref[h, bsels[u, j]])
        return out

    def softmax_pv(u, h, s):
        cols = slice((h // 2) * LANES, (h // 2 + 1) * LANES)
        p_cols = []
        for qc in range(n_qcols):
            ls = slice(qc * LANES, (qc + 1) * LANES)
            lo, hi = qc * LANES, qc * LANES + A_REACH + LANES
            spans = [(max(lo - j * Q_BLOCK, 0), min(hi - j * Q_BLOCK, Q_BLOCK)) for j in range(N_KEY_BLOCKS)]
            parts = [s[j][a:b, ls] for j, (a, b) in enumerate(spans)]
            m = functools.reduce(jnp.maximum, [jnp.max(x, axis=0, keepdims=True) for x in parts])
            ps = [jnp.exp2(x - m) for x in parts]
            padded = []
            for x, (a, b) in zip(ps, spans):
                pieces = ([jnp.zeros((a, LANES), F32)] if a else []) + [x]
                pieces += [jnp.zeros((Q_BLOCK - b, LANES), F32)] if b < Q_BLOCK else []
                padded.append(pieces[0] if len(pieces) == 1 else jnp.concatenate(pieces, axis=0))
            p_cols.append(padded)
        ones = jnp.ones((ONES_ROWS, Q_BLOCK), BF16)
        acc = None
        for j in range(N_KEY_BLOCKS):
            pt = jnp.concatenate([p_cols[qc][j] for qc in range(n_qcols)], axis=1).astype(BF16)
            pv = _dot(jnp.concatenate([vt_ref[0, kbs[u, j], cols, :], ones], axis=0), pt)
            acc = pv if acc is None else acc + pv
        return acc[:LANES] * (1.0 / acc[LANES:LANES + 1])

    tasks = [(u, h) for u in range(n_sub) for h in range(A_HEADS)]
    ready = [scores(*t) for t in tasks[:SCORE_LOOKAHEAD]]
    pair_out = []
    for i, (u, h) in enumerate(tasks):
        if i + SCORE_LOOKAHEAD < len(tasks):
            ready.append(scores(*tasks[i + SCORE_LOOKAHEAD]))
        pair_out.append(softmax_pv(u, h, ready.pop(0)))
        if h % 2 == 1:
            cols = slice((h // 2) * LANES, (h // 2 + 1) * LANES)
            o_pair = jnp.concatenate([pair_out[0][:A_HEAD_DIM], pair_out[1][A_HEAD_DIM:]], axis=0)
            o_ref[0, u * Q_BLOCK:(u + 1) * Q_BLOCK, cols] = o_pair.T.astype(BF16)
            pair_out = []


def _attn_prompt(qt, k, vt, bias):
    b, t, _ = k.shape
    n_sub = ATTN_Q_BLOCKS_PER_STEP
    assert t % (n_sub * Q_BLOCK) == 0
    return pl.pallas_call(
        _attn_prompt_kernel,
        grid=(b, t // (n_sub * Q_BLOCK)),
        in_specs=[
            pl.BlockSpec((1, n_sub, A_WIDTH, Q_BLOCK), lambda i, j: (i, j, 0, 0)),
            pl.BlockSpec((1, t, A_WIDTH), lambda i, j: (i, 0, 0)),
            pl.BlockSpec((1, t // Q_BLOCK, A_WIDTH, Q_BLOCK), lambda i, j: (i, 0, 0, 0)),
            _const_spec(bias.shape),
        ],
        out_specs=pl.BlockSpec((1, n_sub * Q_BLOCK, A_WIDTH), lambda i, j: (i, j, 0)),
        out_shape=jax.ShapeDtypeStruct((b, t, A_WIDTH), BF16),
        compiler_params=_params(("arbitrary", "arbitrary")),
        name="attn_prompt",
    )(qt, k, vt, bias)


def _attn_sample_kernel(q_ref, ck_ref, cv_ref, nk_ref, nv_ref, bc_ref, bn_ref, o_ref):
    rows = q_ref.shape[1]
    lane = lax.broadcasted_iota(jnp.int32, (rows, LANES), 1)
    low = lane < A_HEAD_DIM
    pair_cols = [slice(hp * LANES, (hp + 1) * LANES) for hp in range(A_HEADS // 2)]
    scores = []
    for h in range(A_HEADS):
        cols = pair_cols[h // 2]
        qp = q_ref[0, :, cols]
        qm = jnp.where(low if h % 2 == 0 else jnp.logical_not(low), qp, jnp.zeros_like(qp))
        scores.append([_dot_nt(qm, ck_ref[0, :, cols].astype(BF16)) + bc_ref[h],
                       _dot_nt(qm, nk_ref[0, :, cols].astype(BF16)) + bn_ref[h]])
    outs = [_softmax_pv(scores[h], [cv_ref[0, :, pair_cols[h // 2]].astype(BF16),
                                    nv_ref[0, :, pair_cols[h // 2]].astype(BF16)]) for h in range(A_HEADS)]
    for hp, cols in enumerate(pair_cols):
        o_ref[0, :, cols] = jnp.where(low, outs[2 * hp], outs[2 * hp + 1]).astype(BF16)


def _attn_sample(q, cache_k, cache_v, new_k, new_v, rel_bias):
    b, t, _ = q.shape
    c = cache_k.shape[1]
    bias = _toeplitz_bias(rel_bias, t, c + LANES, c)
    bias = jnp.where((np.arange(c + LANES) < c + t)[None, None, :], bias, -jnp.inf)
    bias_c, bias_n = bias[:, :, :c], bias[:, :, c:]
    pad = ((0, 0), (0, LANES - t), (0, 0))
    nk, nv = jnp.pad(new_k, pad), jnp.pad(new_v, pad)
    blk = lambda r: pl.BlockSpec((1, r, A_WIDTH), lambda i: (i, 0, 0))
    return pl.pallas_call(
        _attn_sample_kernel,
        grid=(b,),
        in_specs=[blk(t), blk(c), blk(c), blk(LANES), blk(LANES),
                  _const_spec(bias_c.shape), _const_spec(bias_n.shape)],
        out_specs=blk(t),
        out_shape=jax.ShapeDtypeStruct((b, t, A_WIDTH), BF16),
        compiler_params=_params(("arbitrary",)),
        name="attn_sample",
    )(q, cache_k, cache_v, nk, nv, bias_c, bias_n)


def _block_diag(x2):
    half = x2.shape[1] // 2
    z = jnp.zeros_like(x2[:, :half])
    top = jnp.concatenate([x2[:, :half], z], axis=1)
    bot = jnp.concatenate([z, x2[:, half:]], axis=1)
    return jnp.concatenate([top, bot], axis=0)


def _quad_diag(x4):
    n = x4.shape[0]
    group = lax.broadcasted_iota(jnp.int32, x4.shape, 1) // n
    zero = jnp.zeros_like(x4)
    return jnp.concatenate([jnp.where(group == g, x4, zero) for g in range(4)], axis=0)


def _unit_lower_inverse(m, eye_half):
    L = m.shape[0]
    h = L // 2
    lane = lax.broadcasted_iota(jnp.int32, (h, 2 * L), 1)
    first_half = (lane & (L - 1)) < h
    left_head = lane < L
    diag = jnp.where(first_half, m[:h], m[h:])
    b16 = jnp.where(first_half, m[h:], 0.0).astype(BF16)
    t = eye_half + diag
    d16 = diag.astype(BF16)
    p = _dot(d16, _quad_diag(d16))
    yield
    n_levels = h.bit_length() - 2
    for lvl in range(n_levels):
        p16 = p.astype(BF16)
        t16 = t.astype(BF16)
        if lvl == n_levels - 1:
            t = t + _dot(t16, _quad_diag(p16))
        else:
            res = _dot(jnp.concatenate([t16, p16], axis=0), _quad_diag(p16))
            t = t + res[:h]
            p = res[h:]
        yield
    t16 = t.astype(BF16)
    zero16 = jnp.zeros_like(t16)
    ta_only = jnp.where(first_half, t16, zero16)
    w_a = jnp.concatenate([jnp.where(left_head, ta_only, zero16), zero16,
                           jnp.where(left_head, zero16, ta_only), zero16], axis=0)
    y16 = _dot(b16, w_a).astype(BF16)
    yield
    w_y = jnp.concatenate([zero16, jnp.where(left_head, y16, zero16),
                           zero16, jnp.where(left_head, zero16, y16)], axis=0)
    t_b = _dot(t16, w_y)
    return jnp.where(first_half, t, 0.0), jnp.where(first_half, t_b, t)


def _round_robin(generators):
    results = [None] * len(generators)
    active = list(enumerate(generators))
    while active:
        still = []
        for i, g in active:
            try:
                next(g)
                still.append((i, g))
            except StopIteration as stop:
                results[i] = stop.value
        active = still
    return results


def _delta_kernel(q_ref, k_ref, v_ref, z_ref, ab_ref, s0_ref, alog_ref, dtb_ref, nw_ref,
                  o_ref, s_ref, rows_ref, *, n_valid):
    L = DELTA_CHUNK
    ns, tb = q_ref.shape[0], q_ref.shape[1]
    nch = tb // L
    n_pairs = B_HEADS // 2

    @pl.when(pl.program_id(1) == 0)
    def _():
        s_ref[...] = s0_ref[...]

    chains = [(s, c, p) for s in range(ns) for c in range(nch) for p in range(n_pairs)]

    st = {}
    for s, c, p in chains:
        r = slice(c * L, (c + 1) * L)
        ls = slice(p * 2 * L, (p + 1) * 2 * L)
        q2 = q_ref[s, r, ls]
        k2 = k_ref[s, r, ls]
        sc = _dot_nt(jnp.concatenate([q2, k2], axis=0), _block_diag(k2))
        st[s, c, p] = dict(q2=q2.astype(F32), k2=k2.astype(F32), qk=sc[:L], kk=sc[L:])

    scal = _round_robin([_delta_scalars(ab_ref, alog_ref, dtb_ref, rows_ref, s, n_valid) for s in range(ns)])
    _delta_chunks(st, chains, scal, v_ref, z_ref, nw_ref, o_ref, s_ref)


def _delta_scalars(ab_ref, alog_ref, dtb_ref, rows_ref, s, n_valid):
    L = DELTA_CHUNK
    tb = ab_ref.shape[1]
    nch = tb // L
    ab_t = ab_ref[s].T
    rows8 = ab_t[0:2 * B_HEADS, :]
    x = rows8 + dtb_ref[:, 0:1]
    softplus = jnp.maximum(x, 0.0) + jnp.log(1.0 + jnp.exp(-jnp.abs(x)))
    g8 = -jnp.exp(alog_ref[:, 0:1]) * softplus
    beta8 = _sigmoid(rows8)
    tok = lax.broadcasted_iota(jnp.int32, (2 * B_HEADS, tb), 1)
    if n_valid < tb:
        g8 = jnp.where(tok < n_valid, g8, 0.0)
        beta8 = jnp.where(tok < n_valid, beta8, 0.0)
    pos = tok & (L - 1)
    gc8 = g8
    shift = 1
    while shift < L:
        yield
        gc8 = gc8 + jnp.where(pos >= shift, pltpu.roll(gc8, shift, 1), 0.0)
        shift *= 2
    gl8 = [gc8[:, (c + 1) * L - 1:(c + 1) * L] for c in range(nch)]
    glb8 = jnp.concatenate([jnp.broadcast_to(g, (2 * B_HEADS, L)) for g in gl8], axis=1)
    rows_ref[s] = jnp.zeros(rows_ref.shape[1:], F32)
    rows_ref[s, 0:8, :] = gc8
    rows_ref[s, 8:16, :] = beta8
    rows_ref[s, 16:24, :] = jnp.exp(gc8)
    rows_ref[s, 24:32, :] = jnp.exp(glb8 - gc8)
    cols = rows_ref[s].T
    return gc8, gl8, cols


def _delta_chunks(st, chains, scal, v_ref, z_ref, nw_ref, o_ref, s_ref):
    L = DELTA_CHUNK
    ns, tb = v_ref.shape[0], v_ref.shape[1]
    nch = tb // L
    n_pairs = B_HEADS // 2
    ri = lax.broadcasted_iota(jnp.int32, (L, 2 * L), 0)
    ci = lax.broadcasted_iota(jnp.int32, (L, 2 * L), 1) & (L - 1)
    tril = ri >= ci
    strict = ri > ci
    rh = lax.broadcasted_iota(jnp.int32, (L // 2, 2 * L), 0)
    ch = lax.broadcasted_iota(jnp.int32, (L // 2, 2 * L), 1) & (L // 2 - 1)
    eye_half = (rh == ch).astype(F32)

    def chain_ops(s, c, p):
        gc8, _, cols = scal[s]
        r = slice(c * L, (c + 1) * L)
        ls = slice(p * 2 * L, (p + 1) * 2 * L)
        h0 = 2 * p
        d = st[s, c, p]
        pair_cols = lambda i: [jnp.broadcast_to(cols[r, i + e:i + e + 1], (L, L)) for e in range(2)]
        gc_c = jnp.concatenate(pair_cols(h0), axis=1)
        beta = pair_cols(8 + B_HEADS + h0)
        egc = pair_cols(16 + h0)
        edec = pair_cols(24 + h0)
        gc_r = jnp.concatenate([jnp.broadcast_to(gc8[h0 + e:h0 + e + 1, r], (L, L)) for e in range(2)], axis=1)
        decay = jnp.exp(jnp.where(tril, gc_c - gc_r, -jnp.inf))
        m = jnp.where(strict, -(jnp.concatenate(beta, axis=1) * d.pop("kk") * decay), 0.0)
        qkd16 = (d.pop("qk") * decay).astype(BF16)
        v2 = v_ref[s, r, ls].astype(F32)
        yield
        top, bot = yield from _unit_lower_inverse(m, eye_half)
        yield
        t16 = jnp.concatenate([top, bot], axis=0).astype(BF16)
        sols = []
        for e in range(2):
            hs = slice(e * L, (e + 1) * L)
            kh = d["k2"][:, hs]
            rhs = jnp.concatenate([v2[:, hs] * beta[e], kh * (beta[e] * egc[e])], axis=1)
            sols.append(_dot(t16[:, hs], rhs.astype(BF16)).astype(BF16))
        yield
        phi, psi, qt, oin = [], [], [], []
        for e in range(2):
            hs = slice(e * L, (e + 1) * L)
            kdt = (d["k2"][:, hs] * edec[e]).T.astype(BF16)
            r2 = _dot(jnp.concatenate([qkd16[:, hs], kdt], axis=0), sols[e])
            oin.append(r2[:L, :B_DV])
            qt.append(d["q2"][:, hs] * egc[e] - r2[:L, B_DV:])
            psi.append(r2[L:, :B_DV])
            phi.append(-r2[L:, B_DV:])
        d["lhs"] = jnp.concatenate([jnp.concatenate(phi, axis=1), jnp.concatenate(qt, axis=1)], axis=0).astype(BF16)
        d["psi"] = jnp.concatenate(psi, axis=1)
        d["oin"] = jnp.concatenate(oin, axis=1)

    lanes = [slice(p * 2 * L, (p + 1) * 2 * L) for p in range(n_pairs)]
    carried = [(s, p) for s in range(ns) for p in range(n_pairs)]
    s2 = {(s, p): s_ref[s, :, lanes[p]] for s, p in carried}
    nw = nw_ref[...]
    pending = []

    def emit(c, o2):
        r = slice(c * L, (c + 1) * L)
        for (s, p), o in o2.items():
            for e in range(2):
                hs = slice((2 * p + e) * B_DV, (2 * p + e + 1) * B_DV)
                zh = z_ref[s, r, hs].astype(F32)
                o_ref[s, r, hs] = (_rms(o[:, e * B_DV:(e + 1) * B_DV], nw) * _silu_of_twice(0.5 * zh)).astype(BF16)

    def carry_state(c):
        o2 = {}
        for s, p in carried:
            d = st[s, c, p]
            gl = scal[s][1][c]
            res = _dot(d["lhs"], _block_diag(s2[s, p].astype(BF16)))
            o2[s, p] = res[L:] + d["oin"]
            egl = jnp.concatenate([jnp.broadcast_to(jnp.exp(gl[2 * p + e:2 * p + e + 1, :]), (1, B_DV))
                                   for e in range(2)], axis=1)
            s2[s, p] = s2[s, p] * egl + res[:L] + d["psi"]
        if pending:
            emit(*pending.pop())
        pending.append((c, o2))

    waiting = [(key, chain_ops(*key)) for key in sorted(chains, key=lambda k: (k[1], k[0], k[2]))]
    live, finished, next_chunk = [], [0] * nch, 0
    while waiting or live:
        live += waiting[:CHAIN_STARTS_PER_ROUND]
        waiting = waiting[CHAIN_STARTS_PER_ROUND:]
        still = []
        for key, g in live:
            try:
                next(g)
                still.append((key, g))
            except StopIteration:
                finished[key[1]] += 1
        live = still
        while next_chunk < nch and finished[next_chunk] == len(carried):
            carry_state(next_chunk)
            next_chunk += 1
    emit(*pending.pop())
    for s, p in carried:
        s_ref[s, :, lanes[p]] = s2[s, p]


def _delta(q, k, v, z, ab, s0, a_log, dt_bias, norm_w, *, block, streams, n_valid):
    b, t, _ = q.shape
    streams = min(streams, b)
    assert t % block == 0 and block % DELTA_CHUNK == 0 and b % streams == 0
    alog8 = jnp.broadcast_to(jnp.tile(a_log.astype(F32), 2)[:, None], (2 * B_HEADS, LANES))
    dtb8 = jnp.broadcast_to(jnp.tile(dt_bias.astype(F32), 2)[:, None], (2 * B_HEADS, LANES))
    row = lambda w: pl.BlockSpec((streams, block, w), lambda i, j: (i, j, 0))
    state = pl.BlockSpec((streams, B_DK, B_HEADS * B_DV), lambda i, j: (i, 0, 0))
    s0 = s0.astype(F32).transpose(0, 2, 1, 3).reshape(b, B_DK, B_HEADS * B_DV)
    o, s = pl.pallas_call(
        functools.partial(_delta_kernel, n_valid=n_valid),
        grid=(b // streams, t // block),
        in_specs=[row(B_KW), row(B_KW), row(B_VW), row(B_VW), row(LANES), state,
                  _const_spec(alog8.shape), _const_spec(dtb8.shape), _const_spec((1, B_DV))],
        out_specs=[row(B_VW), state],
        out_shape=[jax.ShapeDtypeStruct((b, t, B_VW), BF16),
                   jax.ShapeDtypeStruct((b, B_DK, B_HEADS * B_DV), F32)],
        scratch_shapes=[pltpu.VMEM((streams, LANES, block), F32)],
        compiler_params=_params(("arbitrary", "arbitrary")),
        name="delta_%d" % block,
    )(q, k, v, z, ab, s0, alog8, dtb8, norm_w.reshape(1, B_DV).astype(F32))
    return o, s.reshape(b, B_DK, B_HEADS, B_DV).transpose(0, 2, 1, 3)


def _merge_kernel(x_ref, oa_ref, ob_ref, g_ref, wpa_ref, wpb_ref, wout_ref, gain_ref, o_ref):
    tm = x_ref.shape[0]
    sub = min(ROW_TILE, tm)
    blocks = [slice(r, r + sub) for r in range(0, tm, sub)]

    def branches(rs):
        return _dot(oa_ref[rs, :], wpa_ref[...]), _dot(ob_ref[rs, :], wpb_ref[...])

    nxt = branches(blocks[0])
    for i, rs in enumerate(blocks):
        ya, yb = nxt
        if i + 1 < len(blocks):
            nxt = branches(blocks[i + 1])
        merged = g_ref[rs, :D_MODEL].astype(F32) * ya + g_ref[rs, D_MODEL:].astype(F32) * yb
        y = _dot(merged.astype(BF16), wout_ref[...])
        o_ref[rs, :] = x_ref[rs, :] + _rms(y, gain_ref[...])


def _merge(x, oa, ob, gates, wpa, wpb, wout, gain):
    n, d = x.shape
    tm = min(MERGE_ROWS, n)
    row = lambda w: pl.BlockSpec((tm, w), lambda i: (i, 0))
    return pl.pallas_call(
        _merge_kernel,
        grid=(n // tm,),
        in_specs=[row(d), row(A_WIDTH), row(B_VW), row(2 * d),
                  _const_spec(wpa.shape), _const_spec(wpb.shape), _const_spec(wout.shape), _const_spec(gain.shape)],
        out_specs=row(d),
        out_shape=jax.ShapeDtypeStruct((n, d), F32),
        compiler_params=_params(("arbitrary",)),
        name="merge_%d" % n,
    )(x, oa, ob, gates, wpa, wpb, wout, gain)


FF_SPLIT = 4


def _mlp_kernel(x_ref, gpre_ref, wup_ref, wdown_ref, gpost_ref, o_ref):
    x = x_ref[...]
    hb = _rms(x, gpre_ref[...]).astype(BF16)
    wcols = D_FF // FF_SPLIT
    f = None
    for c in range(FF_SPLIT):
        cs = slice(c * wcols, (c + 1) * wcols)
        up = jnp.maximum(_dot(hb, wup_ref[:, cs]), 0.0)
        part = _dot((up * up).astype(BF16), wdown_ref[cs, :])
        f = part if f is None else f + part
    o_ref[...] = x + _rms(f, gpost_ref[...])


def _mlp(x, gpre, wup, wdown, gpost):
    n, d = x.shape
    tm = min(ROW_TILE, n)
    row = pl.BlockSpec((tm, d), lambda i: (i, 0))
    return pl.pallas_call(
        _mlp_kernel,
        grid=(n // tm,),
        in_specs=[row, _const_spec(gpre.shape), _const_spec(wup.shape), _const_spec(wdown.shape),
                  _const_spec(gpost.shape)],
        out_specs=row,
        out_shape=jax.ShapeDtypeStruct((n, d), F32),
        compiler_params=_params(("arbitrary",)),
        name="mlp_%d" % n,
    )(x, gpre, wup, wdown, gpost)


def _layer_weights(norm_mix_pre, w_in, conv_w, w_proj_a, w_proj_b, w_out, norm_mix_post,
                   norm_mlp_pre, w_up, w_down, norm_mlp_post):
    o_cv = 3 * A_WIDTH
    o_z = o_cv + B_CONV_CH
    o_ab = o_z + B_VW
    o_g = o_ab + 2 * B_HEADS
    wb = w_in.astype(BF16)
    wzab = jnp.concatenate([wb[:, o_z:o_g], jnp.zeros((D_MODEL, LANES - 2 * B_HEADS), BF16)], axis=1)
    vec = lambda g: g.reshape(1, -1).astype(F32)
    return dict(
        gain_in=vec(norm_mix_pre),
        wq=wb[:, :A_WIDTH], wk=wb[:, A_WIDTH:2 * A_WIDTH], wv=wb[:, 2 * A_WIDTH:o_cv],
        wcv=wb[:, o_cv:o_z], wzab=wzab, wg=wb[:, o_g:] * 0.5,
        conv_w=conv_w.astype(F32),
        wpa=w_proj_a.astype(BF16), wpb=w_proj_b.astype(BF16), wout=w_out.astype(BF16),
        gain_mix=vec(norm_mix_post), gain_pre=vec(norm_mlp_pre), gain_post=vec(norm_mlp_post),
        wup=w_up.astype(BF16), wdown=w_down.astype(BF16),
    )


def _tail(x, w, oa, ob, gates):
    b, t, d = x.shape
    x1 = _merge(x.reshape(b * t, d), oa.reshape(b * t, A_WIDTH), ob.reshape(b * t, B_VW),
                gates.reshape(b * t, 2 * d), w["wpa"], w["wpb"], w["wout"], w["gain_mix"])
    return _mlp(x1, w["gain_pre"], w["wup"], w["wdown"], w["gain_post"]).reshape(b, t, d)


def _prompt_layer(x, w, rel_bias, a_log, dt_bias, delta_norm_w):
    b, t, _ = x.shape
    qt, k, vt, dq, dk, dv, z, ab, gates, kf, vf, cvt = _inproj(
        x, w["gain_in"], w["wq"], w["wk"], w["wv"], w["wcv"], w["wzab"], w["wg"], w["conv_w"], prompt=True)
    oa = _attn_prompt(qt, k, vt, _prompt_bias_table(rel_bias))
    s0 = jnp.zeros((b, B_HEADS, B_DK, B_DV), F32)
    ob, s_new = _delta(dq, dk, dv, z, ab, s0, a_log, dt_bias, delta_norm_w,
                       block=DELTA_BLOCK, streams=DELTA_STREAMS, n_valid=DELTA_BLOCK)
    y = _tail(x, w, oa, ob, gates)
    keep = min(A_REACH, t)
    new_k = kf.reshape(b, keep, A_HEADS, A_HEAD_DIM)
    new_v = vf.reshape(b, keep, A_HEADS, A_HEAD_DIM)
    new_conv = cvt[:, SUBLANES - (CONV_WIDTH - 1):, :]
    return y, new_k, new_v, new_conv, s_new


def _sample_layer(x, cache_k, cache_v, conv_state, delta_state, w, rel_bias, a_log, dt_bias, delta_norm_w):
    b, t, d = x.shape
    c = cache_k.shape[1]
    q, k, v, cv, z, ab, gates = _inproj(
        x.reshape(1, b * t, d), w["gain_in"], w["wq"], w["wk"], w["wv"], w["wcv"], w["wzab"], w["wg"],
        prompt=False)
    per_stream = lambda a: a.reshape(b, t, a.shape[-1])
    q, k, v, cv, z, ab, gates = map(per_stream, (q, k, v, cv, z, ab, gates))
    oa = _attn_sample(q, cache_k.reshape(b, c, A_WIDTH), cache_v.reshape(b, c, A_WIDTH), k, v, rel_bias)
    prev = jnp.pad(conv_state.astype(F32), ((0, 0), (SUBLANES - (CONV_WIDTH - 1), 0), (0, 0)))
    dq, dk, dv = _conv_sample(cv, prev, w["conv_w"])
    pad = lambda a: jnp.pad(a, ((0, 0), (0, DELTA_CHUNK - t), (0, 0)))
    ob, s_new = _delta(pad(dq), pad(dk), pad(dv), pad(z), pad(ab), delta_state, a_log, dt_bias, delta_norm_w,
                       block=DELTA_CHUNK, streams=SAMPLE_DELTA_STREAMS, n_valid=t)
    y = _tail(x, w, oa, ob[:, :t], gates)
    new_conv = jnp.concatenate([conv_state.astype(F32), cv], axis=1)[:, -(CONV_WIDTH - 1):]
    return (y, k.reshape(b, t, A_HEADS, A_HEAD_DIM), v.reshape(b, t, A_HEADS, A_HEAD_DIM), new_conv, s_new)


def kernel(x_prompt, x_sample, cache_attn_k, cache_attn_v, state_conv, state_delta, norm_mix_pre, w_in,
           rel_bias, conv_w, a_log, dt_bias, delta_norm_w, w_proj_a, w_proj_b, w_out, norm_mix_post,
           norm_mlp_pre, w_up, w_down, norm_mlp_post):
    depth = w_in.shape[0]
    yp, ys = x_prompt, x_sample
    outs = [[] for _ in range(8)]
    for l in range(depth):
        w = _layer_weights(norm_mix_pre[l], w_in[l], conv_w[l], w_proj_a[l], w_proj_b[l], w_out[l],
                           norm_mix_post[l], norm_mlp_pre[l], w_up[l], w_down[l], norm_mlp_post[l])
        yp, k1, v1, c1, s1 = _prompt_layer(yp, w, rel_bias[l], a_log[l], dt_bias[l], delta_norm_w[l])
        ys, k2, v2, c2, s2 = _sample_layer(ys, cache_attn_k[l], cache_attn_v[l], state_conv[l], state_delta[l],
                                           w, rel_bias[l], a_log[l], dt_bias[l], delta_norm_w[l])
        for lst, val in zip(outs, (k1, v1, c1, s1, k2, v2, c2, s2)):
            lst.append(val)
    return (yp, ys) + tuple(jnp.stack(o) for o in outs)
```

```python
import functools

import numpy as np
import jax
import jax.numpy as jnp
from jax import lax
from jax.experimental import pallas as pl
from jax.experimental.pallas import tpu as pltpu

F32 = jnp.float32
BF16 = jnp.bfloat16

D_MODEL = 1024
CHUNK = 64
RMS_EPS = 1e-6
L2_EPS = 1e-6
A_HEADS = 8
A_HEAD_DIM = 64
A_WIDTH = A_HEADS * A_HEAD_DIM
A_PREV_CHUNKS = 8
A_REACH = A_PREV_CHUNKS * CHUNK
MAX_REL = 128
B_HEADS = 4
B_DK = 128
B_DV = 128
B_KW = B_HEADS * B_DK
B_VW = B_HEADS * B_DV
B_CONV_CH = 2 * B_KW + B_VW
CONV_WIDTH = 4
D_FF = 4 * D_MODEL

LANES = 128
SUBLANES = 8
VMEM_LIMIT_BYTES = 56 * 1024 * 1024

Q_BLOCK = 256
N_KEY_BLOCKS = A_REACH // Q_BLOCK + 1
SCORE_LOOKAHEAD = 2
ATTN_Q_BLOCKS_PER_STEP = 4
ONES_ROWS = 16
LOG2_E = 1.4426950408889634
DELTA_CHUNK = 128
DELTA_BLOCK = 512
DELTA_STREAMS = 4
SAMPLE_DELTA_STREAMS = 8
CHAIN_STARTS_PER_ROUND = 2
ROW_TILE = 512
CONV_ROWS = 64
GATE_COLS = 512
MERGE_ROWS = 1024

assert B_DK == LANES and B_DV == LANES and B_HEADS % 2 == 0


def _dot(a, b):
    return jnp.dot(a, b, preferred_element_type=F32)


def _dot_nt(a, b):
    return lax.dot_general(a, b, (((1,), (1,)), ((), ())), preferred_element_type=F32)


def _sigmoid_of_twice(h):
    return 0.5 * jnp.tanh(h) + 0.5


def _sigmoid(x):
    return _sigmoid_of_twice(0.5 * x)


def _silu_of_twice(h):
    return h * jnp.tanh(h) + h


def _rms(x, gain):
    ms = jnp.mean(x * x, axis=-1, keepdims=True)
    return x * lax.rsqrt(ms + RMS_EPS) * gain


def _params(sem):
    return pltpu.CompilerParams(dimension_semantics=sem, vmem_limit_bytes=VMEM_LIMIT_BYTES)


def _const_spec(shape):
    nd = len(shape)
    return pl.BlockSpec(shape, lambda *_: (0,) * nd, pipeline_mode=pl.Buffered(1))


def _conv_silu(hist, x, w):
    n = x.shape[0]
    full = jnp.concatenate([hist, x], axis=0)
    rb = min(CONV_ROWS, n)
    half_w = 0.5 * w
    out = []
    for r0 in range(0, n, rb):
        piece = full[r0:r0 + rb + SUBLANES, :]
        z = piece * half_w[0:1, :]
        for i in range(1, CONV_WIDTH):
            z = pltpu.roll(z, 1, 0) + piece * half_w[i:i + 1, :]
        out.append(_silu_of_twice(z[SUBLANES:, :]))
    return out[0] if len(out) == 1 else jnp.concatenate(out, axis=0)


def _l2norm_heads(y, scale):
    heads = []
    for h in range(y.shape[1] // B_DK):
        yh = y[:, h * B_DK:(h + 1) * B_DK]
        heads.append(yh * (lax.rsqrt(jnp.sum(yh * yh, axis=-1, keepdims=True) + L2_EPS) * scale))
    return jnp.concatenate(heads, axis=1)


def _conv_silu_l2norm(hist, x, convw_ref):
    y = _conv_silu(hist, x, convw_ref[...])
    return (_l2norm_heads(y[:, :B_KW], B_DK ** -0.5), _l2norm_heads(y[:, B_KW:2 * B_KW], 1.0), y[:, 2 * B_KW:])


def _inproj_kernel(*refs, prompt):
    x_ref, gain_ref, wq_ref, wk_ref, wv_ref, wcv_ref, wzab_ref, wg_ref = refs[:8]
    if prompt:
        (convw_ref, q_ref, k_ref, vt_ref, dq_ref, dk_ref, dv_ref, z_ref, ab_ref, g_ref,
         kf_ref, vf_ref, cvt_ref, hist_ref) = refs[8:]
    else:
        q_ref, kf_ref, vf_ref, cv_ref, z_ref, ab_ref, g_ref = refs[8:]
    if prompt:
        @pl.when(pl.program_id(1) == 0)
        def _():
            hist_ref[...] = jnp.zeros((SUBLANES, B_CONV_CH), F32)

    x = x_ref[0]
    hb = _rms(x, gain_ref[...]).astype(BF16)
    tm = x.shape[0]

    stages = []

    def stage(issue):
        return lambda consume: stages.append((issue, consume))

    if prompt:
        norms = (B_DK ** -0.5, 1.0, None)
        for part, out_ref in enumerate((dq_ref, dk_ref, dv_ref)):
            cs = slice(part * B_KW, (part + 1) * B_KW)

            @stage(lambda cs=cs: _dot(hb, wcv_ref[:, cs]))
            def _(cv, cs=cs, out_ref=out_ref, scale=norms[part]):
                y = _conv_silu(hist_ref[:, cs], cv, convw_ref[:, cs])
                out_ref[0] = (y if scale is None else _l2norm_heads(y, scale)).astype(BF16)
                hist_ref[:, cs] = cv[tm - SUBLANES:, :]
                cvt_ref[0, :, cs] = cv[tm - SUBLANES:, :]
    else:
        @stage(lambda: _dot(hb, wcv_ref[...]))
        def _(cv):
            cv_ref[0] = cv

    def feature_major(val, out_ref):
        vt = val.T.astype(BF16)
        for j in range(tm // Q_BLOCK):
            out_ref[0, j] = vt[:, j * Q_BLOCK:(j + 1) * Q_BLOCK]

    @stage(lambda: _dot(hb, wq_ref[...]))
    def _(q):
        if prompt:
            feature_major(q * (A_HEAD_DIM ** -0.5 * LOG2_E), q_ref)
        else:
            q_ref[0] = (q * (A_HEAD_DIM ** -0.5)).astype(BF16)

    @stage(lambda: _dot(hb, wk_ref[...]))
    def _(k):
        kf_ref[0] = k
        if prompt:
            k_ref[0] = k.astype(BF16)

    @stage(lambda: _dot(hb, wv_ref[...]))
    def _(v):
        vf_ref[0] = v
        if prompt:
            feature_major(v, vt_ref)

    @stage(lambda: _dot(hb, wzab_ref[...]))
    def _(zab):
        z_ref[0] = zab[:, :B_VW].astype(BF16)
        ab_ref[0] = zab[:, B_VW:]

    for c0 in range(0, 2 * D_MODEL, GATE_COLS):
        cs = slice(c0, c0 + GATE_COLS)

        @stage(lambda cs=cs: _dot(hb, wg_ref[:, cs]))
        def _(g, cs=cs):
            g_ref[0, :, cs] = _sigmoid_of_twice(g).astype(BF16)

    pending = None
    for issue, consume in stages:
        val = issue()
        if pending is not None:
            pending[1](pending[0])
        pending = (val, consume)
    pending[1](pending[0])


def _inproj(x, gain, wq, wk, wv, wcv, wzab, wg, conv_w=None, *, prompt):
    b, t, d = x.shape
    tm = min(ROW_TILE, t)
    assert t % tm == 0
    row = lambda w, dt: (jax.ShapeDtypeStruct((b, t, w), dt), pl.BlockSpec((1, tm, w), lambda i, j: (i, j, 0)))
    stream = lambda r, w: (jax.ShapeDtypeStruct((b, r, w), F32), pl.BlockSpec((1, r, w), lambda i, j: (i, 0, 0)))
    if prompt:
        assert tm == A_REACH and tm % Q_BLOCK == 0
        nq = tm // Q_BLOCK
        feat = (jax.ShapeDtypeStruct((b, t // Q_BLOCK, A_WIDTH, Q_BLOCK), BF16),
                pl.BlockSpec((1, nq, A_WIDTH, Q_BLOCK), lambda i, j: (i, j, 0, 0)))
        outs = [feat, row(A_WIDTH, BF16), feat, row(B_KW, BF16), row(B_KW, BF16), row(B_VW, BF16)]
    else:
        outs = [row(A_WIDTH, BF16), row(A_WIDTH, F32), row(A_WIDTH, F32), row(B_CONV_CH, F32)]
    outs += [row(B_VW, BF16), row(LANES, F32), row(2 * D_MODEL, BF16)]
    weights = [wq, wk, wv, wcv, wzab, wg]
    scratch = []
    if prompt:
        outs += [stream(tm, A_WIDTH), stream(tm, A_WIDTH), stream(SUBLANES, B_CONV_CH)]
        weights.append(conv_w)
        scratch.append(pltpu.VMEM((SUBLANES, B_CONV_CH), F32))
    in_specs = [pl.BlockSpec((1, tm, d), lambda i, j: (i, j, 0)), _const_spec(gain.shape)]
    in_specs += [_const_spec(w.shape) for w in weights]
    return pl.pallas_call(
        functools.partial(_inproj_kernel, prompt=prompt),
        grid=(b, t // tm),
        in_specs=in_specs,
        out_specs=[o[1] for o in outs],
        out_shape=[o[0] for o in outs],
        scratch_shapes=scratch,
        compiler_params=_params(("arbitrary", "arbitrary")),
        name="inproj_prompt" if prompt else "inproj_sample",
    )(x, gain, *weights)


def _conv_sample_kernel(cv_ref, prev_ref, convw_ref, dq_ref, dk_ref, dv_ref):
    dq, dk, dv = _conv_silu_l2norm(prev_ref[0], cv_ref[0], convw_ref)
    dq_ref[0] = dq.astype(BF16)
    dk_ref[0] = dk.astype(BF16)
    dv_ref[0] = dv.astype(BF16)


def _conv_sample(cv, prev, conv_w):
    b, t, _ = cv.shape
    blk = lambda r, w: pl.BlockSpec((1, r, w), lambda i: (i, 0, 0))
    out = jax.ShapeDtypeStruct((b, t, B_KW), BF16)
    return pl.pallas_call(
        _conv_sample_kernel,
        grid=(b,),
        in_specs=[blk(t, B_CONV_CH), blk(SUBLANES, B_CONV_CH), _const_spec(conv_w.shape)],
        out_specs=[blk(t, B_KW)] * 3,
        out_shape=[out] * 3,
        compiler_params=_params(("arbitrary",)),
        name="conv_sample",
    )(cv, prev, conv_w)


def _toeplitz_bias(rel_bias, rows, width, offset, sign=1, scale=1.0):
    pitch = width + rows
    period = pitch + 1
    j = np.arange(period)
    rel = sign * np.where(j < width, offset - j, offset + period - j)
    idx = np.clip(rel, -MAX_REL, MAX_REL) + MAX_REL
    vec = rel_bias.astype(F32)[:, idx] * scale
    tiled = jnp.tile(vec, (1, rows))[:, :rows * pitch]
    return tiled.reshape(rel_bias.shape[0], rows, pitch)[:, :, :width]


def _prompt_bias_diagonals(rel_bias):
    j = np.arange(2 * Q_BLOCK)
    d = np.where(j < Q_BLOCK, j, j - 2 * Q_BLOCK)
    rel = d[None, :] + A_REACH - Q_BLOCK * np.arange(N_KEY_BLOCKS)[:, None]
    idx = np.clip(rel, -MAX_REL, MAX_REL) + MAX_REL
    return rel_bias.astype(F32)[:, idx] * LOG2_E


def _fill_prompt_bias(diag_ref, bias_ref):
    kc = lax.broadcasted_iota(jnp.int32, (Q_BLOCK, Q_BLOCK), 0) // CHUNK
    qc = lax.broadcasted_iota(jnp.int32, (Q_BLOCK, Q_BLOCK), 1) // CHUNK
    for jb in range(N_KEY_BLOCKS):
        kcw = kc + jb * (Q_BLOCK // CHUNK)
        band = (kcw >= qc) & (kcw <= qc + A_PREV_CHUNKS)
        for h in range(A_HEADS):
            rows = jnp.broadcast_to(diag_ref[h, jb:jb + 1, :], (Q_BLOCK, 2 * Q_BLOCK))
            rotated = pltpu.roll(rows, 0, 1, stride=1, stride_axis=0)
            bias_ref[h, jb] = jnp.where(band, rotated[:, :Q_BLOCK], -jnp.inf)
    for h in range(A_HEADS):
        bias_ref[h, N_KEY_BLOCKS] = jnp.full((Q_BLOCK, Q_BLOCK), -jnp.inf, F32)


def _softmax_pv(scores, values):
    m = None
    for s in scores:
        mj = jnp.max(s, axis=-1, keepdims=True)
        m = mj if m is None else jnp.maximum(m, mj)
    denom = None
    acc = None
    for s, vals in zip(scores, values):
        p = jnp.exp(s - m)
        dj = jnp.sum(p, axis=-1, keepdims=True)
        denom = dj if denom is None else denom + dj
        pv = _dot(p.astype(BF16), vals)
        acc = pv if acc is None else acc + pv
    return acc / denom


def _attn_prompt_kernel(qt_ref, k_ref, vt_ref, diag_ref, o_ref, bias_ref):
    @pl.when((pl.program_id(0) == 0) & (pl.program_id(1) == 0))
    def _():
        _fill_prompt_bias(diag_ref, bias_ref)

    n_sub = qt_ref.shape[1]
    row = lax.broadcasted_iota(jnp.int32, (LANES, Q_BLOCK), 0)
    low = row < A_HEAD_DIM
    kbs, bsels = {}, {}
    for u in range(n_sub):
        qi = pl.program_id(1) * n_sub + u
        for j in range(N_KEY_BLOCKS):
            kraw = qi - (N_KEY_BLOCKS - 1) + j
            kbs[u, j] = jnp.maximum(kraw, 0)
            bsels[u, j] = jnp.where(kraw >= 0, j, N_KEY_BLOCKS)
    n_qcols = Q_BLOCK // LANES

    def scores(u, h):
        cols = slice((h // 2) * LANES, (h // 2 + 1) * LANES)
        qp = qt_ref[0, u, cols, :]
        qm = jnp.where(low if h % 2 == 0 else jnp.logical_not(low), qp, jnp.zeros_like(qp))
        out = []
        for j in range(N_KEY_BLOCKS):
            kblk = k_ref[0, pl.ds(pl.multiple_of(kbs[u, j] * Q_BLOCK, Q_BLOCK), Q_BLOCK), cols]
            out.append(_dot(kblk, qm) + bias_ref[h, bsels[u, j]])
        return out

    def softmax_pv(u, h, s):
        cols = slice((h // 2) * LANES, (h // 2 + 1) * LANES)
        p_cols = []
        for qc in range(n_qcols):
            ls = slice(qc * LANES, (qc + 1) * LANES)
            lo, hi = qc * LANES, qc * LANES + A_REACH + LANES
            spans = [(max(lo - j * Q_BLOCK, 0), min(hi - j * Q_BLOCK, Q_BLOCK)) for j in range(N_KEY_BLOCKS)]
            parts = [s[j][a:b, ls] for j, (a, b) in enumerate(spans)]
            m = functools.reduce(jnp.maximum, [jnp.max(x, axis=0, keepdims=True) for x in parts])
            ps = [jnp.exp2(x - m) for x in parts]
            padded = []
            for x, (a, b) in zip(ps, spans):
                pieces = ([jnp.zeros((a, LANES), F32)] if a else []) + [x]
                pieces += [jnp.zeros((Q_BLOCK - b, LANES), F32)] if b < Q_BLOCK else []
                padded.append(pieces[0] if len(pieces) == 1 else jnp.concatenate(pieces, axis=0))
            p_cols.append(padded)
        ones = jnp.ones((ONES_ROWS, Q_BLOCK), BF16)
        acc = None
        for j in range(N_KEY_BLOCKS):
            pt = jnp.concatenate([p_cols[qc][j] for qc in range(n_qcols)], axis=1).astype(BF16)
            pv = _dot(jnp.concatenate([vt_ref[0, kbs[u, j], cols, :], ones], axis=0), pt)
            acc = pv if acc is None else acc + pv
        return acc[:LANES] * (1.0 / acc[LANES:LANES + 1])

    tasks = [(u, h) for u in range(n_sub) for h in range(A_HEADS)]
    ready = [scores(*t) for t in tasks[:SCORE_LOOKAHEAD]]
    pair_out = []
    for i, (u, h) in enumerate(tasks):
        if i + SCORE_LOOKAHEAD < len(tasks):
            ready.append(scores(*tasks[i + SCORE_LOOKAHEAD]))
        pair_out.append(softmax_pv(u, h, ready.pop(0)))
        if h % 2 == 1:
            cols = slice((h // 2) * LANES, (h // 2 + 1) * LANES)
            o_pair = jnp.concatenate([pair_out[0][:A_HEAD_DIM], pair_out[1][A_HEAD_DIM:]], axis=0)
            o_ref[0, u * Q_BLOCK:(u + 1) * Q_BLOCK, cols] = o_pair.T.astype(BF16)
            pair_out = []


def _attn_prompt(qt, k, vt, diag):
    b, t, _ = k.shape
    n_sub = ATTN_Q_BLOCKS_PER_STEP
    assert t % (n_sub * Q_BLOCK) == 0
    return pl.pallas_call(
        _attn_prompt_kernel,
        grid=(b, t // (n_sub * Q_BLOCK)),
        in_specs=[
            pl.BlockSpec((1, n_sub, A_WIDTH, Q_BLOCK), lambda i, j: (i, j, 0, 0)),
            pl.BlockSpec((1, t, A_WIDTH), lambda i, j: (i, 0, 0)),
            pl.BlockSpec((1, t // Q_BLOCK, A_WIDTH, Q_BLOCK), lambda i, j: (i, 0, 0, 0)),
            _const_spec(diag.shape),
        ],
        out_specs=pl.BlockSpec((1, n_sub * Q_BLOCK, A_WIDTH), lambda i, j: (i, j, 0)),
        out_shape=jax.ShapeDtypeStruct((b, t, A_WIDTH), BF16),
        scratch_shapes=[pltpu.VMEM((A_HEADS, N_KEY_BLOCKS + 1, Q_BLOCK, Q_BLOCK), F32)],
        compiler_params=_params(("arbitrary", "arbitrary")),
        name="attn_prompt",
    )(qt, k, vt, diag)


def _attn_sample_kernel(q_ref, ck_ref, cv_ref, nk_ref, nv_ref, bc_ref, bn_ref, o_ref):
    rows = q_ref.shape[1]
    lane = lax.broadcasted_iota(jnp.int32, (rows, LANES), 1)
    low = lane < A_HEAD_DIM
    pair_cols = [slice(hp * LANES, (hp + 1) * LANES) for hp in range(A_HEADS // 2)]
    scores = []
    for h in range(A_HEADS):
        cols = pair_cols[h // 2]
        qp = q_ref[0, :, cols]
        qm = jnp.where(low if h % 2 == 0 else jnp.logical_not(low), qp, jnp.zeros_like(qp))
        scores.append([_dot_nt(qm, ck_ref[0, :, cols].astype(BF16)) + bc_ref[h],
                       _dot_nt(qm, nk_ref[0, :, cols].astype(BF16)) + bn_ref[h]])
    outs = [_softmax_pv(scores[h], [cv_ref[0, :, pair_cols[h // 2]].astype(BF16),
                                    nv_ref[0, :, pair_cols[h // 2]].astype(BF16)]) for h in range(A_HEADS)]
    for hp, cols in enumerate(pair_cols):
        o_ref[0, :, cols] = jnp.where(low, outs[2 * hp], outs[2 * hp + 1]).astype(BF16)


def _attn_sample(q, cache_k, cache_v, new_k, new_v, rel_bias):
    b, t, _ = q.shape
    c = cache_k.shape[1]
    bias = _toeplitz_bias(rel_bias, t, c + LANES, c)
    bias = jnp.where((np.arange(c + LANES) < c + t)[None, None, :], bias, -jnp.inf)
    bias_c, bias_n = bias[:, :, :c], bias[:, :, c:]
    pad = ((0, 0), (0, LANES - t), (0, 0))
    nk, nv = jnp.pad(new_k, pad), jnp.pad(new_v, pad)
    blk = lambda r: pl.BlockSpec((1, r, A_WIDTH), lambda i: (i, 0, 0))
    return pl.pallas_call(
        _attn_sample_kernel,
        grid=(b,),
        in_specs=[blk(t), blk(c), blk(c), blk(LANES), blk(LANES),
                  _const_spec(bias_c.shape), _const_spec(bias_n.shape)],
        out_specs=blk(t),
        out_shape=jax.ShapeDtypeStruct((b, t, A_WIDTH), BF16),
        compiler_params=_params(("arbitrary",)),
        name="attn_sample",
    )(q, cache_k, cache_v, nk, nv, bias_c, bias_n)


def _block_diag(x2):
    half = x2.shape[1] // 2
    z = jnp.zeros_like(x2[:, :half])
    top = jnp.concatenate([x2[:, :half], z], axis=1)
    bot = jnp.concatenate([z, x2[:, half:]], axis=1)
    return jnp.concatenate([top, bot], axis=0)


def _quad_diag(x4):
    n = x4.shape[0]
    group = lax.broadcasted_iota(jnp.int32, x4.shape, 1) // n
    zero = jnp.zeros_like(x4)
    return jnp.concatenate([jnp.where(group == g, x4, zero) for g in range(4)], axis=0)


def _unit_lower_inverse(m, eye_half):
    L = m.shape[0]
    h = L // 2
    lane = lax.broadcasted_iota(jnp.int32, (h, 2 * L), 1)
    first_half = (lane & (L - 1)) < h
    left_head = lane < L
    diag = jnp.where(first_half, m[:h], m[h:])
    b16 = jnp.where(first_half, m[h:], 0.0).astype(BF16)
    t = eye_half + diag
    d16 = diag.astype(BF16)
    p = _dot(d16, _quad_diag(d16))
    yield
    n_levels = h.bit_length() - 2
    for lvl in range(n_levels):
        p16 = p.astype(BF16)
        t16 = t.astype(BF16)
        if lvl == n_levels - 1:
            t = t + _dot(t16, _quad_diag(p16))
        else:
            res = _dot(jnp.concatenate([t16, p16], axis=0), _quad_diag(p16))
            t = t + res[:h]
            p = res[h:]
        yield
    t16 = t.astype(BF16)
    zero16 = jnp.zeros_like(t16)
    ta_only = jnp.where(first_half, t16, zero16)
    w_a = jnp.concatenate([jnp.where(left_head, ta_only, zero16), zero16,
                           jnp.where(left_head, zero16, ta_only), zero16], axis=0)
    y16 = _dot(b16, w_a).astype(BF16)
    yield
    w_y = jnp.concatenate([zero16, jnp.where(left_head, y16, zero16),
                           zero16, jnp.where(left_head, zero16, y16)], axis=0)
    t_b = _dot(t16, w_y)
    return jnp.where(first_half, t, 0.0), jnp.where(first_half, t_b, t)


def _round_robin(generators):
    results = [None] * len(generators)
    active = list(enumerate(generators))
    while active:
        still = []
        for i, g in active:
            try:
                next(g)
                still.append((i, g))
            except StopIteration as stop:
                results[i] = stop.value
        active = still
    return results


def _delta_kernel(q_ref, k_ref, v_ref, z_ref, ab_ref, s0_ref, alog_ref, dtb_ref, nw_ref,
                  o_ref, s_ref, rows_ref, *, n_valid):
    L = DELTA_CHUNK
    ns, tb = q_ref.shape[0], q_ref.shape[1]
    nch = tb // L
    n_pairs = B_HEADS // 2

    @pl.when(pl.program_id(1) == 0)
    def _():
        s_ref[...] = s0_ref[...]

    chains = [(s, c, p) for s in range(ns) for c in range(nch) for p in range(n_pairs)]

    st = {}
    for s, c, p in chains:
        r = slice(c * L, (c + 1) * L)
        ls = slice(p * 2 * L, (p + 1) * 2 * L)
        q2 = q_ref[s, r, ls]
        k2 = k_ref[s, r, ls]
        sc = _dot_nt(jnp.concatenate([q2, k2], axis=0), _block_diag(k2))
        st[s, c, p] = dict(q2=q2.astype(F32), k2=k2.astype(F32), qk=sc[:L], kk=sc[L:])

    scal = _round_robin([_delta_scalars(ab_ref, alog_ref, dtb_ref, rows_ref, s, n_valid) for s in range(ns)])
    _delta_chunks(st, chains, scal, v_ref, z_ref, nw_ref, o_ref, s_ref)


def _delta_scalars(ab_ref, alog_ref, dtb_ref, rows_ref, s, n_valid):
    L = DELTA_CHUNK
    tb = ab_ref.shape[1]
    nch = tb // L
    ab_t = ab_ref[s].T
    rows8 = ab_t[0:2 * B_HEADS, :]
    x = rows8 + dtb_ref[:, 0:1]
    softplus = jnp.maximum(x, 0.0) + jnp.log(1.0 + jnp.exp(-jnp.abs(x)))
    g8 = -jnp.exp(alog_ref[:, 0:1]) * softplus
    beta8 = _sigmoid(rows8)
    tok = lax.broadcasted_iota(jnp.int32, (2 * B_HEADS, tb), 1)
    if n_valid < tb:
        g8 = jnp.where(tok < n_valid, g8, 0.0)
        beta8 = jnp.where(tok < n_valid, beta8, 0.0)
    pos = tok & (L - 1)
    gc8 = g8
    shift = 1
    while shift < L:
        yield
        gc8 = gc8 + jnp.where(pos >= shift, pltpu.roll(gc8, shift, 1), 0.0)
        shift *= 2
    gl8 = [gc8[:, (c + 1) * L - 1:(c + 1) * L] for c in range(nch)]
    glb8 = jnp.concatenate([jnp.broadcast_to(g, (2 * B_HEADS, L)) for g in gl8], axis=1)
    rows_ref[s] = jnp.zeros(rows_ref.shape[1:], F32)
    rows_ref[s, 0:8, :] = gc8
    rows_ref[s, 8:16, :] = beta8
    rows_ref[s, 16:24, :] = jnp.exp(gc8)
    rows_ref[s, 24:32, :] = jnp.exp(glb8 - gc8)
    cols = rows_ref[s].T
    return gc8, gl8, cols


def _delta_chunks(st, chains, scal, v_ref, z_ref, nw_ref, o_ref, s_ref):
    L = DELTA_CHUNK
    ns, tb = v_ref.shape[0], v_ref.shape[1]
    nch = tb // L
    n_pairs = B_HEADS // 2
    ri = lax.broadcasted_iota(jnp.int32, (L, 2 * L), 0)
    ci = lax.broadcasted_iota(jnp.int32, (L, 2 * L), 1) & (L - 1)
    tril = ri >= ci
    strict = ri > ci
    rh = lax.broadcasted_iota(jnp.int32, (L // 2, 2 * L), 0)
    ch = lax.broadcasted_iota(jnp.int32, (L // 2, 2 * L), 1) & (L // 2 - 1)
    eye_half = (rh == ch).astype(F32)

    def chain_ops(s, c, p):
        gc8, _, cols = scal[s]
        r = slice(c * L, (c + 1) * L)
        ls = slice(p * 2 * L, (p + 1) * 2 * L)
        h0 = 2 * p
        d = st[s, c, p]
        pair_cols = lambda i: [jnp.broadcast_to(cols[r, i + e:i + e + 1], (L, L)) for e in range(2)]
        gc_c = jnp.concatenate(pair_cols(h0), axis=1)
        beta = pair_cols(8 + B_HEADS + h0)
        egc = pair_cols(16 + h0)
        edec = pair_cols(24 + h0)
        gc_r = jnp.concatenate([jnp.broadcast_to(gc8[h0 + e:h0 + e + 1, r], (L, L)) for e in range(2)], axis=1)
        decay = jnp.exp(jnp.where(tril, gc_c - gc_r, -jnp.inf))
        m = jnp.where(strict, -(jnp.concatenate(beta, axis=1) * d.pop("kk") * decay), 0.0)
        qkd16 = (d.pop("qk") * decay).astype(BF16)
        v2 = v_ref[s, r, ls].astype(F32)
        yield
        top, bot = yield from _unit_lower_inverse(m, eye_half)
        yield
        t16 = jnp.concatenate([top, bot], axis=0).astype(BF16)
        sols = []
        for e in range(2):
            hs = slice(e * L, (e + 1) * L)
            kh = d["k2"][:, hs]
            rhs = jnp.concatenate([v2[:, hs] * beta[e], kh * (beta[e] * egc[e])], axis=1)
            sols.append(_dot(t16[:, hs], rhs.astype(BF16)).astype(BF16))
        yield
        phi, psi, qt, oin = [], [], [], []
        for e in range(2):
            hs = slice(e * L, (e + 1) * L)
            kdt = (d["k2"][:, hs] * edec[e]).T.astype(BF16)
            r2 = _dot(jnp.concatenate([qkd16[:, hs], kdt], axis=0), sols[e])
            oin.append(r2[:L, :B_DV])
            qt.append(d["q2"][:, hs] * egc[e] - r2[:L, B_DV:])
            psi.append(r2[L:, :B_DV])
            phi.append(-r2[L:, B_DV:])
        d["lhs"] = jnp.concatenate([jnp.concatenate(phi, axis=1), jnp.concatenate(qt, axis=1)], axis=0).astype(BF16)
        d["psi"] = jnp.concatenate(psi, axis=1)
        d["oin"] = jnp.concatenate(oin, axis=1)

    lanes = [slice(p * 2 * L, (p + 1) * 2 * L) for p in range(n_pairs)]
    carried = [(s, p) for s in range(ns) for p in range(n_pairs)]
    s2 = {(s, p): s_ref[s, :, lanes[p]] for s, p in carried}
    nw = nw_ref[...]
    pending = []

    def emit(c, o2):
        r = slice(c * L, (c + 1) * L)
        for (s, p), o in o2.items():
            for e in range(2):
                hs = slice((2 * p + e) * B_DV, (2 * p + e + 1) * B_DV)
                zh = z_ref[s, r, hs].astype(F32)
                o_ref[s, r, hs] = (_rms(o[:, e * B_DV:(e + 1) * B_DV], nw) * _silu_of_twice(0.5 * zh)).astype(BF16)

    def carry_state(c):
        o2 = {}
        for s, p in carried:
            d = st[s, c, p]
            gl = scal[s][1][c]
            res = _dot(d["lhs"], _block_diag(s2[s, p].astype(BF16)))
            o2[s, p] = res[L:] + d["oin"]
            egl = jnp.concatenate([jnp.broadcast_to(jnp.exp(gl[2 * p + e:2 * p + e + 1, :]), (1, B_DV))
                                   for e in range(2)], axis=1)
            s2[s, p] = s2[s, p] * egl + res[:L] + d["psi"]
        if pending:
            emit(*pending.pop())
        pending.append((c, o2))

    waiting = [(key, chain_ops(*key)) for key in sorted(chains, key=lambda k: (k[1], k[0], k[2]))]
    live, finished, next_chunk = [], [0] * nch, 0
    while waiting or live:
        live += waiting[:CHAIN_STARTS_PER_ROUND]
        waiting = waiting[CHAIN_STARTS_PER_ROUND:]
        still = []
        for key, g in live:
            try:
                next(g)
                still.append((key, g))
            except StopIteration:
                finished[key[1]] += 1
        live = still
        while next_chunk < nch and finished[next_chunk] == len(carried):
            carry_state(next_chunk)
            next_chunk += 1
    emit(*pending.pop())
    for s, p in carried:
        s_ref[s, :, lanes[p]] = s2[s, p]


def _delta(q, k, v, z, ab, s0, a_log, dt_bias, norm_w, *, block, streams, n_valid):
    b, t, _ = q.shape
    streams = min(streams, b)
    assert t % block == 0 and block % DELTA_CHUNK == 0 and b % streams == 0
    alog8 = jnp.broadcast_to(jnp.tile(a_log.astype(F32), 2)[:, None], (2 * B_HEADS, LANES))
    dtb8 = jnp.broadcast_to(jnp.tile(dt_bias.astype(F32), 2)[:, None], (2 * B_HEADS, LANES))
    row = lambda w: pl.BlockSpec((streams, block, w), lambda i, j: (i, j, 0))
    state = pl.BlockSpec((streams, B_DK, B_HEADS * B_DV), lambda i, j: (i, 0, 0))
    s0 = s0.astype(F32).transpose(0, 2, 1, 3).reshape(b, B_DK, B_HEADS * B_DV)
    o, s = pl.pallas_call(
        functools.partial(_delta_kernel, n_valid=n_valid),
        grid=(b // streams, t // block),
        in_specs=[row(B_KW), row(B_KW), row(B_VW), row(B_VW), row(LANES), state,
                  _const_spec(alog8.shape), _const_spec(dtb8.shape), _const_spec((1, B_DV))],
        out_specs=[row(B_VW), state],
        out_shape=[jax.ShapeDtypeStruct((b, t, B_VW), BF16),
                   jax.ShapeDtypeStruct((b, B_DK, B_HEADS * B_DV), F32)],
        scratch_shapes=[pltpu.VMEM((streams, LANES, block), F32)],
        compiler_params=_params(("arbitrary", "arbitrary")),
        name="delta_%d" % block,
    )(q, k, v, z, ab, s0, alog8, dtb8, norm_w.reshape(1, B_DV).astype(F32))
    return o, s.reshape(b, B_DK, B_HEADS, B_DV).transpose(0, 2, 1, 3)


def _merge_kernel(x_ref, oa_ref, ob_ref, g_ref, wpa_ref, wpb_ref, wout_ref, gain_ref, o_ref):
    tm = x_ref.shape[0]
    sub = min(ROW_TILE, tm)
    blocks = [slice(r, r + sub) for r in range(0, tm, sub)]

    def branches(rs):
        return _dot(oa_ref[rs, :], wpa_ref[...]), _dot(ob_ref[rs, :], wpb_ref[...])

    nxt = branches(blocks[0])
    for i, rs in enumerate(blocks):
        ya, yb = nxt
        if i + 1 < len(blocks):
            nxt = branches(blocks[i + 1])
        merged = g_ref[rs, :D_MODEL].astype(F32) * ya + g_ref[rs, D_MODEL:].astype(F32) * yb
        y = _dot(merged.astype(BF16), wout_ref[...])
        o_ref[rs, :] = x_ref[rs, :] + _rms(y, gain_ref[...])


def _merge(x, oa, ob, gates, wpa, wpb, wout, gain):
    n, d = x.shape
    tm = min(MERGE_ROWS, n)
    row = lambda w: pl.BlockSpec((tm, w), lambda i: (i, 0))
    return pl.pallas_call(
        _merge_kernel,
        grid=(n // tm,),
        in_specs=[row(d), row(A_WIDTH), row(B_VW), row(2 * d),
                  _const_spec(wpa.shape), _const_spec(wpb.shape), _const_spec(wout.shape), _const_spec(gain.shape)],
        out_specs=row(d),
        out_shape=jax.ShapeDtypeStruct((n, d), F32),
        compiler_params=_params(("arbitrary",)),
        name="merge_%d" % n,
    )(x, oa, ob, gates, wpa, wpb, wout, gain)


FF_SPLIT = 4


def _mlp_kernel(x_ref, gpre_ref, wup_ref, wdown_ref, gpost_ref, o_ref):
    x = x_ref[...]
    hb = _rms(x, gpre_ref[...]).astype(BF16)
    wcols = D_FF // FF_SPLIT
    f = None
    for c in range(FF_SPLIT):
        cs = slice(c * wcols, (c + 1) * wcols)
        up = jnp.maximum(_dot(hb, wup_ref[:, cs]), 0.0)
        part = _dot((up * up).astype(BF16), wdown_ref[cs, :])
        f = part if f is None else f + part
    o_ref[...] = x + _rms(f, gpost_ref[...])


def _mlp(x, gpre, wup, wdown, gpost):
    n, d = x.shape
    tm = min(ROW_TILE, n)
    row = pl.BlockSpec((tm, d), lambda i: (i, 0))
    return pl.pallas_call(
        _mlp_kernel,
        grid=(n // tm,),
        in_specs=[row, _const_spec(gpre.shape), _const_spec(wup.shape), _const_spec(wdown.shape),
                  _const_spec(gpost.shape)],
        out_specs=row,
        out_shape=jax.ShapeDtypeStruct((n, d), F32),
        compiler_params=_params(("arbitrary",)),
        name="mlp_%d" % n,
    )(x, gpre, wup, wdown, gpost)


def _layer_weights(norm_mix_pre, w_in, conv_w, w_proj_a, w_proj_b, w_out, norm_mix_post,
                   norm_mlp_pre, w_up, w_down, norm_mlp_post):
    o_cv = 3 * A_WIDTH
    o_z = o_cv + B_CONV_CH
    o_ab = o_z + B_VW
    o_g = o_ab + 2 * B_HEADS
    wb = w_in.astype(BF16)
    wzab = jnp.concatenate([wb[:, o_z:o_g], jnp.zeros((D_MODEL, LANES - 2 * B_HEADS), BF16)], axis=1)
    vec = lambda g: g.reshape(1, -1).astype(F32)
    return dict(
        gain_in=vec(norm_mix_pre),
        wq=wb[:, :A_WIDTH], wk=wb[:, A_WIDTH:2 * A_WIDTH], wv=wb[:, 2 * A_WIDTH:o_cv],
        wcv=wb[:, o_cv:o_z], wzab=wzab, wg=wb[:, o_g:] * 0.5,
        conv_w=conv_w.astype(F32),
        wpa=w_proj_a.astype(BF16), wpb=w_proj_b.astype(BF16), wout=w_out.astype(BF16),
        gain_mix=vec(norm_mix_post), gain_pre=vec(norm_mlp_pre), gain_post=vec(norm_mlp_post),
        wup=w_up.astype(BF16), wdown=w_down.astype(BF16),
    )


def _tail(x, w, oa, ob, gates):
    b, t, d = x.shape
    x1 = _merge(x.reshape(b * t, d), oa.reshape(b * t, A_WIDTH), ob.reshape(b * t, B_VW),
                gates.reshape(b * t, 2 * d), w["wpa"], w["wpb"], w["wout"], w["gain_mix"])
    return _mlp(x1, w["gain_pre"], w["wup"], w["wdown"], w["gain_post"]).reshape(b, t, d)


def _prompt_layer(x, w, rel_bias, a_log, dt_bias, delta_norm_w):
    b, t, _ = x.shape
    qt, k, vt, dq, dk, dv, z, ab, gates, kf, vf, cvt = _inproj(
        x, w["gain_in"], w["wq"], w["wk"], w["wv"], w["wcv"], w["wzab"], w["wg"], w["conv_w"], prompt=True)
    oa = _attn_prompt(qt, k, vt, _prompt_bias_diagonals(rel_bias))
    s0 = jnp.zeros((b, B_HEADS, B_DK, B_DV), F32)
    ob, s_new = _delta(dq, dk, dv, z, ab, s0, a_log, dt_bias, delta_norm_w,
                       block=DELTA_BLOCK, streams=DELTA_STREAMS, n_valid=DELTA_BLOCK)
    y = _tail(x, w, oa, ob, gates)
    keep = min(A_REACH, t)
    new_k = kf.reshape(b, keep, A_HEADS, A_HEAD_DIM)
    new_v = vf.reshape(b, keep, A_HEADS, A_HEAD_DIM)
    new_conv = cvt[:, SUBLANES - (CONV_WIDTH - 1):, :]
    return y, new_k, new_v, new_conv, s_new


def _sample_layer(x, cache_k, cache_v, conv_state, delta_state, w, rel_bias, a_log, dt_bias, delta_norm_w):
    b, t, d = x.shape
    c = cache_k.shape[1]
    q, k, v, cv, z, ab, gates = _inproj(
        x.reshape(1, b * t, d), w["gain_in"], w["wq"], w["wk"], w["wv"], w["wcv"], w["wzab"], w["wg"],
        prompt=False)
    per_stream = lambda a: a.reshape(b, t, a.shape[-1])
    q, k, v, cv, z, ab, gates = map(per_stream, (q, k, v, cv, z, ab, gates))
    oa = _attn_sample(q, cache_k.reshape(b, c, A_WIDTH), cache_v.reshape(b, c, A_WIDTH), k, v, rel_bias)
    prev = jnp.pad(conv_state.astype(F32), ((0, 0), (SUBLANES - (CONV_WIDTH - 1), 0), (0, 0)))
    dq, dk, dv = _conv_sample(cv, prev, w["conv_w"])
    pad = lambda a: jnp.pad(a, ((0, 0), (0, DELTA_CHUNK - t), (0, 0)))
    ob, s_new = _delta(pad(dq), pad(dk), pad(dv), pad(z), pad(ab), delta_state, a_log, dt_bias, delta_norm_w,
                       block=DELTA_CHUNK, streams=SAMPLE_DELTA_STREAMS, n_valid=t)
    y = _tail(x, w, oa, ob[:, :t], gates)
    new_conv = jnp.concatenate([conv_state.astype(F32), cv], axis=1)[:, -(CONV_WIDTH - 1):]
    return (y, k.reshape(b, t, A_HEADS, A_HEAD_DIM), v.reshape(b, t, A_HEADS, A_HEAD_DIM), new_conv, s_new)


def kernel(x_prompt, x_sample, cache_attn_k, cache_attn_v, state_conv, state_delta, norm_mix_pre, w_in,
           rel_bias, conv_w, a_log, dt_bias, delta_norm_w, w_proj_a, w_proj_b, w_out, norm_mix_post,
           norm_mlp_pre, w_up, w_down, norm_mlp_post):
    depth = w_in.shape[0]
    yp, ys = x_prompt, x_sample
    outs = [[] for _ in range(8)]
    for l in range(depth):
        w = _layer_weights(norm_mix_pre[l], w_in[l], conv_w[l], w_proj_a[l], w_proj_b[l], w_out[l],
                           norm_mix_post[l], norm_mlp_pre[l], w_up[l], w_down[l], norm_mlp_post[l])
        yp, k1, v1, c1, s1 = _prompt_layer(yp, w, rel_bias[l], a_log[l], dt_bias[l], delta_norm_w[l])
        ys, k2, v2, c2, s2 = _sample_layer(ys, cache_attn_k[l], cache_attn_v[l], state_conv[l], state_delta[l],
                                           w, rel_bias[l], a_log[l], dt_bias[l], delta_norm_w[l])
        for lst, val in zip(outs, (k1, v1, c1, s1, k2, v2, c2, s2)):
            lst.append(val)
    return (yp, ys) + tuple(jnp.stack(o) for o in outs)
```

```python
import functools

import numpy as np
import jax
import jax.numpy as jnp
from jax import lax
from jax.experimental import pallas as pl
from jax.experimental.pallas import tpu as pltpu

F32 = jnp.float32
BF16 = jnp.bfloat16

D_MODEL = 1024
CHUNK = 64
RMS_EPS = 1e-6
L2_EPS = 1e-6
A_HEADS = 8
A_HEAD_DIM = 64
A_WIDTH = A_HEADS * A_HEAD_DIM
A_PREV_CHUNKS = 8
A_REACH = A_PREV_CHUNKS * CHUNK
MAX_REL = 128
B_HEADS = 4
B_DK = 128
B_DV = 128
B_KW = B_HEADS * B_DK
B_VW = B_HEADS * B_DV
B_CONV_CH = 2 * B_KW + B_VW
CONV_WIDTH = 4
D_FF = 4 * D_MODEL

LANES = 128
SUBLANES = 8
VMEM_LIMIT_BYTES = 56 * 1024 * 1024

Q_BLOCK = 256
N_KEY_BLOCKS = A_REACH // Q_BLOCK + 1
SCORE_LOOKAHEAD = 2
ATTN_Q_BLOCKS_PER_STEP = 8
ONES_ROWS = 16
LOG2_E = 1.4426950408889634
DELTA_CHUNK = 128
DELTA_BLOCK = 512
DELTA_STREAMS = 4
SAMPLE_DELTA_STREAMS = 8
SAMPLE_ATTN_STREAMS = 4
CHAIN_STARTS_PER_ROUND = 2
ROW_TILE = 512
CONV_ROWS = 512
GATE_COLS = 512
MERGE_ROWS = 1024
MLP_ROWS = 1024

assert B_DK == LANES and B_DV == LANES and B_HEADS % 2 == 0


def _dot(a, b):
    return jnp.dot(a, b, preferred_element_type=F32)


def _dot_nt(a, b):
    return lax.dot_general(a, b, (((1,), (1,)), ((), ())), preferred_element_type=F32)


def _sigmoid_of_twice(h):
    return 0.5 * jnp.tanh(h) + 0.5


def _sigmoid(x):
    return _sigmoid_of_twice(0.5 * x)


def _silu_of_twice(h):
    return h * jnp.tanh(h) + h


def _rms(x, gain):
    ms = jnp.mean(x * x, axis=-1, keepdims=True)
    return x * lax.rsqrt(ms + RMS_EPS) * gain


def _params(sem):
    return pltpu.CompilerParams(dimension_semantics=sem, vmem_limit_bytes=VMEM_LIMIT_BYTES)


def _const_spec(shape):
    nd = len(shape)
    return pl.BlockSpec(shape, lambda *_: (0,) * nd, pipeline_mode=pl.Buffered(1))


def _conv_silu(hist, x, w):
    n = x.shape[0]
    full = jnp.concatenate([hist, x], axis=0)
    rb = min(CONV_ROWS, n)
    half_w = 0.5 * w
    out = []
    for r0 in range(0, n, rb):
        piece = full[r0:r0 + rb + SUBLANES, :]
        z = piece * half_w[0:1, :]
        for i in range(1, CONV_WIDTH):
            z = pltpu.roll(z, 1, 0) + piece * half_w[i:i + 1, :]
        out.append(_silu_of_twice(z[SUBLANES:, :]))
    return out[0] if len(out) == 1 else jnp.concatenate(out, axis=0)


def _l2norm_heads(y, scale):
    heads = []
    for h in range(y.shape[1] // B_DK):
        yh = y[:, h * B_DK:(h + 1) * B_DK]
        heads.append(yh * (lax.rsqrt(jnp.sum(yh * yh, axis=-1, keepdims=True) + L2_EPS) * scale))
    return jnp.concatenate(heads, axis=1)


def _conv_silu_l2norm(hist, x, convw_ref):
    y = _conv_silu(hist, x, convw_ref[...])
    return (_l2norm_heads(y[:, :B_KW], B_DK ** -0.5), _l2norm_heads(y[:, B_KW:2 * B_KW], 1.0), y[:, 2 * B_KW:])


def _inproj_kernel(*refs, prompt):
    x_ref, gain_ref, wq_ref, wk_ref, wv_ref, wcv_ref, wzab_ref, wg_ref = refs[:8]
    if prompt:
        (convw_ref, q_ref, k_ref, vt_ref, dq_ref, dk_ref, dv_ref, z_ref, ab_ref, g_ref,
         kf_ref, vf_ref, cvt_ref, hist_ref) = refs[8:]
    else:
        q_ref, kf_ref, vf_ref, cv_ref, z_ref, ab_ref, g_ref = refs[8:]
    if prompt:
        @pl.when(pl.program_id(1) == 0)
        def _():
            hist_ref[...] = jnp.zeros((SUBLANES, B_CONV_CH), F32)

    x = x_ref[0]
    hb = _rms(x, gain_ref[...]).astype(BF16)
    tm = x.shape[0]

    stages = []

    def stage(issue):
        return lambda consume: stages.append((issue, consume))

    if prompt:
        norms = (B_DK ** -0.5, 1.0, None)
        for part, out_ref in enumerate((dq_ref, dk_ref, dv_ref)):
            cs = slice(part * B_KW, (part + 1) * B_KW)

            @stage(lambda cs=cs: _dot(hb, wcv_ref[:, cs]))
            def _(cv, cs=cs, out_ref=out_ref, scale=norms[part]):
                y = _conv_silu(hist_ref[:, cs], cv, convw_ref[:, cs])
                out_ref[0] = (y if scale is None else _l2norm_heads(y, scale)).astype(BF16)
                hist_ref[:, cs] = cv[tm - SUBLANES:, :]
                cvt_ref[0, :, cs] = cv[tm - SUBLANES:, :]
    else:
        @stage(lambda: _dot(hb, wcv_ref[...]))
        def _(cv):
            cv_ref[0] = cv

    def feature_major(val, out_ref):
        vt = val.T.astype(BF16)
        for j in range(tm // Q_BLOCK):
            out_ref[0, j] = vt[:, j * Q_BLOCK:(j + 1) * Q_BLOCK]

    @stage(lambda: _dot(hb, wq_ref[...]))
    def _(q):
        if prompt:
            feature_major(q * (A_HEAD_DIM ** -0.5 * LOG2_E), q_ref)
        else:
            q_ref[0] = (q * (A_HEAD_DIM ** -0.5)).astype(BF16)

    @stage(lambda: _dot(hb, wk_ref[...]))
    def _(k):
        kf_ref[0] = k
        if prompt:
            k_ref[0] = k.astype(BF16)

    @stage(lambda: _dot(hb, wv_ref[...]))
    def _(v):
        vf_ref[0] = v
        if prompt:
            feature_major(v, vt_ref)

    @stage(lambda: _dot(hb, wzab_ref[...]))
    def _(zab):
        z_ref[0] = zab[:, :B_VW].astype(BF16)
        ab_ref[0] = zab[:, B_VW:]

    for c0 in range(0, 2 * D_MODEL, GATE_COLS):
        cs = slice(c0, c0 + GATE_COLS)

        @stage(lambda cs=cs: _dot(hb, wg_ref[:, cs]))
        def _(g, cs=cs):
            g_ref[0, :, cs] = _sigmoid_of_twice(g).astype(BF16)

    pending = None
    for issue, consume in stages:
        val = issue()
        if pending is not None:
            pending[1](pending[0])
        pending = (val, consume)
    pending[1](pending[0])


def _inproj(x, gain, wq, wk, wv, wcv, wzab, wg, conv_w=None, *, prompt):
    b, t, d = x.shape
    tm = min(ROW_TILE, t)
    assert t % tm == 0
    row = lambda w, dt: (jax.ShapeDtypeStruct((b, t, w), dt), pl.BlockSpec((1, tm, w), lambda i, j: (i, j, 0)))
    stream = lambda r, w: (jax.ShapeDtypeStruct((b, r, w), F32), pl.BlockSpec((1, r, w), lambda i, j: (i, 0, 0)))
    if prompt:
        assert tm == A_REACH and tm % Q_BLOCK == 0
        nq = tm // Q_BLOCK
        feat = (jax.ShapeDtypeStruct((b, t // Q_BLOCK, A_WIDTH, Q_BLOCK), BF16),
                pl.BlockSpec((1, nq, A_WIDTH, Q_BLOCK), lambda i, j: (i, j, 0, 0)))
        outs = [feat, row(A_WIDTH, BF16), feat, row(B_KW, BF16), row(B_KW, BF16), row(B_VW, BF16)]
    else:
        outs = [row(A_WIDTH, BF16), row(A_WIDTH, F32), row(A_WIDTH, F32), row(B_CONV_CH, F32)]
    outs += [row(B_VW, BF16), row(LANES, F32), row(2 * D_MODEL, BF16)]
    weights = [wq, wk, wv, wcv, wzab, wg]
    scratch = []
    if prompt:
        outs += [stream(tm, A_WIDTH), stream(tm, A_WIDTH), stream(SUBLANES, B_CONV_CH)]
        weights.append(conv_w)
        scratch.append(pltpu.VMEM((SUBLANES, B_CONV_CH), F32))
    in_specs = [pl.BlockSpec((1, tm, d), lambda i, j: (i, j, 0)), _const_spec(gain.shape)]
    in_specs += [_const_spec(w.shape) for w in weights]
    return pl.pallas_call(
        functools.partial(_inproj_kernel, prompt=prompt),
        grid=(b, t // tm),
        in_specs=in_specs,
        out_specs=[o[1] for o in outs],
        out_shape=[o[0] for o in outs],
        scratch_shapes=scratch,
        compiler_params=_params(("arbitrary", "arbitrary")),
        name="inproj_prompt" if prompt else "inproj_sample",
    )(x, gain, *weights)


def _conv_sample_kernel(cv_ref, prev_ref, convw_ref, dq_ref, dk_ref, dv_ref):
    for s in range(cv_ref.shape[0]):
        dq, dk, dv = _conv_silu_l2norm(prev_ref[s], cv_ref[s], convw_ref)
        dq_ref[s] = dq.astype(BF16)
        dk_ref[s] = dk.astype(BF16)
        dv_ref[s] = dv.astype(BF16)


def _conv_sample(cv, prev, conv_w):
    b, t, _ = cv.shape
    blk = lambda r, w: pl.BlockSpec((b, r, w), lambda i: (0, 0, 0))
    out = jax.ShapeDtypeStruct((b, t, B_KW), BF16)
    return pl.pallas_call(
        _conv_sample_kernel,
        grid=(1,),
        in_specs=[blk(t, B_CONV_CH), blk(SUBLANES, B_CONV_CH), _const_spec(conv_w.shape)],
        out_specs=[blk(t, B_KW)] * 3,
        out_shape=[out] * 3,
        compiler_params=_params(("arbitrary",)),
        name="conv_sample",
    )(cv, prev, conv_w)


def _toeplitz_bias(rel_bias, rows, width, offset, sign=1, scale=1.0):
    pitch = width + rows
    period = pitch + 1
    j = np.arange(period)
    rel = sign * np.where(j < width, offset - j, offset + period - j)
    idx = np.clip(rel, -MAX_REL, MAX_REL) + MAX_REL
    vec = rel_bias.astype(F32)[:, idx] * scale
    tiled = jnp.tile(vec, (1, rows))[:, :rows * pitch]
    return tiled.reshape(rel_bias.shape[0], rows, pitch)[:, :, :width]


def _prompt_bias_diagonals(rel_bias):
    j = np.arange(2 * Q_BLOCK)
    d = np.where(j < Q_BLOCK, j, j - 2 * Q_BLOCK)
    rel = d[None, :] + A_REACH - Q_BLOCK * np.arange(N_KEY_BLOCKS)[:, None]
    idx = np.clip(rel, -MAX_REL, MAX_REL) + MAX_REL
    return rel_bias.astype(F32)[:, idx] * LOG2_E


def _fill_prompt_bias(diag_ref, bias_ref):
    kc = lax.broadcasted_iota(jnp.int32, (Q_BLOCK, Q_BLOCK), 0) // CHUNK
    qc = lax.broadcasted_iota(jnp.int32, (Q_BLOCK, Q_BLOCK), 1) // CHUNK
    for jb in range(N_KEY_BLOCKS):
        kcw = kc + jb * (Q_BLOCK // CHUNK)
        band = (kcw >= qc) & (kcw <= qc + A_PREV_CHUNKS)
        for h in range(A_HEADS):
            rows = jnp.broadcast_to(diag_ref[h, jb:jb + 1, :], (Q_BLOCK, 2 * Q_BLOCK))
            rotated = pltpu.roll(rows, 0, 1, stride=1, stride_axis=0)
            bias_ref[h, jb] = jnp.where(band, rotated[:, :Q_BLOCK], -jnp.inf)
    for h in range(A_HEADS):
        bias_ref[h, N_KEY_BLOCKS] = jnp.full((Q_BLOCK, Q_BLOCK), -jnp.inf, F32)


def _softmax_pv(scores, values):
    m = None
    for s in scores:
        mj = jnp.max(s, axis=-1, keepdims=True)
        m = mj if m is None else jnp.maximum(m, mj)
    denom = None
    acc = None
    for s, vals in zip(scores, values):
        p = jnp.exp(s - m)
        dj = jnp.sum(p, axis=-1, keepdims=True)
        denom = dj if denom is None else denom + dj
        pv = _dot(p.astype(BF16), vals)
        acc = pv if acc is None else acc + pv
    return acc / denom


def _attn_prompt_kernel(qt_ref, k_ref, vt_ref, diag_ref, o_ref, bias_ref):
    @pl.when((pl.program_id(0) == 0) & (pl.program_id(1) == 0))
    def _():
        _fill_prompt_bias(diag_ref, bias_ref)

    n_sub = qt_ref.shape[1]
    row = lax.broadcasted_iota(jnp.int32, (LANES, Q_BLOCK), 0)
    low = row < A_HEAD_DIM
    kbs, bsels = {}, {}
    for u in range(n_sub):
        qi = pl.program_id(1) * n_sub + u
        for j in range(N_KEY_BLOCKS):
            kraw = qi - (N_KEY_BLOCKS - 1) + j
            kbs[u, j] = jnp.maximum(kraw, 0)
            bsels[u, j] = jnp.where(kraw >= 0, j, N_KEY_BLOCKS)
    n_qcols = Q_BLOCK // LANES

    def scores(u, h):
        cols = slice((h // 2) * LANES, (h // 2 + 1) * LANES)
        qp = qt_ref[0, u, cols, :]
        qm = jnp.where(low if h % 2 == 0 else jnp.logical_not(low), qp, jnp.zeros_like(qp))
        out = []
        for j in range(N_KEY_BLOCKS):
            kblk = k_ref[0, pl.ds(pl.multiple_of(kbs[u, j] * Q_BLOCK, Q_BLOCK), Q_BLOCK), cols]
            out.append(_dot(kblk, qm) + bias_ref[h, bsels[u, j]])
        return out

    def softmax_pv(u, h, s):
        cols = slice((h // 2) * LANES, (h // 2 + 1) * LANES)
        p_cols = []
        for qc in range(n_qcols):
            ls = slice(qc * LANES, (qc + 1) * LANES)
            lo, hi = qc * LANES, qc * LANES + A_REACH + LANES
            spans = [(max(lo - j * Q_BLOCK, 0), min(hi - j * Q_BLOCK, Q_BLOCK)) for j in range(N_KEY_BLOCKS)]
            parts = [s[j][a:b, ls] for j, (a, b) in enumerate(spans)]
            m = functools.reduce(jnp.maximum, [jnp.max(x, axis=0, keepdims=True) for x in parts])
            ps = [jnp.exp2(x - m) for x in parts]
            padded = []
            for x, (a, b) in zip(ps, spans):
                pieces = ([jnp.zeros((a, LANES), F32)] if a else []) + [x]
                pieces += [jnp.zeros((Q_BLOCK - b, LANES), F32)] if b < Q_BLOCK else []
                padded.append(pieces[0] if len(pieces) == 1 else jnp.concatenate(pieces, axis=0))
            p_cols.append(padded)
        ones = jnp.ones((ONES_ROWS, Q_BLOCK), BF16)
        acc = None
        for j in range(N_KEY_BLOCKS):
            pt = jnp.concatenate([p_cols[qc][j] for qc in range(n_qcols)], axis=1).astype(BF16)
            pv = _dot(jnp.concatenate([vt_ref[0, kbs[u, j], cols, :], ones], axis=0), pt)
            acc = pv if acc is None else acc + pv
        return acc[:LANES] * (1.0 / acc[LANES:LANES + 1])

    tasks = [(u, h) for u in range(n_sub) for h in range(A_HEADS)]
    ready = [scores(*t) for t in tasks[:SCORE_LOOKAHEAD]]
    pair_out = []
    for i, (u, h) in enumerate(tasks):
        if i + SCORE_LOOKAHEAD < len(tasks):
            ready.append(scores(*tasks[i + SCORE_LOOKAHEAD]))
        pair_out.append(softmax_pv(u, h, ready.pop(0)))
        if h % 2 == 1:
            cols = slice((h // 2) * LANES, (h // 2 + 1) * LANES)
            o_pair = jnp.concatenate([pair_out[0][:A_HEAD_DIM], pair_out[1][A_HEAD_DIM:]], axis=0)
            o_ref[0, u * Q_BLOCK:(u + 1) * Q_BLOCK, cols] = o_pair.T.astype(BF16)
            pair_out = []


def _attn_prompt(qt, k, vt, diag):
    b, t, _ = k.shape
    n_sub = min(ATTN_Q_BLOCKS_PER_STEP, t // Q_BLOCK)
    assert t % (n_sub * Q_BLOCK) == 0
    return pl.pallas_call(
        _attn_prompt_kernel,
        grid=(b, t // (n_sub * Q_BLOCK)),
        in_specs=[
            pl.BlockSpec((1, n_sub, A_WIDTH, Q_BLOCK), lambda i, j: (i, j, 0, 0)),
            pl.BlockSpec((1, t, A_WIDTH), lambda i, j: (i, 0, 0)),
            pl.BlockSpec((1, t // Q_BLOCK, A_WIDTH, Q_BLOCK), lambda i, j: (i, 0, 0, 0)),
            _const_spec(diag.shape),
        ],
        out_specs=pl.BlockSpec((1, n_sub * Q_BLOCK, A_WIDTH), lambda i, j: (i, j, 0)),
        out_shape=jax.ShapeDtypeStruct((b, t, A_WIDTH), BF16),
        scratch_shapes=[pltpu.VMEM((A_HEADS, N_KEY_BLOCKS + 1, Q_BLOCK, Q_BLOCK), F32)],
        compiler_params=_params(("arbitrary", "arbitrary")),
        name="attn_prompt",
    )(qt, k, vt, diag)


def _attn_sample_kernel(q_ref, ck_ref, cv_ref, nk_ref, nv_ref, bc_ref, bn_ref, o_ref):
    rows = q_ref.shape[1]
    lane = lax.broadcasted_iota(jnp.int32, (rows, LANES), 1)
    low = lane < A_HEAD_DIM
    pair_cols = [slice(hp * LANES, (hp + 1) * LANES) for hp in range(A_HEADS // 2)]
    tasks = [(s, h) for s in range(q_ref.shape[0]) for h in range(A_HEADS)]
    scores = {}
    for s, h in tasks:
        cols = pair_cols[h // 2]
        qp = q_ref[s, :, cols]
        qm = jnp.where(low if h % 2 == 0 else jnp.logical_not(low), qp, jnp.zeros_like(qp))
        scores[s, h] = [_dot_nt(qm, ck_ref[s, :, cols].astype(BF16)) + bc_ref[h],
                        _dot_nt(qm, nk_ref[s, :, cols].astype(BF16)) + bn_ref[h]]
    outs = {(s, h): _softmax_pv(scores[s, h], [cv_ref[s, :, pair_cols[h // 2]].astype(BF16),
                                               nv_ref[s, :, pair_cols[h // 2]].astype(BF16)]) for s, h in tasks}
    for s in range(q_ref.shape[0]):
        for hp, cols in enumerate(pair_cols):
            o_ref[s, :, cols] = jnp.where(low, outs[s, 2 * hp], outs[s, 2 * hp + 1]).astype(BF16)


def _attn_sample(q, cache_k, cache_v, new_k, new_v, rel_bias):
    b, t, _ = q.shape
    c = cache_k.shape[1]
    bias = _toeplitz_bias(rel_bias, t, c + LANES, c)
    bias = jnp.where((np.arange(c + LANES) < c + t)[None, None, :], bias, -jnp.inf)
    bias_c, bias_n = bias[:, :, :c], bias[:, :, c:]
    pad = ((0, 0), (0, LANES - t), (0, 0))
    nk, nv = jnp.pad(new_k, pad), jnp.pad(new_v, pad)
    ns = min(SAMPLE_ATTN_STREAMS, b)
    assert b % ns == 0
    blk = lambda r: pl.BlockSpec((ns, r, A_WIDTH), lambda i: (i, 0, 0))
    return pl.pallas_call(
        _attn_sample_kernel,
        grid=(b // ns,),
        in_specs=[blk(t), blk(c), blk(c), blk(LANES), blk(LANES),
                  _const_spec(bias_c.shape), _const_spec(bias_n.shape)],
        out_specs=blk(t),
        out_shape=jax.ShapeDtypeStruct((b, t, A_WIDTH), BF16),
        compiler_params=_params(("arbitrary",)),
        name="attn_sample",
    )(q, cache_k, cache_v, nk, nv, bias_c, bias_n)


def _block_diag(x2):
    half = x2.shape[1] // 2
    z = jnp.zeros_like(x2[:, :half])
    top = jnp.concatenate([x2[:, :half], z], axis=1)
    bot = jnp.concatenate([z, x2[:, half:]], axis=1)
    return jnp.concatenate([top, bot], axis=0)


def _quad_diag(x4):
    n = x4.shape[0]
    group = lax.broadcasted_iota(jnp.int32, x4.shape, 1) // n
    zero = jnp.zeros_like(x4)
    return jnp.concatenate([jnp.where(group == g, x4, zero) for g in range(4)], axis=0)


def _unit_lower_inverse(m, eye_half):
    L = m.shape[0]
    h = L // 2
    lane = lax.broadcasted_iota(jnp.int32, (h, 2 * L), 1)
    first_half = (lane & (L - 1)) < h
    left_head = lane < L
    diag = jnp.where(first_half, m[:h], m[h:])
    b16 = jnp.where(first_half, m[h:], 0.0).astype(BF16)
    t = eye_half + diag
    d16 = diag.astype(BF16)
    p = _dot(d16, _quad_diag(d16))
    yield
    n_levels = h.bit_length() - 2
    for lvl in range(n_levels):
        p16 = p.astype(BF16)
        t16 = t.astype(BF16)
        if lvl == n_levels - 1:
            t = t + _dot(t16, _quad_diag(p16))
        else:
            res = _dot(jnp.concatenate([t16, p16], axis=0), _quad_diag(p16))
            t = t + res[:h]
            p = res[h:]
        yield
    t16 = t.astype(BF16)
    zero16 = jnp.zeros_like(t16)
    ta_only = jnp.where(first_half, t16, zero16)
    w_a = jnp.concatenate([jnp.where(left_head, ta_only, zero16), zero16,
                           jnp.where(left_head, zero16, ta_only), zero16], axis=0)
    y16 = _dot(b16, w_a).astype(BF16)
    yield
    w_y = jnp.concatenate([zero16, jnp.where(left_head, y16, zero16),
                           zero16, jnp.where(left_head, zero16, y16)], axis=0)
    t_b = _dot(t16, w_y)
    return jnp.where(first_half, t, 0.0), jnp.where(first_half, t_b, t)


def _round_robin(generators):
    results = [None] * len(generators)
    active = list(enumerate(generators))
    while active:
        still = []
        for i, g in active:
            try:
                next(g)
                still.append((i, g))
            except StopIteration as stop:
                results[i] = stop.value
        active = still
    return results


def _delta_kernel(q_ref, k_ref, v_ref, z_ref, ab_ref, s0_ref, alog_ref, dtb_ref, nw_ref,
                  o_ref, s_ref, rows_ref, *, n_valid):
    L = DELTA_CHUNK
    ns, tb = q_ref.shape[0], q_ref.shape[1]
    nch = tb // L
    n_pairs = B_HEADS // 2

    @pl.when(pl.program_id(1) == 0)
    def _():
        s_ref[...] = s0_ref[...]

    chains = [(s, c, p) for s in range(ns) for c in range(nch) for p in range(n_pairs)]

    st = {}
    for s, c, p in chains:
        r = slice(c * L, (c + 1) * L)
        ls = slice(p * 2 * L, (p + 1) * 2 * L)
        q2 = q_ref[s, r, ls]
        k2 = k_ref[s, r, ls]
        sc = _dot_nt(jnp.concatenate([q2, k2], axis=0), _block_diag(k2))
        st[s, c, p] = dict(q2=q2.astype(F32), k2=k2.astype(F32), qk=sc[:L], kk=sc[L:])

    scal = _round_robin([_delta_scalars(ab_ref, alog_ref, dtb_ref, rows_ref, s, n_valid) for s in range(ns)])
    _delta_chunks(st, chains, scal, v_ref, z_ref, nw_ref, o_ref, s_ref)


def _delta_scalars(ab_ref, alog_ref, dtb_ref, rows_ref, s, n_valid):
    L = DELTA_CHUNK
    tb = ab_ref.shape[1]
    nch = tb // L
    ab_t = ab_ref[s].T
    rows8 = ab_t[0:2 * B_HEADS, :]
    x = rows8 + dtb_ref[:, 0:1]
    softplus = jnp.maximum(x, 0.0) + jnp.log(1.0 + jnp.exp(-jnp.abs(x)))
    g8 = -jnp.exp(alog_ref[:, 0:1]) * softplus
    beta8 = _sigmoid(rows8)
    tok = lax.broadcasted_iota(jnp.int32, (2 * B_HEADS, tb), 1)
    if n_valid < tb:
        g8 = jnp.where(tok < n_valid, g8, 0.0)
        beta8 = jnp.where(tok < n_valid, beta8, 0.0)
    pos = tok & (L - 1)
    gc8 = g8
    shift = 1
    while shift < L:
        yield
        gc8 = gc8 + jnp.where(pos >= shift, pltpu.roll(gc8, shift, 1), 0.0)
        shift *= 2
    gl8 = [gc8[:, (c + 1) * L - 1:(c + 1) * L] for c in range(nch)]
    glb8 = jnp.concatenate([jnp.broadcast_to(g, (2 * B_HEADS, L)) for g in gl8], axis=1)
    rows_ref[s] = jnp.zeros(rows_ref.shape[1:], F32)
    rows_ref[s, 0:8, :] = gc8
    rows_ref[s, 8:16, :] = beta8
    rows_ref[s, 16:24, :] = jnp.exp(gc8)
    rows_ref[s, 24:32, :] = jnp.exp(glb8 - gc8)
    cols = rows_ref[s].T
    return gc8, gl8, cols


def _delta_chunks(st, chains, scal, v_ref, z_ref, nw_ref, o_ref, s_ref):
    L = DELTA_CHUNK
    ns, tb = v_ref.shape[0], v_ref.shape[1]
    nch = tb // L
    n_pairs = B_HEADS // 2
    ri = lax.broadcasted_iota(jnp.int32, (L, 2 * L), 0)
    ci = lax.broadcasted_iota(jnp.int32, (L, 2 * L), 1) & (L - 1)
    tril = ri >= ci
    strict = ri > ci
    rh = lax.broadcasted_iota(jnp.int32, (L // 2, 2 * L), 0)
    ch = lax.broadcasted_iota(jnp.int32, (L // 2, 2 * L), 1) & (L // 2 - 1)
    eye_half = (rh == ch).astype(F32)

    def chain_ops(s, c, p):
        gc8, _, cols = scal[s]
        r = slice(c * L, (c + 1) * L)
        ls = slice(p * 2 * L, (p + 1) * 2 * L)
        h0 = 2 * p
        d = st[s, c, p]
        pair_cols = lambda i: [jnp.broadcast_to(cols[r, i + e:i + e + 1], (L, L)) for e in range(2)]
        gc_c = jnp.concatenate(pair_cols(h0), axis=1)
        beta = pair_cols(8 + B_HEADS + h0)
        egc = pair_cols(16 + h0)
        edec = pair_cols(24 + h0)
        gc_r = jnp.concatenate([jnp.broadcast_to(gc8[h0 + e:h0 + e + 1, r], (L, L)) for e in range(2)], axis=1)
        decay = jnp.exp(jnp.where(tril, gc_c - gc_r, -jnp.inf))
        m = jnp.where(strict, -(jnp.concatenate(beta, axis=1) * d.pop("kk") * decay), 0.0)
        qkd16 = (d.pop("qk") * decay).astype(BF16)
        v2 = v_ref[s, r, ls].astype(F32)
        yield
        top, bot = yield from _unit_lower_inverse(m, eye_half)
        yield
        t16 = jnp.concatenate([top, bot], axis=0).astype(BF16)
        sols = []
        for e in range(2):
            hs = slice(e * L, (e + 1) * L)
            kh = d["k2"][:, hs]
            rhs = jnp.concatenate([v2[:, hs] * beta[e], kh * (beta[e] * egc[e])], axis=1)
            sols.append(_dot(t16[:, hs], rhs.astype(BF16)).astype(BF16))
        yield
        phi, psi, qt, oin = [], [], [], []
        for e in range(2):
            hs = slice(e * L, (e + 1) * L)
            kdt = (d["k2"][:, hs] * edec[e]).T.astype(BF16)
            r2 = _dot(jnp.concatenate([qkd16[:, hs], kdt], axis=0), sols[e])
            oin.append(r2[:L, :B_DV])
            qt.append(d["q2"][:, hs] * egc[e] - r2[:L, B_DV:])
            psi.append(r2[L:, :B_DV])
            phi.append(-r2[L:, B_DV:])
        d["lhs"] = jnp.concatenate([jnp.concatenate(phi, axis=1), jnp.concatenate(qt, axis=1)], axis=0).astype(BF16)
        d["psi"] = jnp.concatenate(psi, axis=1)
        d["oin"] = jnp.concatenate(oin, axis=1)

    lanes = [slice(p * 2 * L, (p + 1) * 2 * L) for p in range(n_pairs)]
    carried = [(s, p) for s in range(ns) for p in range(n_pairs)]
    s2 = {(s, p): s_ref[s, :, lanes[p]] for s, p in carried}
    nw = nw_ref[...]
    pending = []

    def emit(c, o2):
        r = slice(c * L, (c + 1) * L)
        for (s, p), o in o2.items():
            for e in range(2):
                hs = slice((2 * p + e) * B_DV, (2 * p + e + 1) * B_DV)
                zh = z_ref[s, r, hs].astype(F32)
                o_ref[s, r, hs] = (_rms(o[:, e * B_DV:(e + 1) * B_DV], nw) * _silu_of_twice(0.5 * zh)).astype(BF16)

    def carry_state(c):
        o2 = {}
        for s, p in carried:
            d = st[s, c, p]
            gl = scal[s][1][c]
            res = _dot(d["lhs"], _block_diag(s2[s, p].astype(BF16)))
            o2[s, p] = res[L:] + d["oin"]
            egl = jnp.concatenate([jnp.broadcast_to(jnp.exp(gl[2 * p + e:2 * p + e + 1, :]), (1, B_DV))
                                   for e in range(2)], axis=1)
            s2[s, p] = s2[s, p] * egl + res[:L] + d["psi"]
        if pending:
            emit(*pending.pop())
        pending.append((c, o2))

    waiting = [(key, chain_ops(*key)) for key in sorted(chains, key=lambda k: (k[1], k[0], k[2]))]
    live, finished, next_chunk = [], [0] * nch, 0
    while waiting or live:
        live += waiting[:CHAIN_STARTS_PER_ROUND]
        waiting = waiting[CHAIN_STARTS_PER_ROUND:]
        still = []
        for key, g in live:
            try:
                next(g)
                still.append((key, g))
            except StopIteration:
                finished[key[1]] += 1
        live = still
        while next_chunk < nch and finished[next_chunk] == len(carried):
            carry_state(next_chunk)
            next_chunk += 1
    emit(*pending.pop())
    for s, p in carried:
        s_ref[s, :, lanes[p]] = s2[s, p]


def _delta(q, k, v, z, ab, s0, a_log, dt_bias, norm_w, *, block, streams, n_valid):
    b, t, _ = q.shape
    streams = min(streams, b)
    assert t % block == 0 and block % DELTA_CHUNK == 0 and b % streams == 0
    alog8 = jnp.broadcast_to(jnp.tile(a_log.astype(F32), 2)[:, None], (2 * B_HEADS, LANES))
    dtb8 = jnp.broadcast_to(jnp.tile(dt_bias.astype(F32), 2)[:, None], (2 * B_HEADS, LANES))
    row = lambda w: pl.BlockSpec((streams, block, w), lambda i, j: (i, j, 0))
    state = pl.BlockSpec((streams, B_DK, B_HEADS * B_DV), lambda i, j: (i, 0, 0))
    s0 = s0.astype(F32).transpose(0, 2, 1, 3).reshape(b, B_DK, B_HEADS * B_DV)
    o, s = pl.pallas_call(
        functools.partial(_delta_kernel, n_valid=n_valid),
        grid=(b // streams, t // block),
        in_specs=[row(B_KW), row(B_KW), row(B_VW), row(B_VW), row(LANES), state,
                  _const_spec(alog8.shape), _const_spec(dtb8.shape), _const_spec((1, B_DV))],
        out_specs=[row(B_VW), state],
        out_shape=[jax.ShapeDtypeStruct((b, t, B_VW), BF16),
                   jax.ShapeDtypeStruct((b, B_DK, B_HEADS * B_DV), F32)],
        scratch_shapes=[pltpu.VMEM((streams, LANES, block), F32)],
        compiler_params=_params(("arbitrary", "arbitrary")),
        name="delta_%d" % block,
    )(q, k, v, z, ab, s0, alog8, dtb8, norm_w.reshape(1, B_DV).astype(F32))
    return o, s.reshape(b, B_DK, B_HEADS, B_DV).transpose(0, 2, 1, 3)


def _merge_kernel(x_ref, oa_ref, ob_ref, g_ref, wpa_ref, wpb_ref, wout_ref, gain_ref, o_ref):
    tm = x_ref.shape[0]
    sub = min(ROW_TILE, tm)
    blocks = [slice(r, r + sub) for r in range(0, tm, sub)]

    def branches(rs):
        return _dot(oa_ref[rs, :], wpa_ref[...]), _dot(ob_ref[rs, :], wpb_ref[...])

    nxt = branches(blocks[0])
    for i, rs in enumerate(blocks):
        ya, yb = nxt
        if i + 1 < len(blocks):
            nxt = branches(blocks[i + 1])
        merged = g_ref[rs, :D_MODEL].astype(F32) * ya + g_ref[rs, D_MODEL:].astype(F32) * yb
        y = _dot(merged.astype(BF16), wout_ref[...])
        o_ref[rs, :] = x_ref[rs, :] + _rms(y, gain_ref[...])


def _merge(x, oa, ob, gates, wpa, wpb, wout, gain):
    n, d = x.shape
    tm = min(MERGE_ROWS, n)
    row = lambda w: pl.BlockSpec((tm, w), lambda i: (i, 0))
    return pl.pallas_call(
        _merge_kernel,
        grid=(n // tm,),
        in_specs=[row(d), row(A_WIDTH), row(B_VW), row(2 * d),
                  _const_spec(wpa.shape), _const_spec(wpb.shape), _const_spec(wout.shape), _const_spec(gain.shape)],
        out_specs=row(d),
        out_shape=jax.ShapeDtypeStruct((n, d), F32),
        compiler_params=_params(("arbitrary",)),
        name="merge_%d" % n,
    )(x, oa, ob, gates, wpa, wpb, wout, gain)


FF_SPLIT = 4


def _mlp_kernel(x_ref, gpre_ref, wup_ref, wdown_ref, gpost_ref, o_ref):
    tm = x_ref.shape[0]
    sub = min(ROW_TILE, tm)
    blocks = [slice(r, r + sub) for r in range(0, tm, sub)]
    groups = [slice(c, c + D_FF // FF_SPLIT) for c in range(0, D_FF, D_FF // FF_SPLIT)]
    hbs = [_rms(x_ref[rs, :], gpre_ref[...]).astype(BF16) for rs in blocks]
    finish = None
    for rs, hb in zip(blocks, hbs):
        f = None
        for c, cs in enumerate(groups):
            up = jnp.maximum(_dot(hb, wup_ref[:, cs]), 0.0)
            if c == 0 and finish is not None:
                finish()
            part = _dot((up * up).astype(BF16), wdown_ref[cs, :])
            f = part if f is None else f + part

        def finish(rs=rs, f=f):
            o_ref[rs, :] = x_ref[rs, :] + _rms(f, gpost_ref[...])
    finish()


def _mlp(x, gpre, wup, wdown, gpost):
    n, d = x.shape
    tm = min(MLP_ROWS, n)
    row = pl.BlockSpec((tm, d), lambda i: (i, 0))
    return pl.pallas_call(
        _mlp_kernel,
        grid=(n // tm,),
        in_specs=[row, _const_spec(gpre.shape), _const_spec(wup.shape), _const_spec(wdown.shape),
                  _const_spec(gpost.shape)],
        out_specs=row,
        out_shape=jax.ShapeDtypeStruct((n, d), F32),
        compiler_params=_params(("arbitrary",)),
        name="mlp_%d" % n,
    )(x, gpre, wup, wdown, gpost)


def _layer_weights(norm_mix_pre, w_in, conv_w, w_proj_a, w_proj_b, w_out, norm_mix_post,
                   norm_mlp_pre, w_up, w_down, norm_mlp_post):
    o_cv = 3 * A_WIDTH
    o_z = o_cv + B_CONV_CH
    o_ab = o_z + B_VW
    o_g = o_ab + 2 * B_HEADS
    wb = w_in.astype(BF16)
    wzab = jnp.concatenate([wb[:, o_z:o_g], jnp.zeros((D_MODEL, LANES - 2 * B_HEADS), BF16)], axis=1)
    vec = lambda g: g.reshape(1, -1).astype(F32)
    return dict(
        gain_in=vec(norm_mix_pre),
        wq=wb[:, :A_WIDTH], wk=wb[:, A_WIDTH:2 * A_WIDTH], wv=wb[:, 2 * A_WIDTH:o_cv],
        wcv=wb[:, o_cv:o_z], wzab=wzab, wg=wb[:, o_g:] * 0.5,
        conv_w=conv_w.astype(F32),
        wpa=w_proj_a.astype(BF16), wpb=w_proj_b.astype(BF16), wout=w_out.astype(BF16),
        gain_mix=vec(norm_mix_post), gain_pre=vec(norm_mlp_pre), gain_post=vec(norm_mlp_post),
        wup=w_up.astype(BF16), wdown=w_down.astype(BF16),
    )


def _tail(x, w, oa, ob, gates):
    b, t, d = x.shape
    x1 = _merge(x.reshape(b * t, d), oa.reshape(b * t, A_WIDTH), ob.reshape(b * t, B_VW),
                gates.reshape(b * t, 2 * d), w["wpa"], w["wpb"], w["wout"], w["gain_mix"])
    return _mlp(x1, w["gain_pre"], w["wup"], w["wdown"], w["gain_post"]).reshape(b, t, d)


def _prompt_layer(x, w, rel_bias, a_log, dt_bias, delta_norm_w):
    b, t, _ = x.shape
    qt, k, vt, dq, dk, dv, z, ab, gates, kf, vf, cvt = _inproj(
        x, w["gain_in"], w["wq"], w["wk"], w["wv"], w["wcv"], w["wzab"], w["wg"], w["conv_w"], prompt=True)
    oa = _attn_prompt(qt, k, vt, _prompt_bias_diagonals(rel_bias))
    s0 = jnp.zeros((b, B_HEADS, B_DK, B_DV), F32)
    ob, s_new = _delta(dq, dk, dv, z, ab, s0, a_log, dt_bias, delta_norm_w,
                       block=DELTA_BLOCK, streams=DELTA_STREAMS, n_valid=DELTA_BLOCK)
    y = _tail(x, w, oa, ob, gates)
    keep = min(A_REACH, t)
    new_k = kf.reshape(b, keep, A_HEADS, A_HEAD_DIM)
    new_v = vf.reshape(b, keep, A_HEADS, A_HEAD_DIM)
    new_conv = cvt[:, SUBLANES - (CONV_WIDTH - 1):, :]
    return y, new_k, new_v, new_conv, s_new


def _sample_layer(x, cache_k, cache_v, conv_state, delta_state, w, rel_bias, a_log, dt_bias, delta_norm_w):
    b, t, d = x.shape
    c = cache_k.shape[1]
    q, k, v, cv, z, ab, gates = _inproj(
        x.reshape(1, b * t, d), w["gain_in"], w["wq"], w["wk"], w["wv"], w["wcv"], w["wzab"], w["wg"],
        prompt=False)
    per_stream = lambda a: a.reshape(b, t, a.shape[-1])
    q, k, v, cv, z, ab, gates = map(per_stream, (q, k, v, cv, z, ab, gates))
    oa = _attn_sample(q, cache_k.reshape(b, c, A_WIDTH), cache_v.reshape(b, c, A_WIDTH), k, v, rel_bias)
    prev = jnp.pad(conv_state.astype(F32), ((0, 0), (SUBLANES - (CONV_WIDTH - 1), 0), (0, 0)))
    dq, dk, dv = _conv_sample(cv, prev, w["conv_w"])
    pad = lambda a: jnp.pad(a, ((0, 0), (0, DELTA_CHUNK - t), (0, 0)))
    ob, s_new = _delta(pad(dq), pad(dk), pad(dv), pad(z), pad(ab), delta_state, a_log, dt_bias, delta_norm_w,
                       block=DELTA_CHUNK, streams=SAMPLE_DELTA_STREAMS, n_valid=t)
    y = _tail(x, w, oa, ob[:, :t], gates)
    new_conv = jnp.concatenate([conv_state.astype(F32), cv], axis=1)[:, -(CONV_WIDTH - 1):]
    return (y, k.reshape(b, t, A_HEADS, A_HEAD_DIM), v.reshape(b, t, A_HEADS, A_HEAD_DIM), new_conv, s_new)


def kernel(x_prompt, x_sample, cache_attn_k, cache_attn_v, state_conv, state_delta, norm_mix_pre, w_in,
           rel_bias, conv_w, a_log, dt_bias, delta_norm_w, w_proj_a, w_proj_b, w_out, norm_mix_post,
           norm_mlp_pre, w_up, w_down, norm_mlp_post):
    depth = w_in.shape[0]
    yp, ys = x_prompt, x_sample
    outs = [[] for _ in range(8)]
    for l in range(depth):
        w = _layer_weights(norm_mix_pre[l], w_in[l], conv_w[l], w_proj_a[l], w_proj_b[l], w_out[l],
                           norm_mix_post[l], norm_mlp_pre[l], w_up[l], w_down[l], norm_mlp_post[l])
        yp, k1, v1, c1, s1 = _prompt_layer(yp, w, rel_bias[l], a_log[l], dt_bias[l], delta_norm_w[l])
        ys, k2, v2, c2, s2 = _sample_layer(ys, cache_attn_k[l], cache_attn_v[l], state_conv[l], state_delta[l],
                                           w, rel_bias[l], a_log[l], dt_bias[l], delta_norm_w[l])
        for lst, val in zip(outs, (k1, v1, c1, s1, k2, v2, c2, s2)):
            lst.append(val)
    return (yp, ys) + tuple(jnp.stack(o) for o in outs)
```

```python
import functools

import numpy as np
import jax
import jax.numpy as jnp
from jax import lax
from jax.experimental import pallas as pl
from jax.experimental.pallas import tpu as pltpu

F32 = jnp.float32
BF16 = jnp.bfloat16

D_MODEL = 1024
CHUNK = 64
RMS_EPS = 1e-6
L2_EPS = 1e-6
A_HEADS = 8
A_HEAD_DIM = 64
A_WIDTH = A_HEADS * A_HEAD_DIM
A_PREV_CHUNKS = 8
A_REACH = A_PREV_CHUNKS * CHUNK
MAX_REL = 128
B_HEADS = 4
B_DK = 128
B_DV = 128
B_KW = B_HEADS * B_DK
B_VW = B_HEADS * B_DV
B_CONV_CH = 2 * B_KW + B_VW
CONV_WIDTH = 4
D_FF = 4 * D_MODEL

LANES = 128
SUBLANES = 8
VMEM_LIMIT_BYTES = 56 * 1024 * 1024

Q_BLOCK = 256
N_KEY_BLOCKS = A_REACH // Q_BLOCK + 1
SCORE_LOOKAHEAD = 2
ATTN_Q_BLOCKS_PER_STEP = 8
ONES_ROWS = 16
LOG2_E = 1.4426950408889634
DELTA_CHUNK = 128
DELTA_BLOCK = 512
DELTA_STREAMS = 4
SAMPLE_DELTA_STREAMS = 8
SAMPLE_ATTN_STREAMS = 4
CHAIN_STARTS_PER_ROUND = 2
ROW_TILE = 512
CONV_ROWS = 512
GATE_COLS = 512
AB_ROWS = 16
MERGE_ROWS = 1024
MLP_ROWS = 1024

assert B_DK == LANES and B_DV == LANES and B_HEADS % 2 == 0


def _dot(a, b):
    return jnp.dot(a, b, preferred_element_type=F32)


def _dot_nt(a, b):
    return lax.dot_general(a, b, (((1,), (1,)), ((), ())), preferred_element_type=F32)


def _sigmoid_of_twice(h):
    return 0.5 * jnp.tanh(h) + 0.5


def _sigmoid(x):
    return _sigmoid_of_twice(0.5 * x)


def _silu_of_twice(h):
    return h * jnp.tanh(h) + h


def _rms(x, gain):
    ms = jnp.mean(x * x, axis=-1, keepdims=True)
    return x * lax.rsqrt(ms + RMS_EPS) * gain


def _params(sem):
    return pltpu.CompilerParams(dimension_semantics=sem, vmem_limit_bytes=VMEM_LIMIT_BYTES)


def _const_spec(shape):
    nd = len(shape)
    return pl.BlockSpec(shape, lambda *_: (0,) * nd, pipeline_mode=pl.Buffered(1))


def _conv_silu(hist, x, w):
    n = x.shape[0]
    full = jnp.concatenate([hist, x], axis=0)
    rb = min(CONV_ROWS, n)
    half_w = 0.5 * w
    out = []
    for r0 in range(0, n, rb):
        piece = full[r0:r0 + rb + SUBLANES, :]
        z = piece * half_w[0:1, :]
        for i in range(1, CONV_WIDTH):
            z = pltpu.roll(z, 1, 0) + piece * half_w[i:i + 1, :]
        out.append(_silu_of_twice(z[SUBLANES:, :]))
    return out[0] if len(out) == 1 else jnp.concatenate(out, axis=0)


def _l2norm_heads(y, scale):
    heads = []
    for h in range(y.shape[1] // B_DK):
        yh = y[:, h * B_DK:(h + 1) * B_DK]
        heads.append(yh * (lax.rsqrt(jnp.sum(yh * yh, axis=-1, keepdims=True) + L2_EPS) * scale))
    return jnp.concatenate(heads, axis=1)


def _conv_silu_l2norm(hist, x, convw_ref):
    y = _conv_silu(hist, x, convw_ref[...])
    return (_l2norm_heads(y[:, :B_KW], B_DK ** -0.5), _l2norm_heads(y[:, B_KW:2 * B_KW], 1.0), y[:, 2 * B_KW:])


def _inproj_kernel(*refs, prompt):
    x_ref, gain_ref, wq_ref, wk_ref, wv_ref, wcv_ref, wzab_ref, wg_ref = refs[:8]
    if prompt:
        (convw_ref, q_ref, k_ref, vt_ref, dq_ref, dk_ref, dv_ref, z_ref, ab_ref, g_ref,
         kf_ref, vf_ref, cvt_ref, hist_ref) = refs[8:]
    else:
        q_ref, kf_ref, vf_ref, cv_ref, z_ref, ab_ref, g_ref = refs[8:]
    if prompt:
        @pl.when(pl.program_id(1) == 0)
        def _():
            hist_ref[...] = jnp.zeros((SUBLANES, B_CONV_CH), F32)

    x = x_ref[0]
    hb = _rms(x, gain_ref[...]).astype(BF16)
    tm = x.shape[0]

    stages = []

    def stage(issue):
        return lambda consume: stages.append((issue, consume))

    if prompt:
        norms = (B_DK ** -0.5, 1.0, None)
        for part, out_ref in enumerate((dq_ref, dk_ref, dv_ref)):
            cs = slice(part * B_KW, (part + 1) * B_KW)

            @stage(lambda cs=cs: _dot(hb, wcv_ref[:, cs]))
            def _(cv, cs=cs, out_ref=out_ref, scale=norms[part]):
                y = _conv_silu(hist_ref[:, cs], cv, convw_ref[:, cs])
                out_ref[0] = (y if scale is None else _l2norm_heads(y, scale)).astype(BF16)
                hist_ref[:, cs] = cv[tm - SUBLANES:, :]
                cvt_ref[0, :, cs] = cv[tm - SUBLANES:, :]
    else:
        @stage(lambda: _dot(hb, wcv_ref[...]))
        def _(cv):
            cv_ref[0] = cv

    def feature_major(val, out_ref):
        vt = val.T.astype(BF16)
        for j in range(tm // Q_BLOCK):
            out_ref[0, j] = vt[:, j * Q_BLOCK:(j + 1) * Q_BLOCK]

    @stage(lambda: _dot(hb, wq_ref[...]))
    def _(q):
        if prompt:
            feature_major(q * (A_HEAD_DIM ** -0.5 * LOG2_E), q_ref)
        else:
            q_ref[0] = (q * (A_HEAD_DIM ** -0.5)).astype(BF16)

    @stage(lambda: _dot(hb, wk_ref[...]))
    def _(k):
        kf_ref[0] = k
        if prompt:
            k_ref[0] = k.astype(BF16)

    @stage(lambda: _dot(hb, wv_ref[...]))
    def _(v):
        vf_ref[0] = v
        if prompt:
            feature_major(v, vt_ref)

    @stage(lambda: _dot(hb, wzab_ref[...]))
    def _(zab):
        z_ref[0] = zab[:, :B_VW].astype(BF16)
        ab_ref[0] = zab[:, B_VW:].T[0:AB_ROWS, :]

    for c0 in range(0, 2 * D_MODEL, GATE_COLS):
        cs = slice(c0, c0 + GATE_COLS)

        @stage(lambda cs=cs: _dot(hb, wg_ref[:, cs]))
        def _(g, cs=cs):
            g_ref[0, :, cs] = _sigmoid_of_twice(g).astype(BF16)

    pending = None
    for issue, consume in stages:
        val = issue()
        if pending is not None:
            pending[1](pending[0])
        pending = (val, consume)
    pending[1](pending[0])


def _inproj(x, gain, wq, wk, wv, wcv, wzab, wg, conv_w=None, *, prompt):
    b, t, d = x.shape
    tm = min(ROW_TILE, t)
    assert t % tm == 0
    row = lambda w, dt: (jax.ShapeDtypeStruct((b, t, w), dt), pl.BlockSpec((1, tm, w), lambda i, j: (i, j, 0)))
    stream = lambda r, w: (jax.ShapeDtypeStruct((b, r, w), F32), pl.BlockSpec((1, r, w), lambda i, j: (i, 0, 0)))
    if prompt:
        assert tm == A_REACH and tm % Q_BLOCK == 0
        nq = tm // Q_BLOCK
        feat = (jax.ShapeDtypeStruct((b, t // Q_BLOCK, A_WIDTH, Q_BLOCK), BF16),
                pl.BlockSpec((1, nq, A_WIDTH, Q_BLOCK), lambda i, j: (i, j, 0, 0)))
        outs = [feat, row(A_WIDTH, BF16), feat, row(B_KW, BF16), row(B_KW, BF16), row(B_VW, BF16)]
    else:
        outs = [row(A_WIDTH, BF16), row(A_WIDTH, F32), row(A_WIDTH, F32), row(B_CONV_CH, F32)]
    logits = (jax.ShapeDtypeStruct((b, AB_ROWS, t), F32), pl.BlockSpec((1, AB_ROWS, tm), lambda i, j: (i, 0, j)))
    outs += [row(B_VW, BF16), logits, row(2 * D_MODEL, BF16)]
    weights = [wq, wk, wv, wcv, wzab, wg]
    scratch = []
    if prompt:
        outs += [stream(tm, A_WIDTH), stream(tm, A_WIDTH), stream(SUBLANES, B_CONV_CH)]
        weights.append(conv_w)
        scratch.append(pltpu.VMEM((SUBLANES, B_CONV_CH), F32))
    in_specs = [pl.BlockSpec((1, tm, d), lambda i, j: (i, j, 0)), _const_spec(gain.shape)]
    in_specs += [_const_spec(w.shape) for w in weights]
    return pl.pallas_call(
        functools.partial(_inproj_kernel, prompt=prompt),
        grid=(b, t // tm),
        in_specs=in_specs,
        out_specs=[o[1] for o in outs],
        out_shape=[o[0] for o in outs],
        scratch_shapes=scratch,
        compiler_params=_params(("arbitrary", "arbitrary")),
        name="inproj_prompt" if prompt else "inproj_sample",
    )(x, gain, *weights)


def _conv_sample_kernel(cv_ref, prev_ref, convw_ref, dq_ref, dk_ref, dv_ref):
    for s in range(cv_ref.shape[0]):
        dq, dk, dv = _conv_silu_l2norm(prev_ref[s], cv_ref[s], convw_ref)
        dq_ref[s] = dq.astype(BF16)
        dk_ref[s] = dk.astype(BF16)
        dv_ref[s] = dv.astype(BF16)


def _conv_sample(cv, prev, conv_w):
    b, t, _ = cv.shape
    blk = lambda r, w: pl.BlockSpec((b, r, w), lambda i: (0, 0, 0))
    out = jax.ShapeDtypeStruct((b, t, B_KW), BF16)
    return pl.pallas_call(
        _conv_sample_kernel,
        grid=(1,),
        in_specs=[blk(t, B_CONV_CH), blk(SUBLANES, B_CONV_CH), _const_spec(conv_w.shape)],
        out_specs=[blk(t, B_KW)] * 3,
        out_shape=[out] * 3,
        compiler_params=_params(("arbitrary",)),
        name="conv_sample",
    )(cv, prev, conv_w)


def _toeplitz_bias(rel_bias, rows, width, offset, sign=1, scale=1.0):
    pitch = width + rows
    period = pitch + 1
    j = np.arange(period)
    rel = sign * np.where(j < width, offset - j, offset + period - j)
    idx = np.clip(rel, -MAX_REL, MAX_REL) + MAX_REL
    vec = rel_bias.astype(F32)[:, idx] * scale
    tiled = jnp.tile(vec, (1, rows))[:, :rows * pitch]
    return tiled.reshape(rel_bias.shape[0], rows, pitch)[:, :, :width]


def _prompt_bias_diagonals(rel_bias):
    j = np.arange(2 * Q_BLOCK)
    d = np.where(j < Q_BLOCK, j, j - 2 * Q_BLOCK)
    rel = d[None, :] + A_REACH - Q_BLOCK * np.arange(N_KEY_BLOCKS)[:, None]
    idx = np.clip(rel, -MAX_REL, MAX_REL) + MAX_REL
    return rel_bias.astype(F32)[:, idx] * LOG2_E


def _fill_prompt_bias(diag_ref, bias_ref):
    kc = lax.broadcasted_iota(jnp.int32, (Q_BLOCK, Q_BLOCK), 0) // CHUNK
    qc = lax.broadcasted_iota(jnp.int32, (Q_BLOCK, Q_BLOCK), 1) // CHUNK
    for jb in range(N_KEY_BLOCKS):
        kcw = kc + jb * (Q_BLOCK // CHUNK)
        band = (kcw >= qc) & (kcw <= qc + A_PREV_CHUNKS)
        for h in range(A_HEADS):
            rows = jnp.broadcast_to(diag_ref[h, jb:jb + 1, :], (Q_BLOCK, 2 * Q_BLOCK))
            rotated = pltpu.roll(rows, 0, 1, stride=1, stride_axis=0)
            bias_ref[h, jb] = jnp.where(band, rotated[:, :Q_BLOCK], -jnp.inf)
    for h in range(A_HEADS):
        bias_ref[h, N_KEY_BLOCKS] = jnp.full((Q_BLOCK, Q_BLOCK), -jnp.inf, F32)


def _softmax_pv(scores, values):
    m = None
    for s in scores:
        mj = jnp.max(s, axis=-1, keepdims=True)
        m = mj if m is None else jnp.maximum(m, mj)
    denom = None
    acc = None
    for s, vals in zip(scores, values):
        p = jnp.exp(s - m)
        dj = jnp.sum(p, axis=-1, keepdims=True)
        denom = dj if denom is None else denom + dj
        pv = _dot(p.astype(BF16), vals)
        acc = pv if acc is None else acc + pv
    return acc / denom


def _attn_prompt_kernel(qt_ref, k_ref, vt_ref, diag_ref, o_ref, bias_ref):
    @pl.when((pl.program_id(0) == 0) & (pl.program_id(1) == 0))
    def _():
        _fill_prompt_bias(diag_ref, bias_ref)

    n_sub = qt_ref.shape[1]
    row = lax.broadcasted_iota(jnp.int32, (LANES, Q_BLOCK), 0)
    low = row < A_HEAD_DIM
    kbs, bsels = {}, {}
    for u in range(n_sub):
        qi = pl.program_id(1) * n_sub + u
        for j in range(N_KEY_BLOCKS):
            kraw = qi - (N_KEY_BLOCKS - 1) + j
            kbs[u, j] = jnp.maximum(kraw, 0)
            bsels[u, j] = jnp.where(kraw >= 0, j, N_KEY_BLOCKS)
    n_qcols = Q_BLOCK // LANES

    def scores(u, h):
        cols = slice((h // 2) * LANES, (h // 2 + 1) * LANES)
        qp = qt_ref[0, u, cols, :]
        qm = jnp.where(low if h % 2 == 0 else jnp.logical_not(low), qp, jnp.zeros_like(qp))
        out = []
        for j in range(N_KEY_BLOCKS):
            kblk = k_ref[0, pl.ds(pl.multiple_of(kbs[u, j] * Q_BLOCK, Q_BLOCK), Q_BLOCK), cols]
            out.append(_dot(kblk, qm) + bias_ref[h, bsels[u, j]])
        return out

    def softmax_pv(u, h, s):
        cols = slice((h // 2) * LANES, (h // 2 + 1) * LANES)
        p_cols = []
        for qc in range(n_qcols):
            ls = slice(qc * LANES, (qc + 1) * LANES)
            lo, hi = qc * LANES, qc * LANES + A_REACH + LANES
            spans = [(max(lo - j * Q_BLOCK, 0), min(hi - j * Q_BLOCK, Q_BLOCK)) for j in range(N_KEY_BLOCKS)]
            parts = [s[j][a:b, ls] for j, (a, b) in enumerate(spans)]
            m = functools.reduce(jnp.maximum, [jnp.max(x, axis=0, keepdims=True) for x in parts])
            ps = [jnp.exp2(x - m) for x in parts]
            padded = []
            for x, (a, b) in zip(ps, spans):
                pieces = ([jnp.zeros((a, LANES), F32)] if a else []) + [x]
                pieces += [jnp.zeros((Q_BLOCK - b, LANES), F32)] if b < Q_BLOCK else []
                padded.append(pieces[0] if len(pieces) == 1 else jnp.concatenate(pieces, axis=0))
            p_cols.append(padded)
        ones = jnp.ones((ONES_ROWS, Q_BLOCK), BF16)
        acc = None
        for j in range(N_KEY_BLOCKS):
            pt = jnp.concatenate([p_cols[qc][j] for qc in range(n_qcols)], axis=1).astype(BF16)
            pv = _dot(jnp.concatenate([vt_ref[0, kbs[u, j], cols, :], ones], axis=0), pt)
            acc = pv if acc is None else acc + pv
        return acc[:LANES] * (1.0 / acc[LANES:LANES + 1])

    tasks = [(u, h) for u in range(n_sub) for h in range(A_HEADS)]
    ready = [scores(*t) for t in tasks[:SCORE_LOOKAHEAD]]
    pair_out = []
    for i, (u, h) in enumerate(tasks):
        if i + SCORE_LOOKAHEAD < len(tasks):
            ready.append(scores(*tasks[i + SCORE_LOOKAHEAD]))
        pair_out.append(softmax_pv(u, h, ready.pop(0)))
        if h % 2 == 1:
            cols = slice((h // 2) * LANES, (h // 2 + 1) * LANES)
            o_pair = jnp.concatenate([pair_out[0][:A_HEAD_DIM], pair_out[1][A_HEAD_DIM:]], axis=0)
            o_ref[0, u * Q_BLOCK:(u + 1) * Q_BLOCK, cols] = o_pair.T.astype(BF16)
            pair_out = []


def _attn_prompt(qt, k, vt, diag):
    b, t, _ = k.shape
    n_sub = min(ATTN_Q_BLOCKS_PER_STEP, t // Q_BLOCK)
    assert t % (n_sub * Q_BLOCK) == 0
    return pl.pallas_call(
        _attn_prompt_kernel,
        grid=(b, t // (n_sub * Q_BLOCK)),
        in_specs=[
            pl.BlockSpec((1, n_sub, A_WIDTH, Q_BLOCK), lambda i, j: (i, j, 0, 0)),
            pl.BlockSpec((1, t, A_WIDTH), lambda i, j: (i, 0, 0)),
            pl.BlockSpec((1, t // Q_BLOCK, A_WIDTH, Q_BLOCK), lambda i, j: (i, 0, 0, 0)),
            _const_spec(diag.shape),
        ],
        out_specs=pl.BlockSpec((1, n_sub * Q_BLOCK, A_WIDTH), lambda i, j: (i, j, 0)),
        out_shape=jax.ShapeDtypeStruct((b, t, A_WIDTH), BF16),
        scratch_shapes=[pltpu.VMEM((A_HEADS, N_KEY_BLOCKS + 1, Q_BLOCK, Q_BLOCK), F32)],
        compiler_params=_params(("arbitrary", "arbitrary")),
        name="attn_prompt",
    )(qt, k, vt, diag)


def _attn_sample_kernel(q_ref, ck_ref, cv_ref, nk_ref, nv_ref, bc_ref, bn_ref, o_ref):
    rows = q_ref.shape[1]
    lane = lax.broadcasted_iota(jnp.int32, (rows, LANES), 1)
    low = lane < A_HEAD_DIM
    pair_cols = [slice(hp * LANES, (hp + 1) * LANES) for hp in range(A_HEADS // 2)]
    tasks = [(s, h) for s in range(q_ref.shape[0]) for h in range(A_HEADS)]
    scores = {}
    for s, h in tasks:
        cols = pair_cols[h // 2]
        qp = q_ref[s, :, cols]
        qm = jnp.where(low if h % 2 == 0 else jnp.logical_not(low), qp, jnp.zeros_like(qp))
        scores[s, h] = [_dot_nt(qm, ck_ref[s, :, cols].astype(BF16)) + bc_ref[h],
                        _dot_nt(qm, nk_ref[s, :, cols].astype(BF16)) + bn_ref[h]]
    outs = {(s, h): _softmax_pv(scores[s, h], [cv_ref[s, :, pair_cols[h // 2]].astype(BF16),
                                               nv_ref[s, :, pair_cols[h // 2]].astype(BF16)]) for s, h in tasks}
    for s in range(q_ref.shape[0]):
        for hp, cols in enumerate(pair_cols):
            o_ref[s, :, cols] = jnp.where(low, outs[s, 2 * hp], outs[s, 2 * hp + 1]).astype(BF16)


def _attn_sample(q, cache_k, cache_v, new_k, new_v, rel_bias):
    b, t, _ = q.shape
    c = cache_k.shape[1]
    bias = _toeplitz_bias(rel_bias, t, c + LANES, c)
    bias = jnp.where((np.arange(c + LANES) < c + t)[None, None, :], bias, -jnp.inf)
    bias_c, bias_n = bias[:, :, :c], bias[:, :, c:]
    pad = ((0, 0), (0, LANES - t), (0, 0))
    nk, nv = jnp.pad(new_k, pad), jnp.pad(new_v, pad)
    ns = min(SAMPLE_ATTN_STREAMS, b)
    assert b % ns == 0
    blk = lambda r: pl.BlockSpec((ns, r, A_WIDTH), lambda i: (i, 0, 0))
    return pl.pallas_call(
        _attn_sample_kernel,
        grid=(b // ns,),
        in_specs=[blk(t), blk(c), blk(c), blk(LANES), blk(LANES),
                  _const_spec(bias_c.shape), _const_spec(bias_n.shape)],
        out_specs=blk(t),
        out_shape=jax.ShapeDtypeStruct((b, t, A_WIDTH), BF16),
        compiler_params=_params(("arbitrary",)),
        name="attn_sample",
    )(q, cache_k, cache_v, nk, nv, bias_c, bias_n)


def _block_diag(x2):
    half = x2.shape[1] // 2
    z = jnp.zeros_like(x2[:, :half])
    top = jnp.concatenate([x2[:, :half], z], axis=1)
    bot = jnp.concatenate([z, x2[:, half:]], axis=1)
    return jnp.concatenate([top, bot], axis=0)


def _quad_diag(x4):
    n = x4.shape[0]
    group = lax.broadcasted_iota(jnp.int32, x4.shape, 1) // n
    zero = jnp.zeros_like(x4)
    return jnp.concatenate([jnp.where(group == g, x4, zero) for g in range(4)], axis=0)


def _unit_lower_inverse(m, eye_half):
    L = m.shape[0]
    h = L // 2
    lane = lax.broadcasted_iota(jnp.int32, (h, 2 * L), 1)
    first_half = (lane & (L - 1)) < h
    left_head = lane < L
    diag = jnp.where(first_half, m[:h], m[h:])
    b16 = jnp.where(first_half, m[h:], 0.0).astype(BF16)
    t = eye_half + diag
    d16 = diag.astype(BF16)
    p = _dot(d16, _quad_diag(d16))
    yield
    n_levels = h.bit_length() - 2
    for lvl in range(n_levels):
        p16 = p.astype(BF16)
        t16 = t.astype(BF16)
        if lvl == n_levels - 1:
            t = t + _dot(t16, _quad_diag(p16))
        else:
            res = _dot(jnp.concatenate([t16, p16], axis=0), _quad_diag(p16))
            t = t + res[:h]
            p = res[h:]
        yield
    t16 = t.astype(BF16)
    zero16 = jnp.zeros_like(t16)
    ta_only = jnp.where(first_half, t16, zero16)
    w_a = jnp.concatenate([jnp.where(left_head, ta_only, zero16), zero16,
                           jnp.where(left_head, zero16, ta_only), zero16], axis=0)
    y16 = _dot(b16, w_a).astype(BF16)
    yield
    w_y = jnp.concatenate([zero16, jnp.where(left_head, y16, zero16),
                           zero16, jnp.where(left_head, zero16, y16)], axis=0)
    t_b = _dot(t16, w_y)
    return jnp.where(first_half, t, 0.0), jnp.where(first_half, t_b, t)


def _round_robin(generators):
    results = [None] * len(generators)
    active = list(enumerate(generators))
    while active:
        still = []
        for i, g in active:
            try:
                next(g)
                still.append((i, g))
            except StopIteration as stop:
                results[i] = stop.value
        active = still
    return results


def _delta_kernel(q_ref, k_ref, v_ref, z_ref, ab_ref, s0_ref, alog_ref, dtb_ref, nw_ref,
                  o_ref, s_ref, rows_ref, *, n_valid):
    L = DELTA_CHUNK
    ns, tb = q_ref.shape[0], q_ref.shape[1]
    nch = tb // L
    n_pairs = B_HEADS // 2

    @pl.when(pl.program_id(1) == 0)
    def _():
        s_ref[...] = s0_ref[...]

    chains = [(s, c, p) for s in range(ns) for c in range(nch) for p in range(n_pairs)]

    st = {}
    for s, c, p in chains:
        r = slice(c * L, (c + 1) * L)
        ls = slice(p * 2 * L, (p + 1) * 2 * L)
        q2 = q_ref[s, r, ls]
        k2 = k_ref[s, r, ls]
        sc = _dot_nt(jnp.concatenate([q2, k2], axis=0), _block_diag(k2))
        st[s, c, p] = dict(q2=q2.astype(F32), k2=k2.astype(F32), qk=sc[:L], kk=sc[L:])

    scal = _round_robin([_delta_scalars(ab_ref, alog_ref, dtb_ref, rows_ref, s, n_valid) for s in range(ns)])
    _delta_chunks(st, chains, scal, v_ref, z_ref, nw_ref, o_ref, s_ref)


def _delta_scalars(ab_ref, alog_ref, dtb_ref, rows_ref, s, n_valid):
    L = DELTA_CHUNK
    tb = ab_ref.shape[2]
    nch = tb // L
    rows8 = ab_ref[s, 0:2 * B_HEADS, :]
    x = rows8 + dtb_ref[:, 0:1]
    softplus = jnp.maximum(x, 0.0) + jnp.log(1.0 + jnp.exp(-jnp.abs(x)))
    g8 = -jnp.exp(alog_ref[:, 0:1]) * softplus
    beta8 = _sigmoid(rows8)
    tok = lax.broadcasted_iota(jnp.int32, (2 * B_HEADS, tb), 1)
    if n_valid < tb:
        g8 = jnp.where(tok < n_valid, g8, 0.0)
        beta8 = jnp.where(tok < n_valid, beta8, 0.0)
    pos = tok & (L - 1)
    gc8 = g8
    shift = 1
    while shift < L:
        yield
        gc8 = gc8 + jnp.where(pos >= shift, pltpu.roll(gc8, shift, 1), 0.0)
        shift *= 2
    gl8 = [gc8[:, (c + 1) * L - 1:(c + 1) * L] for c in range(nch)]
    glb8 = jnp.concatenate([jnp.broadcast_to(g, (2 * B_HEADS, L)) for g in gl8], axis=1)
    rows_ref[s] = jnp.zeros(rows_ref.shape[1:], F32)
    rows_ref[s, 0:8, :] = gc8
    rows_ref[s, 8:16, :] = beta8
    rows_ref[s, 16:24, :] = jnp.exp(gc8)
    rows_ref[s, 24:32, :] = jnp.exp(glb8 - gc8)
    cols = rows_ref[s].T
    return gc8, gl8, cols


def _delta_chunks(st, chains, scal, v_ref, z_ref, nw_ref, o_ref, s_ref):
    L = DELTA_CHUNK
    ns, tb = v_ref.shape[0], v_ref.shape[1]
    nch = tb // L
    n_pairs = B_HEADS // 2
    ri = lax.broadcasted_iota(jnp.int32, (L, 2 * L), 0)
    ci = lax.broadcasted_iota(jnp.int32, (L, 2 * L), 1) & (L - 1)
    tril = ri >= ci
    strict = ri > ci
    rh = lax.broadcasted_iota(jnp.int32, (L // 2, 2 * L), 0)
    ch = lax.broadcasted_iota(jnp.int32, (L // 2, 2 * L), 1) & (L // 2 - 1)
    eye_half = (rh == ch).astype(F32)

    def chain_ops(s, c, p):
        gc8, _, cols = scal[s]
        r = slice(c * L, (c + 1) * L)
        ls = slice(p * 2 * L, (p + 1) * 2 * L)
        h0 = 2 * p
        d = st[s, c, p]
        pair_cols = lambda i: [jnp.broadcast_to(cols[r, i + e:i + e + 1], (L, L)) for e in range(2)]
        gc_c = jnp.concatenate(pair_cols(h0), axis=1)
        beta = pair_cols(8 + B_HEADS + h0)
        egc = pair_cols(16 + h0)
        edec = pair_cols(24 + h0)
        gc_r = jnp.concatenate([jnp.broadcast_to(gc8[h0 + e:h0 + e + 1, r], (L, L)) for e in range(2)], axis=1)
        decay = jnp.exp(jnp.where(tril, gc_c - gc_r, -jnp.inf))
        m = jnp.where(strict, -(jnp.concatenate(beta, axis=1) * d.pop("kk") * decay), 0.0)
        qkd16 = (d.pop("qk") * decay).astype(BF16)
        v2 = v_ref[s, r, ls].astype(F32)
        yield
        top, bot = yield from _unit_lower_inverse(m, eye_half)
        yield
        t16 = jnp.concatenate([top, bot], axis=0).astype(BF16)
        sols = []
        for e in range(2):
            hs = slice(e * L, (e + 1) * L)
            kh = d["k2"][:, hs]
            rhs = jnp.concatenate([v2[:, hs] * beta[e], kh * (beta[e] * egc[e])], axis=1)
            sols.append(_dot(t16[:, hs], rhs.astype(BF16)).astype(BF16))
        yield
        phi, psi, qt, oin = [], [], [], []
        for e in range(2):
            hs = slice(e * L, (e + 1) * L)
            kdt = (d["k2"][:, hs] * edec[e]).T.astype(BF16)
            r2 = _dot(jnp.concatenate([qkd16[:, hs], kdt], axis=0), sols[e])
            oin.append(r2[:L, :B_DV])
            qt.append(d["q2"][:, hs] * egc[e] - r2[:L, B_DV:])
            psi.append(r2[L:, :B_DV])
            phi.append(-r2[L:, B_DV:])
        d["lhs"] = jnp.concatenate([jnp.concatenate(phi, axis=1), jnp.concatenate(qt, axis=1)], axis=0).astype(BF16)
        d["psi"] = jnp.concatenate(psi, axis=1)
        d["oin"] = jnp.concatenate(oin, axis=1)

    lanes = [slice(p * 2 * L, (p + 1) * 2 * L) for p in range(n_pairs)]
    carried = [(s, p) for s in range(ns) for p in range(n_pairs)]
    s2 = {(s, p): s_ref[s, :, lanes[p]] for s, p in carried}
    nw = nw_ref[...]
    pending = []

    def emit(c, o2):
        r = slice(c * L, (c + 1) * L)
        for (s, p), o in o2.items():
            for e in range(2):
                hs = slice((2 * p + e) * B_DV, (2 * p + e + 1) * B_DV)
                zh = z_ref[s, r, hs].astype(F32)
                o_ref[s, r, hs] = (_rms(o[:, e * B_DV:(e + 1) * B_DV], nw) * _silu_of_twice(0.5 * zh)).astype(BF16)

    def carry_state(c):
        o2 = {}
        for s, p in carried:
            d = st[s, c, p]
            gl = scal[s][1][c]
            res = _dot(d["lhs"], _block_diag(s2[s, p].astype(BF16)))
            o2[s, p] = res[L:] + d["oin"]
            egl = jnp.concatenate([jnp.broadcast_to(jnp.exp(gl[2 * p + e:2 * p + e + 1, :]), (1, B_DV))
                                   for e in range(2)], axis=1)
            s2[s, p] = s2[s, p] * egl + res[:L] + d["psi"]
        if pending:
            emit(*pending.pop())
        pending.append((c, o2))

    waiting = [(key, chain_ops(*key)) for key in sorted(chains, key=lambda k: (k[1], k[0], k[2]))]
    live, finished, next_chunk = [], [0] * nch, 0
    while waiting or live:
        live += waiting[:CHAIN_STARTS_PER_ROUND]
        waiting = waiting[CHAIN_STARTS_PER_ROUND:]
        still = []
        for key, g in live:
            try:
                next(g)
                still.append((key, g))
            except StopIteration:
                finished[key[1]] += 1
        live = still
        while next_chunk < nch and finished[next_chunk] == len(carried):
            carry_state(next_chunk)
            next_chunk += 1
    emit(*pending.pop())
    for s, p in carried:
        s_ref[s, :, lanes[p]] = s2[s, p]


def _delta(q, k, v, z, ab, s0, a_log, dt_bias, norm_w, *, block, streams, n_valid):
    b, t, _ = q.shape
    streams = min(streams, b)
    assert t % block == 0 and block % DELTA_CHUNK == 0 and b % streams == 0
    alog8 = jnp.broadcast_to(jnp.tile(a_log.astype(F32), 2)[:, None], (2 * B_HEADS, LANES))
    dtb8 = jnp.broadcast_to(jnp.tile(dt_bias.astype(F32), 2)[:, None], (2 * B_HEADS, LANES))
    row = lambda w: pl.BlockSpec((streams, block, w), lambda i, j: (i, j, 0))
    state = pl.BlockSpec((streams, B_DK, B_HEADS * B_DV), lambda i, j: (i, 0, 0))
    s0 = s0.astype(F32).transpose(0, 2, 1, 3).reshape(b, B_DK, B_HEADS * B_DV)
    o, s = pl.pallas_call(
        functools.partial(_delta_kernel, n_valid=n_valid),
        grid=(b // streams, t // block),
        in_specs=[row(B_KW), row(B_KW), row(B_VW), row(B_VW),
                  pl.BlockSpec((streams, ab.shape[1], block), lambda i, j: (i, 0, j)), state,
                  _const_spec(alog8.shape), _const_spec(dtb8.shape), _const_spec((1, B_DV))],
        out_specs=[row(B_VW), state],
        out_shape=[jax.ShapeDtypeStruct((b, t, B_VW), BF16),
                   jax.ShapeDtypeStruct((b, B_DK, B_HEADS * B_DV), F32)],
        scratch_shapes=[pltpu.VMEM((streams, LANES, block), F32)],
        compiler_params=_params(("arbitrary", "arbitrary")),
        name="delta_%d" % block,
    )(q, k, v, z, ab, s0, alog8, dtb8, norm_w.reshape(1, B_DV).astype(F32))
    return o, s.reshape(b, B_DK, B_HEADS, B_DV).transpose(0, 2, 1, 3)


def _merge_kernel(x_ref, oa_ref, ob_ref, g_ref, wpa_ref, wpb_ref, wout_ref, gain_ref, o_ref):
    tm = x_ref.shape[0]
    sub = min(ROW_TILE, tm)
    blocks = [slice(r, r + sub) for r in range(0, tm, sub)]

    def branches(rs):
        return _dot(oa_ref[rs, :], wpa_ref[...]), _dot(ob_ref[rs, :], wpb_ref[...])

    nxt = branches(blocks[0])
    for i, rs in enumerate(blocks):
        ya, yb = nxt
        if i + 1 < len(blocks):
            nxt = branches(blocks[i + 1])
        merged = g_ref[rs, :D_MODEL].astype(F32) * ya + g_ref[rs, D_MODEL:].astype(F32) * yb
        y = _dot(merged.astype(BF16), wout_ref[...])
        o_ref[rs, :] = x_ref[rs, :] + _rms(y, gain_ref[...])


def _merge(x, oa, ob, gates, wpa, wpb, wout, gain):
    n, d = x.shape
    tm = min(MERGE_ROWS, n)
    row = lambda w: pl.BlockSpec((tm, w), lambda i: (i, 0))
    return pl.pallas_call(
        _merge_kernel,
        grid=(n // tm,),
        in_specs=[row(d), row(A_WIDTH), row(B_VW), row(2 * d),
                  _const_spec(wpa.shape), _const_spec(wpb.shape), _const_spec(wout.shape), _const_spec(gain.shape)],
        out_specs=row(d),
        out_shape=jax.ShapeDtypeStruct((n, d), F32),
        compiler_params=_params(("arbitrary",)),
        name="merge_%d" % n,
    )(x, oa, ob, gates, wpa, wpb, wout, gain)


FF_SPLIT = 4


def _mlp_kernel(x_ref, gpre_ref, wup_ref, wdown_ref, gpost_ref, o_ref):
    tm = x_ref.shape[0]
    sub = min(ROW_TILE, tm)
    blocks = [slice(r, r + sub) for r in range(0, tm, sub)]
    groups = [slice(c, c + D_FF // FF_SPLIT) for c in range(0, D_FF, D_FF // FF_SPLIT)]
    hbs = [_rms(x_ref[rs, :], gpre_ref[...]).astype(BF16) for rs in blocks]
    finish = None
    for rs, hb in zip(blocks, hbs):
        f = None
        for c, cs in enumerate(groups):
            up = jnp.maximum(_dot(hb, wup_ref[:, cs]), 0.0)
            if c == 0 and finish is not None:
                finish()
            part = _dot((up * up).astype(BF16), wdown_ref[cs, :])
            f = part if f is None else f + part

        def finish(rs=rs, f=f):
            o_ref[rs, :] = x_ref[rs, :] + _rms(f, gpost_ref[...])
    finish()


def _mlp(x, gpre, wup, wdown, gpost):
    n, d = x.shape
    tm = min(MLP_ROWS, n)
    row = pl.BlockSpec((tm, d), lambda i: (i, 0))
    return pl.pallas_call(
        _mlp_kernel,
        grid=(n // tm,),
        in_specs=[row, _const_spec(gpre.shape), _const_spec(wup.shape), _const_spec(wdown.shape),
                  _const_spec(gpost.shape)],
        out_specs=row,
        out_shape=jax.ShapeDtypeStruct((n, d), F32),
        compiler_params=_params(("arbitrary",)),
        name="mlp_%d" % n,
    )(x, gpre, wup, wdown, gpost)


def _layer_weights(norm_mix_pre, w_in, conv_w, w_proj_a, w_proj_b, w_out, norm_mix_post,
                   norm_mlp_pre, w_up, w_down, norm_mlp_post):
    o_cv = 3 * A_WIDTH
    o_z = o_cv + B_CONV_CH
    o_ab = o_z + B_VW
    o_g = o_ab + 2 * B_HEADS
    wb = w_in.astype(BF16)
    wzab = jnp.concatenate([wb[:, o_z:o_g], jnp.zeros((D_MODEL, LANES - 2 * B_HEADS), BF16)], axis=1)
    vec = lambda g: g.reshape(1, -1).astype(F32)
    return dict(
        gain_in=vec(norm_mix_pre),
        wq=wb[:, :A_WIDTH], wk=wb[:, A_WIDTH:2 * A_WIDTH], wv=wb[:, 2 * A_WIDTH:o_cv],
        wcv=wb[:, o_cv:o_z], wzab=wzab, wg=wb[:, o_g:] * 0.5,
        conv_w=conv_w.astype(F32),
        wpa=w_proj_a.astype(BF16), wpb=w_proj_b.astype(BF16), wout=w_out.astype(BF16),
        gain_mix=vec(norm_mix_post), gain_pre=vec(norm_mlp_pre), gain_post=vec(norm_mlp_post),
        wup=w_up.astype(BF16), wdown=w_down.astype(BF16),
    )


def _tail(x, w, oa, ob, gates):
    b, t, d = x.shape
    x1 = _merge(x.reshape(b * t, d), oa.reshape(b * t, A_WIDTH), ob.reshape(b * t, B_VW),
                gates.reshape(b * t, 2 * d), w["wpa"], w["wpb"], w["wout"], w["gain_mix"])
    return _mlp(x1, w["gain_pre"], w["wup"], w["wdown"], w["gain_post"]).reshape(b, t, d)


def _prompt_layer(x, w, rel_bias, a_log, dt_bias, delta_norm_w):
    b, t, _ = x.shape
    qt, k, vt, dq, dk, dv, z, ab, gates, kf, vf, cvt = _inproj(
        x, w["gain_in"], w["wq"], w["wk"], w["wv"], w["wcv"], w["wzab"], w["wg"], w["conv_w"], prompt=True)
    oa = _attn_prompt(qt, k, vt, _prompt_bias_diagonals(rel_bias))
    s0 = jnp.zeros((b, B_HEADS, B_DK, B_DV), F32)
    ob, s_new = _delta(dq, dk, dv, z, ab, s0, a_log, dt_bias, delta_norm_w,
                       block=DELTA_BLOCK, streams=DELTA_STREAMS, n_valid=DELTA_BLOCK)
    y = _tail(x, w, oa, ob, gates)
    keep = min(A_REACH, t)
    new_k = kf.reshape(b, keep, A_HEADS, A_HEAD_DIM)
    new_v = vf.reshape(b, keep, A_HEADS, A_HEAD_DIM)
    new_conv = cvt[:, SUBLANES - (CONV_WIDTH - 1):, :]
    return y, new_k, new_v, new_conv, s_new


def _sample_layer(x, cache_k, cache_v, conv_state, delta_state, w, rel_bias, a_log, dt_bias, delta_norm_w):
    b, t, d = x.shape
    c = cache_k.shape[1]
    q, k, v, cv, z, ab, gates = _inproj(
        x.reshape(1, b * t, d), w["gain_in"], w["wq"], w["wk"], w["wv"], w["wcv"], w["wzab"], w["wg"],
        prompt=False)
    per_stream = lambda a: a.reshape(b, t, a.shape[-1])
    q, k, v, cv, z, gates = map(per_stream, (q, k, v, cv, z, gates))
    ab = ab.reshape(AB_ROWS, b, t).transpose(1, 0, 2)
    oa = _attn_sample(q, cache_k.reshape(b, c, A_WIDTH), cache_v.reshape(b, c, A_WIDTH), k, v, rel_bias)
    prev = jnp.pad(conv_state.astype(F32), ((0, 0), (SUBLANES - (CONV_WIDTH - 1), 0), (0, 0)))
    dq, dk, dv = _conv_sample(cv, prev, w["conv_w"])
    pad = lambda a: jnp.pad(a, ((0, 0), (0, DELTA_CHUNK - t), (0, 0)))
    ab = jnp.pad(ab, ((0, 0), (0, 0), (0, DELTA_CHUNK - t)))
    ob, s_new = _delta(pad(dq), pad(dk), pad(dv), pad(z), ab, delta_state, a_log, dt_bias, delta_norm_w,
                       block=DELTA_CHUNK, streams=SAMPLE_DELTA_STREAMS, n_valid=t)
    y = _tail(x, w, oa, ob[:, :t], gates)
    new_conv = jnp.concatenate([conv_state.astype(F32), cv], axis=1)[:, -(CONV_WIDTH - 1):]
    return (y, k.reshape(b, t, A_HEADS, A_HEAD_DIM), v.reshape(b, t, A_HEADS, A_HEAD_DIM), new_conv, s_new)


def kernel(x_prompt, x_sample, cache_attn_k, cache_attn_v, state_conv, state_delta, norm_mix_pre, w_in,
           rel_bias, conv_w, a_log, dt_bias, delta_norm_w, w_proj_a, w_proj_b, w_out, norm_mix_post,
           norm_mlp_pre, w_up, w_down, norm_mlp_post):
    depth = w_in.shape[0]
    yp, ys = x_prompt, x_sample
    outs = [[] for _ in range(8)]
    for l in range(depth):
        w = _layer_weights(norm_mix_pre[l], w_in[l], conv_w[l], w_proj_a[l], w_proj_b[l], w_out[l],
                           norm_mix_post[l], norm_mlp_pre[l], w_up[l], w_down[l], norm_mlp_post[l])
        yp, k1, v1, c1, s1 = _prompt_layer(yp, w, rel_bias[l], a_log[l], dt_bias[l], delta_norm_w[l])
        ys, k2, v2, c2, s2 = _sample_layer(ys, cache_attn_k[l], cache_attn_v[l], state_conv[l], state_delta[l],
                                           w, rel_bias[l], a_log[l], dt_bias[l], delta_norm_w[l])
        for lst, val in zip(outs, (k1, v1, c1, s1, k2, v2, c2, s2)):
            lst.append(val)
    return (yp, ys) + tuple(jnp.stack(o) for o in outs)
```

```python
import functools

import numpy as np
import jax
import jax.numpy as jnp
from jax import lax
from jax.experimental import pallas as pl
from jax.experimental.pallas import tpu as pltpu

F32 = jnp.float32
BF16 = jnp.bfloat16

D_MODEL = 1024
CHUNK = 64
RMS_EPS = 1e-6
L2_EPS = 1e-6
A_HEADS = 8
A_HEAD_DIM = 64
A_WIDTH = A_HEADS * A_HEAD_DIM
A_PREV_CHUNKS = 8
A_REACH = A_PREV_CHUNKS * CHUNK
MAX_REL = 128
B_HEADS = 4
B_DK = 128
B_DV = 128
B_KW = B_HEADS * B_DK
B_VW = B_HEADS * B_DV
B_CONV_CH = 2 * B_KW + B_VW
CONV_WIDTH = 4
D_FF = 4 * D_MODEL

LANES = 128
SUBLANES = 8
VMEM_LIMIT_BYTES = 56 * 1024 * 1024

Q_BLOCK = 256
N_KEY_BLOCKS = A_REACH // Q_BLOCK + 1
SCORE_LOOKAHEAD = 2
ATTN_Q_BLOCKS_PER_STEP = 8
ONES_ROWS = 16
LOG2_E = 1.4426950408889634
DELTA_CHUNK = 128
DELTA_BLOCK = 512
DELTA_STREAMS = 4
SAMPLE_DELTA_STREAMS = 8
SAMPLE_ATTN_STREAMS = 2
CHAIN_STARTS_PER_ROUND = 2
ROW_TILE = 512
CONV_ROWS = 512
GATE_COLS = 512
AB_ROWS = 16
MERGE_ROWS = 1024
MLP_ROWS = 1024

assert B_DK == LANES and B_DV == LANES and B_HEADS % 2 == 0


def _dot(a, b):
    return jnp.dot(a, b, preferred_element_type=F32)


def _dot_nt(a, b):
    return lax.dot_general(a, b, (((1,), (1,)), ((), ())), preferred_element_type=F32)


def _sigmoid_of_twice(h):
    return 0.5 * jnp.tanh(h) + 0.5


def _sigmoid(x):
    return _sigmoid_of_twice(0.5 * x)


def _silu_of_twice(h):
    return h * jnp.tanh(h) + h


def _rms(x, gain):
    ms = jnp.mean(x * x, axis=-1, keepdims=True)
    return x * lax.rsqrt(ms + RMS_EPS) * gain


def _params(sem):
    return pltpu.CompilerParams(dimension_semantics=sem, vmem_limit_bytes=VMEM_LIMIT_BYTES)


def _const_spec(shape):
    nd = len(shape)
    return pl.BlockSpec(shape, lambda *_: (0,) * nd, pipeline_mode=pl.Buffered(1))


def _conv_silu(hist, x, w):
    n = x.shape[0]
    full = jnp.concatenate([hist, x], axis=0)
    rb = min(CONV_ROWS, n)
    half_w = 0.5 * w
    out = []
    for r0 in range(0, n, rb):
        piece = full[r0:r0 + rb + SUBLANES, :]
        z = piece * half_w[0:1, :]
        for i in range(1, CONV_WIDTH):
            z = pltpu.roll(z, 1, 0) + piece * half_w[i:i + 1, :]
        out.append(_silu_of_twice(z[SUBLANES:, :]))
    return out[0] if len(out) == 1 else jnp.concatenate(out, axis=0)


def _l2norm_heads(y, scale):
    heads = []
    for h in range(y.shape[1] // B_DK):
        yh = y[:, h * B_DK:(h + 1) * B_DK]
        heads.append(yh * (lax.rsqrt(jnp.sum(yh * yh, axis=-1, keepdims=True) + L2_EPS) * scale))
    return jnp.concatenate(heads, axis=1)


def _conv_silu_l2norm(hist, x, convw_ref):
    y = _conv_silu(hist, x, convw_ref[...])
    return (_l2norm_heads(y[:, :B_KW], B_DK ** -0.5), _l2norm_heads(y[:, B_KW:2 * B_KW], 1.0), y[:, 2 * B_KW:])


def _inproj_kernel(*refs, prompt):
    x_ref, gain_ref, wq_ref, wk_ref, wv_ref, wcv_ref, wzab_ref, wg_ref = refs[:8]
    if prompt:
        (convw_ref, q_ref, k_ref, vt_ref, dq_ref, dk_ref, dv_ref, z_ref, ab_ref, g_ref,
         kf_ref, vf_ref, cvt_ref, hist_ref) = refs[8:]
    else:
        q_ref, kf_ref, vf_ref, cv_ref, z_ref, ab_ref, g_ref = refs[8:]
    if prompt:
        @pl.when(pl.program_id(1) == 0)
        def _():
            hist_ref[...] = jnp.zeros((SUBLANES, B_CONV_CH), F32)

    x = x_ref[0]
    hb = _rms(x, gain_ref[...]).astype(BF16)
    tm = x.shape[0]

    stages = []

    def stage(issue):
        return lambda consume: stages.append((issue, consume))

    if prompt:
        norms = (B_DK ** -0.5, 1.0, None)
        for part, out_ref in enumerate((dq_ref, dk_ref, dv_ref)):
            cs = slice(part * B_KW, (part + 1) * B_KW)

            @stage(lambda cs=cs: _dot(hb, wcv_ref[:, cs]))
            def _(cv, cs=cs, out_ref=out_ref, scale=norms[part]):
                y = _conv_silu(hist_ref[:, cs], cv, convw_ref[:, cs])
                out_ref[0] = (y if scale is None else _l2norm_heads(y, scale)).astype(BF16)
                hist_ref[:, cs] = cv[tm - SUBLANES:, :]
                cvt_ref[0, :, cs] = cv[tm - SUBLANES:, :]
    else:
        @stage(lambda: _dot(hb, wcv_ref[...]))
        def _(cv):
            cv_ref[0] = cv

    def feature_major(val, out_ref):
        vt = val.T.astype(BF16)
        for j in range(tm // Q_BLOCK):
            out_ref[0, j] = vt[:, j * Q_BLOCK:(j + 1) * Q_BLOCK]

    @stage(lambda: _dot(hb, wq_ref[...]))
    def _(q):
        if prompt:
            feature_major(q * (A_HEAD_DIM ** -0.5 * LOG2_E), q_ref)
        else:
            q_ref[0] = (q * (A_HEAD_DIM ** -0.5)).astype(BF16)

    @stage(lambda: _dot(hb, wk_ref[...]))
    def _(k):
        kf_ref[0] = k
        if prompt:
            k_ref[0] = k.astype(BF16)

    @stage(lambda: _dot(hb, wv_ref[...]))
    def _(v):
        vf_ref[0] = v
        if prompt:
            feature_major(v, vt_ref)

    @stage(lambda: _dot(hb, wzab_ref[...]))
    def _(zab):
        z_ref[0] = zab[:, :B_VW].astype(BF16)
        ab_ref[0] = zab[:, B_VW:].T[0:AB_ROWS, :]

    for c0 in range(0, 2 * D_MODEL, GATE_COLS):
        cs = slice(c0, c0 + GATE_COLS)

        @stage(lambda cs=cs: _dot(hb, wg_ref[:, cs]))
        def _(g, cs=cs):
            g_ref[0, :, cs] = _sigmoid_of_twice(g).astype(BF16)

    pending = None
    for issue, consume in stages:
        val = issue()
        if pending is not None:
            pending[1](pending[0])
        pending = (val, consume)
    pending[1](pending[0])


def _inproj(x, gain, wq, wk, wv, wcv, wzab, wg, conv_w=None, *, prompt):
    b, t, d = x.shape
    tm = min(ROW_TILE, t)
    assert t % tm == 0
    row = lambda w, dt: (jax.ShapeDtypeStruct((b, t, w), dt), pl.BlockSpec((1, tm, w), lambda i, j: (i, j, 0)))
    stream = lambda r, w: (jax.ShapeDtypeStruct((b, r, w), F32), pl.BlockSpec((1, r, w), lambda i, j: (i, 0, 0)))
    if prompt:
        assert tm == A_REACH and tm % Q_BLOCK == 0
        nq = tm // Q_BLOCK
        feat = (jax.ShapeDtypeStruct((b, t // Q_BLOCK, A_WIDTH, Q_BLOCK), BF16),
                pl.BlockSpec((1, nq, A_WIDTH, Q_BLOCK), lambda i, j: (i, j, 0, 0)))
        outs = [feat, row(A_WIDTH, BF16), feat, row(B_KW, BF16), row(B_KW, BF16), row(B_VW, BF16)]
    else:
        outs = [row(A_WIDTH, BF16), row(A_WIDTH, F32), row(A_WIDTH, F32), row(B_CONV_CH, F32)]
    logits = (jax.ShapeDtypeStruct((b, AB_ROWS, t), F32), pl.BlockSpec((1, AB_ROWS, tm), lambda i, j: (i, 0, j)))
    outs += [row(B_VW, BF16), logits, row(2 * D_MODEL, BF16)]
    weights = [wq, wk, wv, wcv, wzab, wg]
    scratch = []
    if prompt:
        outs += [stream(tm, A_WIDTH), stream(tm, A_WIDTH), stream(SUBLANES, B_CONV_CH)]
        weights.append(conv_w)
        scratch.append(pltpu.VMEM((SUBLANES, B_CONV_CH), F32))
    in_specs = [pl.BlockSpec((1, tm, d), lambda i, j: (i, j, 0)), _const_spec(gain.shape)]
    in_specs += [_const_spec(w.shape) for w in weights]
    return pl.pallas_call(
        functools.partial(_inproj_kernel, prompt=prompt),
        grid=(b, t // tm),
        in_specs=in_specs,
        out_specs=[o[1] for o in outs],
        out_shape=[o[0] for o in outs],
        scratch_shapes=scratch,
        compiler_params=_params(("arbitrary", "arbitrary")),
        name="inproj_prompt" if prompt else "inproj_sample",
    )(x, gain, *weights)


def _conv_sample_kernel(cv_ref, prev_ref, convw_ref, dq_ref, dk_ref, dv_ref):
    for s in range(cv_ref.shape[0]):
        dq, dk, dv = _conv_silu_l2norm(prev_ref[s], cv_ref[s], convw_ref)
        dq_ref[s] = dq.astype(BF16)
        dk_ref[s] = dk.astype(BF16)
        dv_ref[s] = dv.astype(BF16)


def _conv_sample(cv, prev, conv_w):
    b, t, _ = cv.shape
    blk = lambda r, w: pl.BlockSpec((b, r, w), lambda i: (0, 0, 0))
    out = jax.ShapeDtypeStruct((b, t, B_KW), BF16)
    return pl.pallas_call(
        _conv_sample_kernel,
        grid=(1,),
        in_specs=[blk(t, B_CONV_CH), blk(SUBLANES, B_CONV_CH), _const_spec(conv_w.shape)],
        out_specs=[blk(t, B_KW)] * 3,
        out_shape=[out] * 3,
        compiler_params=_params(("arbitrary",)),
        name="conv_sample",
    )(cv, prev, conv_w)


def _toeplitz_bias(rel_bias, rows, width, offset, sign=1, scale=1.0):
    pitch = width + rows
    period = pitch + 1
    j = np.arange(period)
    rel = sign * np.where(j < width, offset - j, offset + period - j)
    idx = np.clip(rel, -MAX_REL, MAX_REL) + MAX_REL
    vec = rel_bias.astype(F32)[:, idx] * scale
    tiled = jnp.tile(vec, (1, rows))[:, :rows * pitch]
    return tiled.reshape(rel_bias.shape[0], rows, pitch)[:, :, :width]


def _prompt_bias_diagonals(rel_bias):
    j = np.arange(2 * Q_BLOCK)
    d = np.where(j < Q_BLOCK, j, j - 2 * Q_BLOCK)
    rel = d[None, :] + A_REACH - Q_BLOCK * np.arange(N_KEY_BLOCKS)[:, None]
    idx = np.clip(rel, -MAX_REL, MAX_REL) + MAX_REL
    return rel_bias.astype(F32)[:, idx] * LOG2_E


def _fill_prompt_bias(diag_ref, bias_ref):
    kc = lax.broadcasted_iota(jnp.int32, (Q_BLOCK, Q_BLOCK), 0) // CHUNK
    qc = lax.broadcasted_iota(jnp.int32, (Q_BLOCK, Q_BLOCK), 1) // CHUNK
    for jb in range(N_KEY_BLOCKS):
        kcw = kc + jb * (Q_BLOCK // CHUNK)
        band = (kcw >= qc) & (kcw <= qc + A_PREV_CHUNKS)
        for h in range(A_HEADS):
            rows = jnp.broadcast_to(diag_ref[h, jb:jb + 1, :], (Q_BLOCK, 2 * Q_BLOCK))
            rotated = pltpu.roll(rows, 0, 1, stride=1, stride_axis=0)
            bias_ref[h, jb] = jnp.where(band, rotated[:, :Q_BLOCK], -jnp.inf)
    for h in range(A_HEADS):
        bias_ref[h, N_KEY_BLOCKS] = jnp.full((Q_BLOCK, Q_BLOCK), -jnp.inf, F32)


def _softmax_pv(scores, values):
    m = None
    for s in scores:
        mj = jnp.max(s, axis=-1, keepdims=True)
        m = mj if m is None else jnp.maximum(m, mj)
    denom = None
    acc = None
    for s, vals in zip(scores, values):
        p = jnp.exp(s - m)
        dj = jnp.sum(p, axis=-1, keepdims=True)
        denom = dj if denom is None else denom + dj
        pv = _dot(p.astype(BF16), vals)
        acc = pv if acc is None else acc + pv
    return acc / denom


def _attn_prompt_kernel(qt_ref, k_ref, vt_ref, diag_ref, o_ref, bias_ref):
    @pl.when((pl.program_id(0) == 0) & (pl.program_id(1) == 0))
    def _():
        _fill_prompt_bias(diag_ref, bias_ref)

    n_sub = qt_ref.shape[1]
    row = lax.broadcasted_iota(jnp.int32, (LANES, Q_BLOCK), 0)
    low = row < A_HEAD_DIM
    kbs, bsels = {}, {}
    for u in range(n_sub):
        qi = pl.program_id(1) * n_sub + u
        for j in range(N_KEY_BLOCKS):
            kraw = qi - (N_KEY_BLOCKS - 1) + j
            kbs[u, j] = jnp.maximum(kraw, 0)
            bsels[u, j] = jnp.where(kraw >= 0, j, N_KEY_BLOCKS)
    n_qcols = Q_BLOCK // LANES

    def scores(u, h):
        cols = slice((h // 2) * LANES, (h // 2 + 1) * LANES)
        qp = qt_ref[0, u, cols, :]
        qm = jnp.where(low if h % 2 == 0 else jnp.logical_not(low), qp, jnp.zeros_like(qp))
        out = []
        for j in range(N_KEY_BLOCKS):
            kblk = k_ref[0, pl.ds(pl.multiple_of(kbs[u, j] * Q_BLOCK, Q_BLOCK), Q_BLOCK), cols]
            out.append(_dot(kblk, qm) + bias_ref[h, bsels[u, j]])
        return out

    def softmax_pv(u, h, s):
        cols = slice((h // 2) * LANES, (h // 2 + 1) * LANES)
        p_cols = []
        for qc in range(n_qcols):
            ls = slice(qc * LANES, (qc + 1) * LANES)
            lo, hi = qc * LANES, qc * LANES + A_REACH + LANES
            spans = [(max(lo - j * Q_BLOCK, 0), min(hi - j * Q_BLOCK, Q_BLOCK)) for j in range(N_KEY_BLOCKS)]
            parts = [s[j][a:b, ls] for j, (a, b) in enumerate(spans)]
            m = functools.reduce(jnp.maximum, [jnp.max(x, axis=0, keepdims=True) for x in parts])
            ps = [jnp.exp2(x - m) for x in parts]
            padded = []
            for x, (a, b) in zip(ps, spans):
                pieces = ([jnp.zeros((a, LANES), F32)] if a else []) + [x]
                pieces += [jnp.zeros((Q_BLOCK - b, LANES), F32)] if b < Q_BLOCK else []
                padded.append(pieces[0] if len(pieces) == 1 else jnp.concatenate(pieces, axis=0))
            p_cols.append(padded)
        ones = jnp.ones((ONES_ROWS, Q_BLOCK), BF16)
        acc = None
        for j in range(N_KEY_BLOCKS):
            pt = jnp.concatenate([p_cols[qc][j] for qc in range(n_qcols)], axis=1).astype(BF16)
            pv = _dot(jnp.concatenate([vt_ref[0, kbs[u, j], cols, :], ones], axis=0), pt)
            acc = pv if acc is None else acc + pv
        return acc[:LANES] * (1.0 / acc[LANES:LANES + 1])

    tasks = [(u, h) for u in range(n_sub) for h in range(A_HEADS)]
    ready = [scores(*t) for t in tasks[:SCORE_LOOKAHEAD]]
    pair_out = []
    for i, (u, h) in enumerate(tasks):
        if i + SCORE_LOOKAHEAD < len(tasks):
            ready.append(scores(*tasks[i + SCORE_LOOKAHEAD]))
        pair_out.append(softmax_pv(u, h, ready.pop(0)))
        if h % 2 == 1:
            cols = slice((h // 2) * LANES, (h // 2 + 1) * LANES)
            o_pair = jnp.concatenate([pair_out[0][:A_HEAD_DIM], pair_out[1][A_HEAD_DIM:]], axis=0)
            o_ref[0, u * Q_BLOCK:(u + 1) * Q_BLOCK, cols] = o_pair.T.astype(BF16)
            pair_out = []


def _attn_prompt(qt, k, vt, diag):
    b, t, _ = k.shape
    n_sub = min(ATTN_Q_BLOCKS_PER_STEP, t // Q_BLOCK)
    assert t % (n_sub * Q_BLOCK) == 0
    return pl.pallas_call(
        _attn_prompt_kernel,
        grid=(b, t // (n_sub * Q_BLOCK)),
        in_specs=[
            pl.BlockSpec((1, n_sub, A_WIDTH, Q_BLOCK), lambda i, j: (i, j, 0, 0)),
            pl.BlockSpec((1, t, A_WIDTH), lambda i, j: (i, 0, 0)),
            pl.BlockSpec((1, t // Q_BLOCK, A_WIDTH, Q_BLOCK), lambda i, j: (i, 0, 0, 0)),
            _const_spec(diag.shape),
        ],
        out_specs=pl.BlockSpec((1, n_sub * Q_BLOCK, A_WIDTH), lambda i, j: (i, j, 0)),
        out_shape=jax.ShapeDtypeStruct((b, t, A_WIDTH), BF16),
        scratch_shapes=[pltpu.VMEM((A_HEADS, N_KEY_BLOCKS + 1, Q_BLOCK, Q_BLOCK), F32)],
        compiler_params=_params(("arbitrary", "arbitrary")),
        name="attn_prompt",
    )(qt, k, vt, diag)


def _attn_sample_kernel(q_ref, ck_ref, cv_ref, nk_ref, nv_ref, bc_ref, bn_ref, o_ref):
    c = ck_ref.shape[1] // A_HEADS
    tasks = [(s, h) for s in range(q_ref.shape[0]) for h in range(A_HEADS)]
    head = lambda h: slice(h * A_HEAD_DIM, (h + 1) * A_HEAD_DIM)
    cached = lambda ref, s, h: ref[s, pl.ds(h, c, stride=A_HEADS), :].astype(BF16)
    scores = {}
    for s, h in tasks:
        qh = q_ref[s, :, head(h)]
        scores[s, h] = [_dot_nt(qh, cached(ck_ref, s, h)) + bc_ref[h],
                        _dot_nt(qh, nk_ref[s, :, head(h)].astype(BF16)) + bn_ref[h]]
    for s, h in tasks:
        out = _softmax_pv(scores[s, h], [cached(cv_ref, s, h), nv_ref[s, :, head(h)].astype(BF16)])
        o_ref[s, :, head(h)] = out.astype(BF16)


def _attn_sample(q, cache_k, cache_v, new_k, new_v, rel_bias):
    b, t, _ = q.shape
    c = cache_k.shape[1] // A_HEADS
    bias = _toeplitz_bias(rel_bias, t, c + LANES, c)
    bias = jnp.where((np.arange(c + LANES) < c + t)[None, None, :], bias, -jnp.inf)
    bias_c, bias_n = bias[:, :, :c], bias[:, :, c:]
    pad = ((0, 0), (0, LANES - t), (0, 0))
    nk, nv = jnp.pad(new_k, pad), jnp.pad(new_v, pad)
    ns = min(SAMPLE_ATTN_STREAMS, b)
    assert b % ns == 0
    blk = lambda r: pl.BlockSpec((ns, r, A_WIDTH), lambda i: (i, 0, 0))
    cache_blk = pl.BlockSpec((ns, c * A_HEADS, A_HEAD_DIM), lambda i: (i, 0, 0))
    return pl.pallas_call(
        _attn_sample_kernel,
        grid=(b // ns,),
        in_specs=[blk(t), cache_blk, cache_blk, blk(LANES), blk(LANES),
                  _const_spec(bias_c.shape), _const_spec(bias_n.shape)],
        out_specs=blk(t),
        out_shape=jax.ShapeDtypeStruct((b, t, A_WIDTH), BF16),
        compiler_params=_params(("arbitrary",)),
        name="attn_sample",
    )(q, cache_k, cache_v, nk, nv, bias_c, bias_n)


def _block_diag(x2):
    half = x2.shape[1] // 2
    z = jnp.zeros_like(x2[:, :half])
    top = jnp.concatenate([x2[:, :half], z], axis=1)
    bot = jnp.concatenate([z, x2[:, half:]], axis=1)
    return jnp.concatenate([top, bot], axis=0)


def _quad_diag(x4):
    n = x4.shape[0]
    group = lax.broadcasted_iota(jnp.int32, x4.shape, 1) // n
    zero = jnp.zeros_like(x4)
    return jnp.concatenate([jnp.where(group == g, x4, zero) for g in range(4)], axis=0)


def _unit_lower_inverse(m, eye_half):
    L = m.shape[0]
    h = L // 2
    lane = lax.broadcasted_iota(jnp.int32, (h, 2 * L), 1)
    first_half = (lane & (L - 1)) < h
    left_head = lane < L
    diag = jnp.where(first_half, m[:h], m[h:])
    b16 = jnp.where(first_half, m[h:], 0.0).astype(BF16)
    t = eye_half + diag
    d16 = diag.astype(BF16)
    p = _dot(d16, _quad_diag(d16))
    yield
    n_levels = h.bit_length() - 2
    for lvl in range(n_levels):
        p16 = p.astype(BF16)
        t16 = t.astype(BF16)
        if lvl == n_levels - 1:
            t = t + _dot(t16, _quad_diag(p16))
        else:
            res = _dot(jnp.concatenate([t16, p16], axis=0), _quad_diag(p16))
            t = t + res[:h]
            p = res[h:]
        yield
    t16 = t.astype(BF16)
    zero16 = jnp.zeros_like(t16)
    ta_only = jnp.where(first_half, t16, zero16)
    w_a = jnp.concatenate([jnp.where(left_head, ta_only, zero16), zero16,
                           jnp.where(left_head, zero16, ta_only), zero16], axis=0)
    y16 = _dot(b16, w_a).astype(BF16)
    yield
    w_y = jnp.concatenate([zero16, jnp.where(left_head, y16, zero16),
                           zero16, jnp.where(left_head, zero16, y16)], axis=0)
    t_b = _dot(t16, w_y)
    return jnp.where(first_half, t, 0.0), jnp.where(first_half, t_b, t)


def _round_robin(generators):
    results = [None] * len(generators)
    active = list(enumerate(generators))
    while active:
        still = []
        for i, g in active:
            try:
                next(g)
                still.append((i, g))
            except StopIteration as stop:
                results[i] = stop.value
        active = still
    return results


def _delta_kernel(q_ref, k_ref, v_ref, z_ref, ab_ref, s0_ref, alog_ref, dtb_ref, nw_ref,
                  o_ref, s_ref, rows_ref, *, n_valid):
    L = DELTA_CHUNK
    ns, tb = q_ref.shape[0], q_ref.shape[1]
    nch = tb // L
    n_pairs = B_HEADS // 2

    @pl.when(pl.program_id(1) == 0)
    def _():
        s_ref[...] = s0_ref[...]

    chains = [(s, c, p) for s in range(ns) for c in range(nch) for p in range(n_pairs)]

    st = {}
    for s, c, p in chains:
        r = slice(c * L, (c + 1) * L)
        ls = slice(p * 2 * L, (p + 1) * 2 * L)
        q2 = q_ref[s, r, ls]
        k2 = k_ref[s, r, ls]
        sc = _dot_nt(jnp.concatenate([q2, k2], axis=0), _block_diag(k2))
        st[s, c, p] = dict(q2=q2.astype(F32), k2=k2.astype(F32), qk=sc[:L], kk=sc[L:])

    scal = _round_robin([_delta_scalars(ab_ref, alog_ref, dtb_ref, rows_ref, s, n_valid) for s in range(ns)])
    _delta_chunks(st, chains, scal, v_ref, z_ref, nw_ref, o_ref, s_ref)


def _delta_scalars(ab_ref, alog_ref, dtb_ref, rows_ref, s, n_valid):
    L = DELTA_CHUNK
    tb = ab_ref.shape[2]
    nch = tb // L
    rows8 = ab_ref[s, 0:2 * B_HEADS, :]
    x = rows8 + dtb_ref[:, 0:1]
    softplus = jnp.maximum(x, 0.0) + jnp.log(1.0 + jnp.exp(-jnp.abs(x)))
    g8 = -jnp.exp(alog_ref[:, 0:1]) * softplus
    beta8 = _sigmoid(rows8)
    tok = lax.broadcasted_iota(jnp.int32, (2 * B_HEADS, tb), 1)
    if n_valid < tb:
        g8 = jnp.where(tok < n_valid, g8, 0.0)
        beta8 = jnp.where(tok < n_valid, beta8, 0.0)
    pos = tok & (L - 1)
    gc8 = g8
    shift = 1
    while shift < L:
        yield
        gc8 = gc8 + jnp.where(pos >= shift, pltpu.roll(gc8, shift, 1), 0.0)
        shift *= 2
    gl8 = [gc8[:, (c + 1) * L - 1:(c + 1) * L] for c in range(nch)]
    glb8 = jnp.concatenate([jnp.broadcast_to(g, (2 * B_HEADS, L)) for g in gl8], axis=1)
    rows_ref[s] = jnp.zeros(rows_ref.shape[1:], F32)
    rows_ref[s, 0:8, :] = gc8
    rows_ref[s, 8:16, :] = beta8
    rows_ref[s, 16:24, :] = jnp.exp(gc8)
    rows_ref[s, 24:32, :] = jnp.exp(glb8 - gc8)
    cols = rows_ref[s].T
    return gc8, gl8, cols


def _delta_chunks(st, chains, scal, v_ref, z_ref, nw_ref, o_ref, s_ref):
    L = DELTA_CHUNK
    ns, tb = v_ref.shape[0], v_ref.shape[1]
    nch = tb // L
    n_pairs = B_HEADS // 2
    ri = lax.broadcasted_iota(jnp.int32, (L, 2 * L), 0)
    ci = lax.broadcasted_iota(jnp.int32, (L, 2 * L), 1) & (L - 1)
    tril = ri >= ci
    strict = ri > ci
    rh = lax.broadcasted_iota(jnp.int32, (L // 2, 2 * L), 0)
    ch = lax.broadcasted_iota(jnp.int32, (L // 2, 2 * L), 1) & (L // 2 - 1)
    eye_half = (rh == ch).astype(F32)

    def chain_ops(s, c, p):
        gc8, _, cols = scal[s]
        r = slice(c * L, (c + 1) * L)
        ls = slice(p * 2 * L, (p + 1) * 2 * L)
        h0 = 2 * p
        d = st[s, c, p]
        pair_cols = lambda i: [jnp.broadcast_to(cols[r, i + e:i + e + 1], (L, L)) for e in range(2)]
        gc_c = jnp.concatenate(pair_cols(h0), axis=1)
        beta = pair_cols(8 + B_HEADS + h0)
        egc = pair_cols(16 + h0)
        edec = pair_cols(24 + h0)
        gc_r = jnp.concatenate([jnp.broadcast_to(gc8[h0 + e:h0 + e + 1, r], (L, L)) for e in range(2)], axis=1)
        decay = jnp.exp(jnp.where(tril, gc_c - gc_r, -jnp.inf))
        m = jnp.where(strict, -(jnp.concatenate(beta, axis=1) * d.pop("kk") * decay), 0.0)
        qkd16 = (d.pop("qk") * decay).astype(BF16)
        v2 = v_ref[s, r, ls].astype(F32)
        yield
        top, bot = yield from _unit_lower_inverse(m, eye_half)
        yield
        t16 = jnp.concatenate([top, bot], axis=0).astype(BF16)
        sols = []
        for e in range(2):
            hs = slice(e * L, (e + 1) * L)
            kh = d["k2"][:, hs]
            rhs = jnp.concatenate([v2[:, hs] * beta[e], kh * (beta[e] * egc[e])], axis=1)
            sols.append(_dot(t16[:, hs], rhs.astype(BF16)).astype(BF16))
        yield
        phi, psi, qt, oin = [], [], [], []
        for e in range(2):
            hs = slice(e * L, (e + 1) * L)
            kdt = (d["k2"][:, hs] * edec[e]).T.astype(BF16)
            r2 = _dot(jnp.concatenate([qkd16[:, hs], kdt], axis=0), sols[e])
            oin.append(r2[:L, :B_DV])
            qt.append(d["q2"][:, hs] * egc[e] - r2[:L, B_DV:])
            psi.append(r2[L:, :B_DV])
            phi.append(-r2[L:, B_DV:])
        d["lhs"] = jnp.concatenate([jnp.concatenate(phi, axis=1), jnp.concatenate(qt, axis=1)], axis=0).astype(BF16)
        d["psi"] = jnp.concatenate(psi, axis=1)
        d["oin"] = jnp.concatenate(oin, axis=1)

    lanes = [slice(p * 2 * L, (p + 1) * 2 * L) for p in range(n_pairs)]
    carried = [(s, p) for s in range(ns) for p in range(n_pairs)]
    s2 = {(s, p): s_ref[s, :, lanes[p]] for s, p in carried}
    nw = nw_ref[...]
    pending = []

    def emit(c, o2):
        r = slice(c * L, (c + 1) * L)
        for (s, p), o in o2.items():
            for e in range(2):
                hs = slice((2 * p + e) * B_DV, (2 * p + e + 1) * B_DV)
                zh = z_ref[s, r, hs].astype(F32)
                o_ref[s, r, hs] = (_rms(o[:, e * B_DV:(e + 1) * B_DV], nw) * _silu_of_twice(0.5 * zh)).astype(BF16)

    def carry_state(c):
        o2 = {}
        for s, p in carried:
            d = st[s, c, p]
            gl = scal[s][1][c]
            res = _dot(d["lhs"], _block_diag(s2[s, p].astype(BF16)))
            o2[s, p] = res[L:] + d["oin"]
            egl = jnp.concatenate([jnp.broadcast_to(jnp.exp(gl[2 * p + e:2 * p + e + 1, :]), (1, B_DV))
                                   for e in range(2)], axis=1)
            s2[s, p] = s2[s, p] * egl + res[:L] + d["psi"]
        if pending:
            emit(*pending.pop())
        pending.append((c, o2))

    waiting = [(key, chain_ops(*key)) for key in sorted(chains, key=lambda k: (k[1], k[0], k[2]))]
    live, finished, next_chunk = [], [0] * nch, 0
    while waiting or live:
        live += waiting[:CHAIN_STARTS_PER_ROUND]
        waiting = waiting[CHAIN_STARTS_PER_ROUND:]
        still = []
        for key, g in live:
            try:
                next(g)
                still.append((key, g))
            except StopIteration:
                finished[key[1]] += 1
        live = still
        while next_chunk < nch and finished[next_chunk] == len(carried):
            carry_state(next_chunk)
            next_chunk += 1
    emit(*pending.pop())
    for s, p in carried:
        s_ref[s, :, lanes[p]] = s2[s, p]


def _delta(q, k, v, z, ab, s0, a_log, dt_bias, norm_w, *, block, streams, n_valid):
    b, t, _ = q.shape
    streams = min(streams, b)
    assert t % block == 0 and block % DELTA_CHUNK == 0 and b % streams == 0
    alog8 = jnp.broadcast_to(jnp.tile(a_log.astype(F32), 2)[:, None], (2 * B_HEADS, LANES))
    dtb8 = jnp.broadcast_to(jnp.tile(dt_bias.astype(F32), 2)[:, None], (2 * B_HEADS, LANES))
    row = lambda w: pl.BlockSpec((streams, block, w), lambda i, j: (i, j, 0))
    state = pl.BlockSpec((streams, B_DK, B_HEADS * B_DV), lambda i, j: (i, 0, 0))
    s0 = s0.astype(F32).transpose(0, 2, 1, 3).reshape(b, B_DK, B_HEADS * B_DV)
    o, s = pl.pallas_call(
        functools.partial(_delta_kernel, n_valid=n_valid),
        grid=(b // streams, t // block),
        in_specs=[row(B_KW), row(B_KW), row(B_VW), row(B_VW),
                  pl.BlockSpec((streams, ab.shape[1], block), lambda i, j: (i, 0, j)), state,
                  _const_spec(alog8.shape), _const_spec(dtb8.shape), _const_spec((1, B_DV))],
        out_specs=[row(B_VW), state],
        out_shape=[jax.ShapeDtypeStruct((b, t, B_VW), BF16),
                   jax.ShapeDtypeStruct((b, B_DK, B_HEADS * B_DV), F32)],
        scratch_shapes=[pltpu.VMEM((streams, LANES, block), F32)],
        compiler_params=_params(("arbitrary", "arbitrary")),
        name="delta_%d" % block,
    )(q, k, v, z, ab, s0, alog8, dtb8, norm_w.reshape(1, B_DV).astype(F32))
    return o, s.reshape(b, B_DK, B_HEADS, B_DV).transpose(0, 2, 1, 3)


def _merge_kernel(x_ref, oa_ref, ob_ref, g_ref, wpa_ref, wpb_ref, wout_ref, gain_ref, o_ref):
    tm = x_ref.shape[0]
    sub = min(ROW_TILE, tm)
    blocks = [slice(r, r + sub) for r in range(0, tm, sub)]

    def branches(rs):
        return _dot(oa_ref[rs, :], wpa_ref[...]), _dot(ob_ref[rs, :], wpb_ref[...])

    nxt = branches(blocks[0])
    for i, rs in enumerate(blocks):
        ya, yb = nxt
        if i + 1 < len(blocks):
            nxt = branches(blocks[i + 1])
        merged = g_ref[rs, :D_MODEL].astype(F32) * ya + g_ref[rs, D_MODEL:].astype(F32) * yb
        y = _dot(merged.astype(BF16), wout_ref[...])
        o_ref[rs, :] = x_ref[rs, :] + _rms(y, gain_ref[...])


def _merge(x, oa, ob, gates, wpa, wpb, wout, gain):
    n, d = x.shape
    tm = min(MERGE_ROWS, n)
    row = lambda w: pl.BlockSpec((tm, w), lambda i: (i, 0))
    return pl.pallas_call(
        _merge_kernel,
        grid=(n // tm,),
        in_specs=[row(d), row(A_WIDTH), row(B_VW), row(2 * d),
                  _const_spec(wpa.shape), _const_spec(wpb.shape), _const_spec(wout.shape), _const_spec(gain.shape)],
        out_specs=row(d),
        out_shape=jax.ShapeDtypeStruct((n, d), F32),
        compiler_params=_params(("arbitrary",)),
        name="merge_%d" % n,
    )(x, oa, ob, gates, wpa, wpb, wout, gain)


FF_SPLIT = 4


def _mlp_kernel(x_ref, gpre_ref, wup_ref, wdown_ref, gpost_ref, o_ref):
    tm = x_ref.shape[0]
    sub = min(ROW_TILE, tm)
    blocks = [slice(r, r + sub) for r in range(0, tm, sub)]
    groups = [slice(c, c + D_FF // FF_SPLIT) for c in range(0, D_FF, D_FF // FF_SPLIT)]
    hbs = [_rms(x_ref[rs, :], gpre_ref[...]).astype(BF16) for rs in blocks]
    finish = None
    for rs, hb in zip(blocks, hbs):
        f = None
        for c, cs in enumerate(groups):
            up = jnp.maximum(_dot(hb, wup_ref[:, cs]), 0.0)
            if c == 0 and finish is not None:
                finish()
            part = _dot((up * up).astype(BF16), wdown_ref[cs, :])
            f = part if f is None else f + part

        def finish(rs=rs, f=f):
            o_ref[rs, :] = x_ref[rs, :] + _rms(f, gpost_ref[...])
    finish()


def _mlp(x, gpre, wup, wdown, gpost):
    n, d = x.shape
    tm = min(MLP_ROWS, n)
    row = pl.BlockSpec((tm, d), lambda i: (i, 0))
    return pl.pallas_call(
        _mlp_kernel,
        grid=(n // tm,),
        in_specs=[row, _const_spec(gpre.shape), _const_spec(wup.shape), _const_spec(wdown.shape),
                  _const_spec(gpost.shape)],
        out_specs=row,
        out_shape=jax.ShapeDtypeStruct((n, d), F32),
        compiler_params=_params(("arbitrary",)),
        name="mlp_%d" % n,
    )(x, gpre, wup, wdown, gpost)


def _layer_weights(norm_mix_pre, w_in, conv_w, w_proj_a, w_proj_b, w_out, norm_mix_post,
                   norm_mlp_pre, w_up, w_down, norm_mlp_post):
    o_cv = 3 * A_WIDTH
    o_z = o_cv + B_CONV_CH
    o_ab = o_z + B_VW
    o_g = o_ab + 2 * B_HEADS
    wb = w_in.astype(BF16)
    wzab = jnp.concatenate([wb[:, o_z:o_g], jnp.zeros((D_MODEL, LANES - 2 * B_HEADS), BF16)], axis=1)
    vec = lambda g: g.reshape(1, -1).astype(F32)
    return dict(
        gain_in=vec(norm_mix_pre),
        wq=wb[:, :A_WIDTH], wk=wb[:, A_WIDTH:2 * A_WIDTH], wv=wb[:, 2 * A_WIDTH:o_cv],
        wcv=wb[:, o_cv:o_z], wzab=wzab, wg=wb[:, o_g:] * 0.5,
        conv_w=conv_w.astype(F32),
        wpa=w_proj_a.astype(BF16), wpb=w_proj_b.astype(BF16), wout=w_out.astype(BF16),
        gain_mix=vec(norm_mix_post), gain_pre=vec(norm_mlp_pre), gain_post=vec(norm_mlp_post),
        wup=w_up.astype(BF16), wdown=w_down.astype(BF16),
    )


def _tail(x, w, oa, ob, gates):
    b, t, d = x.shape
    x1 = _merge(x.reshape(b * t, d), oa.reshape(b * t, A_WIDTH), ob.reshape(b * t, B_VW),
                gates.reshape(b * t, 2 * d), w["wpa"], w["wpb"], w["wout"], w["gain_mix"])
    return _mlp(x1, w["gain_pre"], w["wup"], w["wdown"], w["gain_post"]).reshape(b, t, d)


def _prompt_layer(x, w, rel_bias, a_log, dt_bias, delta_norm_w):
    b, t, _ = x.shape
    qt, k, vt, dq, dk, dv, z, ab, gates, kf, vf, cvt = _inproj(
        x, w["gain_in"], w["wq"], w["wk"], w["wv"], w["wcv"], w["wzab"], w["wg"], w["conv_w"], prompt=True)
    oa = _attn_prompt(qt, k, vt, _prompt_bias_diagonals(rel_bias))
    s0 = jnp.zeros((b, B_HEADS, B_DK, B_DV), F32)
    ob, s_new = _delta(dq, dk, dv, z, ab, s0, a_log, dt_bias, delta_norm_w,
                       block=DELTA_BLOCK, streams=DELTA_STREAMS, n_valid=DELTA_BLOCK)
    y = _tail(x, w, oa, ob, gates)
    keep = min(A_REACH, t)
    new_k = kf.reshape(b, keep, A_HEADS, A_HEAD_DIM)
    new_v = vf.reshape(b, keep, A_HEADS, A_HEAD_DIM)
    new_conv = cvt[:, SUBLANES - (CONV_WIDTH - 1):, :]
    return y, new_k, new_v, new_conv, s_new


def _sample_layer(x, cache_k, cache_v, conv_state, delta_state, w, rel_bias, a_log, dt_bias, delta_norm_w):
    b, t, d = x.shape
    c = cache_k.shape[1]
    q, k, v, cv, z, ab, gates = _inproj(
        x.reshape(1, b * t, d), w["gain_in"], w["wq"], w["wk"], w["wv"], w["wcv"], w["wzab"], w["wg"],
        prompt=False)
    per_stream = lambda a: a.reshape(b, t, a.shape[-1])
    q, k, v, cv, z, gates = map(per_stream, (q, k, v, cv, z, gates))
    ab = ab.reshape(AB_ROWS, b, t).transpose(1, 0, 2)
    oa = _attn_sample(q, cache_k.reshape(b, c * A_HEADS, A_HEAD_DIM), cache_v.reshape(b, c * A_HEADS, A_HEAD_DIM),
                      k, v, rel_bias)
    prev = jnp.pad(conv_state.astype(F32), ((0, 0), (SUBLANES - (CONV_WIDTH - 1), 0), (0, 0)))
    dq, dk, dv = _conv_sample(cv, prev, w["conv_w"])
    pad = lambda a: jnp.pad(a, ((0, 0), (0, DELTA_CHUNK - t), (0, 0)))
    ab = jnp.pad(ab, ((0, 0), (0, 0), (0, DELTA_CHUNK - t)))
    ob, s_new = _delta(pad(dq), pad(dk), pad(dv), pad(z), ab, delta_state, a_log, dt_bias, delta_norm_w,
                       block=DELTA_CHUNK, streams=SAMPLE_DELTA_STREAMS, n_valid=t)
    y = _tail(x, w, oa, ob[:, :t], gates)
    new_conv = jnp.concatenate([conv_state.astype(F32), cv], axis=1)[:, -(CONV_WIDTH - 1):]
    return (y, k.reshape(b, t, A_HEADS, A_HEAD_DIM), v.reshape(b, t, A_HEADS, A_HEAD_DIM), new_conv, s_new)


def kernel(x_prompt, x_sample, cache_attn_k, cache_attn_v, state_conv, state_delta, norm_mix_pre, w_in,
           rel_bias, conv_w, a_log, dt_bias, delta_norm_w, w_proj_a, w_proj_b, w_out, norm_mix_post,
           norm_mlp_pre, w_up, w_down, norm_mlp_post):
    depth = w_in.shape[0]
    yp, ys = x_prompt, x_sample
    outs = [[] for _ in range(8)]
    for l in range(depth):
        w = _layer_weights(norm_mix_pre[l], w_in[l], conv_w[l], w_proj_a[l], w_proj_b[l], w_out[l],
                           norm_mix_post[l], norm_mlp_pre[l], w_up[l], w_down[l], norm_mlp_post[l])
        yp, k1, v1, c1, s1 = _prompt_layer(yp, w, rel_bias[l], a_log[l], dt_bias[l], delta_norm_w[l])
        ys, k2, v2, c2, s2 = _sample_layer(ys, cache_attn_k[l], cache_attn_v[l], state_conv[l], state_delta[l],
                                           w, rel_bias[l], a_log[l], dt_bias[l], delta_norm_w[l])
        for lst, val in zip(outs, (k1, v1, c1, s1, k2, v2, c2, s2)):
            lst.append(val)
    return (yp, ys) + tuple(jnp.stack(o) for o in outs)
```

```python
import functools

import numpy as np
import jax
import jax.numpy as jnp
from jax import lax
from jax.experimental import pallas as pl
from jax.experimental.pallas import tpu as pltpu

F32 = jnp.float32
BF16 = jnp.bfloat16

D_MODEL = 1024
CHUNK = 64
RMS_EPS = 1e-6
L2_EPS = 1e-6
A_HEADS = 8
A_HEAD_DIM = 64
A_WIDTH = A_HEADS * A_HEAD_DIM
A_PREV_CHUNKS = 8
A_REACH = A_PREV_CHUNKS * CHUNK
MAX_REL = 128
B_HEADS = 4
B_DK = 128
B_DV = 128
B_KW = B_HEADS * B_DK
B_VW = B_HEADS * B_DV
B_CONV_CH = 2 * B_KW + B_VW
CONV_WIDTH = 4
D_FF = 4 * D_MODEL

LANES = 128
SUBLANES = 8
VMEM_LIMIT_BYTES = 56 * 1024 * 1024

Q_BLOCK = 256
N_KEY_BLOCKS = A_REACH // Q_BLOCK + 1
SCORE_LOOKAHEAD = 2
ATTN_Q_BLOCKS_PER_STEP = 8
ONES_ROWS = 16
LOG2_E = 1.4426950408889634
DELTA_CHUNK = 128
DELTA_BLOCK = 512
DELTA_STREAMS = 4
SAMPLE_DELTA_STREAMS = 8
SAMPLE_ATTN_STREAMS = 4
CHAIN_STARTS_PER_ROUND = 2
ROW_TILE = 512
INPROJ_ROWS = 1024
CONV_ROWS = 512
GATE_COLS = 512
AB_ROWS = 16
MERGE_ROWS = 1024
MLP_ROWS = 1024

assert B_DK == LANES and B_DV == LANES and B_HEADS % 2 == 0


def _dot(a, b):
    return jnp.dot(a, b, preferred_element_type=F32)


def _dot_nt(a, b):
    return lax.dot_general(a, b, (((1,), (1,)), ((), ())), preferred_element_type=F32)


def _sigmoid_of_twice(h):
    return 0.5 * jnp.tanh(h) + 0.5


def _sigmoid(x):
    return _sigmoid_of_twice(0.5 * x)


def _silu_of_twice(h):
    return h * jnp.tanh(h) + h


def _rms(x, gain):
    ms = jnp.mean(x * x, axis=-1, keepdims=True)
    return x * lax.rsqrt(ms + RMS_EPS) * gain


def _params(sem):
    return pltpu.CompilerParams(dimension_semantics=sem, vmem_limit_bytes=VMEM_LIMIT_BYTES)


def _const_spec(shape):
    nd = len(shape)
    return pl.BlockSpec(shape, lambda *_: (0,) * nd, pipeline_mode=pl.Buffered(1))


def _conv_silu(hist, x, w):
    n = x.shape[0]
    full = jnp.concatenate([hist, x], axis=0)
    rb = min(CONV_ROWS, n)
    half_w = 0.5 * w
    out = []
    for r0 in range(0, n, rb):
        piece = full[r0:r0 + rb + SUBLANES, :]
        z = piece * half_w[0:1, :]
        for i in range(1, CONV_WIDTH):
            z = pltpu.roll(z, 1, 0) + piece * half_w[i:i + 1, :]
        out.append(_silu_of_twice(z[SUBLANES:, :]))
    return out[0] if len(out) == 1 else jnp.concatenate(out, axis=0)


def _l2norm_heads(y, scale):
    heads = []
    for h in range(y.shape[1] // B_DK):
        yh = y[:, h * B_DK:(h + 1) * B_DK]
        heads.append(yh * (lax.rsqrt(jnp.sum(yh * yh, axis=-1, keepdims=True) + L2_EPS) * scale))
    return jnp.concatenate(heads, axis=1)


def _conv_silu_l2norm(hist, x, convw_ref):
    y = _conv_silu(hist, x, convw_ref[...])
    return (_l2norm_heads(y[:, :B_KW], B_DK ** -0.5), _l2norm_heads(y[:, B_KW:2 * B_KW], 1.0), y[:, 2 * B_KW:])


def _inproj_kernel(*refs, prompt):
    x_ref, gain_ref, wq_ref, wk_ref, wv_ref, wcv_ref, wzab_ref, wg_ref = refs[:8]
    if prompt:
        (convw_ref, q_ref, k_ref, vt_ref, dq_ref, dk_ref, dv_ref, z_ref, ab_ref, g_ref,
         kf_ref, vf_ref, cvt_ref, hist_ref) = refs[8:]
    else:
        q_ref, kf_ref, vf_ref, cv_ref, z_ref, ab_ref, g_ref = refs[8:]
    if prompt:
        @pl.when(pl.program_id(1) == 0)
        def _():
            hist_ref[...] = jnp.zeros((SUBLANES, B_CONV_CH), F32)

    x = x_ref[0]
    hb = _rms(x, gain_ref[...]).astype(BF16)
    tm = x.shape[0]

    stages = []

    def stage(issue):
        return lambda consume: stages.append((issue, consume))

    if prompt:
        norms = (B_DK ** -0.5, 1.0, None)
        for part, out_ref in enumerate((dq_ref, dk_ref, dv_ref)):
            cs = slice(part * B_KW, (part + 1) * B_KW)

            @stage(lambda cs=cs: _dot(hb, wcv_ref[:, cs]))
            def _(cv, cs=cs, out_ref=out_ref, scale=norms[part]):
                y = _conv_silu(hist_ref[:, cs], cv, convw_ref[:, cs])
                out_ref[0] = (y if scale is None else _l2norm_heads(y, scale)).astype(BF16)
                hist_ref[:, cs] = cv[tm - SUBLANES:, :]
                cvt_ref[0, :, cs] = cv[tm - SUBLANES:, :]
    else:
        @stage(lambda: _dot(hb, wcv_ref[...]))
        def _(cv):
            cv_ref[0] = cv

    def feature_major(val, out_ref):
        vt = val.T.astype(BF16)
        for j in range(tm // Q_BLOCK):
            out_ref[0, j] = vt[:, j * Q_BLOCK:(j + 1) * Q_BLOCK]

    @stage(lambda: _dot(hb, wq_ref[...]))
    def _(q):
        if prompt:
            feature_major(q * (A_HEAD_DIM ** -0.5 * LOG2_E), q_ref)
        else:
            q_ref[0] = (q * (A_HEAD_DIM ** -0.5)).astype(BF16)

    @stage(lambda: _dot(hb, wk_ref[...]))
    def _(k):
        kf_ref[0] = k[tm - kf_ref.shape[1]:]
        if prompt:
            k_ref[0] = k.astype(BF16)

    @stage(lambda: _dot(hb, wv_ref[...]))
    def _(v):
        vf_ref[0] = v[tm - vf_ref.shape[1]:]
        if prompt:
            feature_major(v, vt_ref)

    @stage(lambda: _dot(hb, wzab_ref[...]))
    def _(zab):
        z_ref[0] = zab[:, :B_VW].astype(BF16)
        ab_ref[0] = zab[:, B_VW:].T[0:AB_ROWS, :]

    for c0 in range(0, 2 * D_MODEL, GATE_COLS):
        cs = slice(c0, c0 + GATE_COLS)

        @stage(lambda cs=cs: _dot(hb, wg_ref[:, cs]))
        def _(g, cs=cs):
            g_ref[0, :, cs] = _sigmoid_of_twice(g).astype(BF16)

    pending = None
    for issue, consume in stages:
        val = issue()
        if pending is not None:
            pending[1](pending[0])
        pending = (val, consume)
    pending[1](pending[0])


def _inproj(x, gain, wq, wk, wv, wcv, wzab, wg, conv_w=None, *, prompt):
    b, t, d = x.shape
    tm = min(INPROJ_ROWS if prompt else ROW_TILE, t)
    assert t % tm == 0
    row = lambda w, dt: (jax.ShapeDtypeStruct((b, t, w), dt), pl.BlockSpec((1, tm, w), lambda i, j: (i, j, 0)))
    stream = lambda r, w: (jax.ShapeDtypeStruct((b, r, w), F32), pl.BlockSpec((1, r, w), lambda i, j: (i, 0, 0)))
    if prompt:
        assert tm % A_REACH == 0 and tm % Q_BLOCK == 0
        nq = tm // Q_BLOCK
        feat = (jax.ShapeDtypeStruct((b, t // Q_BLOCK, A_WIDTH, Q_BLOCK), BF16),
                pl.BlockSpec((1, nq, A_WIDTH, Q_BLOCK), lambda i, j: (i, j, 0, 0)))
        outs = [feat, row(A_WIDTH, BF16), feat, row(B_KW, BF16), row(B_KW, BF16), row(B_VW, BF16)]
    else:
        outs = [row(A_WIDTH, BF16), row(A_WIDTH, F32), row(A_WIDTH, F32), row(B_CONV_CH, F32)]
    logits = (jax.ShapeDtypeStruct((b, AB_ROWS, t), F32), pl.BlockSpec((1, AB_ROWS, tm), lambda i, j: (i, 0, j)))
    outs += [row(B_VW, BF16), logits, row(2 * D_MODEL, BF16)]
    weights = [wq, wk, wv, wcv, wzab, wg]
    scratch = []
    if prompt:
        outs += [stream(A_REACH, A_WIDTH), stream(A_REACH, A_WIDTH), stream(SUBLANES, B_CONV_CH)]
        weights.append(conv_w)
        scratch.append(pltpu.VMEM((SUBLANES, B_CONV_CH), F32))
    in_specs = [pl.BlockSpec((1, tm, d), lambda i, j: (i, j, 0)), _const_spec(gain.shape)]
    in_specs += [_const_spec(w.shape) for w in weights]
    return pl.pallas_call(
        functools.partial(_inproj_kernel, prompt=prompt),
        grid=(b, t // tm),
        in_specs=in_specs,
        out_specs=[o[1] for o in outs],
        out_shape=[o[0] for o in outs],
        scratch_shapes=scratch,
        compiler_params=_params(("arbitrary", "arbitrary")),
        name="inproj_prompt" if prompt else "inproj_sample",
    )(x, gain, *weights)


def _conv_sample_kernel(cv_ref, prev_ref, convw_ref, dq_ref, dk_ref, dv_ref):
    for s in range(cv_ref.shape[0]):
        dq, dk, dv = _conv_silu_l2norm(prev_ref[s], cv_ref[s], convw_ref)
        dq_ref[s] = dq.astype(BF16)
        dk_ref[s] = dk.astype(BF16)
        dv_ref[s] = dv.astype(BF16)


def _conv_sample(cv, prev, conv_w):
    b, t, _ = cv.shape
    blk = lambda r, w: pl.BlockSpec((b, r, w), lambda i: (0, 0, 0))
    out = jax.ShapeDtypeStruct((b, t, B_KW), BF16)
    return pl.pallas_call(
        _conv_sample_kernel,
        grid=(1,),
        in_specs=[blk(t, B_CONV_CH), blk(SUBLANES, B_CONV_CH), _const_spec(conv_w.shape)],
        out_specs=[blk(t, B_KW)] * 3,
        out_shape=[out] * 3,
        compiler_params=_params(("arbitrary",)),
        name="conv_sample",
    )(cv, prev, conv_w)


def _toeplitz_bias(rel_bias, rows, width, offset, sign=1, scale=1.0):
    pitch = width + rows
    period = pitch + 1
    j = np.arange(period)
    rel = sign * np.where(j < width, offset - j, offset + period - j)
    idx = np.clip(rel, -MAX_REL, MAX_REL) + MAX_REL
    vec = rel_bias.astype(F32)[:, idx] * scale
    tiled = jnp.tile(vec, (1, rows))[:, :rows * pitch]
    return tiled.reshape(rel_bias.shape[0], rows, pitch)[:, :, :width]


def _prompt_bias_diagonals(rel_bias):
    j = np.arange(2 * Q_BLOCK)
    d = np.where(j < Q_BLOCK, j, j - 2 * Q_BLOCK)
    rel = d[None, :] + A_REACH - Q_BLOCK * np.arange(N_KEY_BLOCKS)[:, None]
    idx = np.clip(rel, -MAX_REL, MAX_REL) + MAX_REL
    return rel_bias.astype(F32)[:, idx] * LOG2_E


def _fill_prompt_bias(diag_ref, bias_ref):
    kc = lax.broadcasted_iota(jnp.int32, (Q_BLOCK, Q_BLOCK), 0) // CHUNK
    qc = lax.broadcasted_iota(jnp.int32, (Q_BLOCK, Q_BLOCK), 1) // CHUNK
    for jb in range(N_KEY_BLOCKS):
        kcw = kc + jb * (Q_BLOCK // CHUNK)
        band = (kcw >= qc) & (kcw <= qc + A_PREV_CHUNKS)
        for h in range(A_HEADS):
            rows = jnp.broadcast_to(diag_ref[h, jb:jb + 1, :], (Q_BLOCK, 2 * Q_BLOCK))
            rotated = pltpu.roll(rows, 0, 1, stride=1, stride_axis=0)
            bias_ref[h, jb] = jnp.where(band, rotated[:, :Q_BLOCK], -jnp.inf)
    for h in range(A_HEADS):
        bias_ref[h, N_KEY_BLOCKS] = jnp.full((Q_BLOCK, Q_BLOCK), -jnp.inf, F32)


def _softmax_pv(scores, values):
    m = None
    for s in scores:
        mj = jnp.max(s, axis=-1, keepdims=True)
        m = mj if m is None else jnp.maximum(m, mj)
    denom = None
    acc = None
    for s, vals in zip(scores, values):
        p = jnp.exp(s - m)
        dj = jnp.sum(p, axis=-1, keepdims=True)
        denom = dj if denom is None else denom + dj
        pv = _dot(p.astype(BF16), vals)
        acc = pv if acc is None else acc + pv
    return acc / denom


def _attn_prompt_kernel(qt_ref, k_ref, vt_ref, diag_ref, o_ref, bias_ref):
    @pl.when((pl.program_id(0) == 0) & (pl.program_id(1) == 0))
    def _():
        _fill_prompt_bias(diag_ref, bias_ref)

    n_sub = qt_ref.shape[1]
    row = lax.broadcasted_iota(jnp.int32, (LANES, Q_BLOCK), 0)
    low = row < A_HEAD_DIM
    kbs, bsels = {}, {}
    for u in range(n_sub):
        qi = pl.program_id(1) * n_sub + u
        for j in range(N_KEY_BLOCKS):
            kraw = qi - (N_KEY_BLOCKS - 1) + j
            kbs[u, j] = jnp.maximum(kraw, 0)
            bsels[u, j] = jnp.where(kraw >= 0, j, N_KEY_BLOCKS)
    n_qcols = Q_BLOCK // LANES

    def scores(u, h):
        cols = slice((h // 2) * LANES, (h // 2 + 1) * LANES)
        qp = qt_ref[0, u, cols, :]
        qm = jnp.where(low if h % 2 == 0 else jnp.logical_not(low), qp, jnp.zeros_like(qp))
        out = []
        for j in range(N_KEY_BLOCKS):
            kblk = k_ref[0, pl.ds(pl.multiple_of(kbs[u, j] * Q_BLOCK, Q_BLOCK), Q_BLOCK), cols]
            out.append(_dot(kblk, qm) + bias_ref[h, bsels[u, j]])
        return out

    def softmax_pv(u, h, s):
        cols = slice((h // 2) * LANES, (h // 2 + 1) * LANES)
        p_cols = []
        for qc in range(n_qcols):
            ls = slice(qc * LANES, (qc + 1) * LANES)
            lo, hi = qc * LANES, qc * LANES + A_REACH + LANES
            spans = [(max(lo - j * Q_BLOCK, 0), min(hi - j * Q_BLOCK, Q_BLOCK)) for j in range(N_KEY_BLOCKS)]
            parts = [s[j][a:b, ls] for j, (a, b) in enumerate(spans)]
            m = functools.reduce(jnp.maximum, [jnp.max(x, axis=0, keepdims=True) for x in parts])
            ps = [jnp.exp2(x - m) for x in parts]
            padded = []
            for x, (a, b) in zip(ps, spans):
                pieces = ([jnp.zeros((a, LANES), F32)] if a else []) + [x]
                pieces += [jnp.zeros((Q_BLOCK - b, LANES), F32)] if b < Q_BLOCK else []
                padded.append(pieces[0] if len(pieces) == 1 else jnp.concatenate(pieces, axis=0))
            p_cols.append(padded)
        ones = jnp.ones((ONES_ROWS, Q_BLOCK), BF16)
        acc = None
        for j in range(N_KEY_BLOCKS):
            pt = jnp.concatenate([p_cols[qc][j] for qc in range(n_qcols)], axis=1).astype(BF16)
            pv = _dot(jnp.concatenate([vt_ref[0, kbs[u, j], cols, :], ones], axis=0), pt)
            acc = pv if acc is None else acc + pv
        return acc[:LANES] * (1.0 / acc[LANES:LANES + 1])

    tasks = [(u, h) for u in range(n_sub) for h in range(A_HEADS)]
    ready = [scores(*t) for t in tasks[:SCORE_LOOKAHEAD]]
    pair_out = []
    for i, (u, h) in enumerate(tasks):
        if i + SCORE_LOOKAHEAD < len(tasks):
            ready.append(scores(*tasks[i + SCORE_LOOKAHEAD]))
        pair_out.append(softmax_pv(u, h, ready.pop(0)))
        if h % 2 == 1:
            cols = slice((h // 2) * LANES, (h // 2 + 1) * LANES)
            o_pair = jnp.concatenate([pair_out[0][:A_HEAD_DIM], pair_out[1][A_HEAD_DIM:]], axis=0)
            o_ref[0, u * Q_BLOCK:(u + 1) * Q_BLOCK, cols] = o_pair.T.astype(BF16)
            pair_out = []


def _attn_prompt(qt, k, vt, diag):
    b, t, _ = k.shape
    n_sub = min(ATTN_Q_BLOCKS_PER_STEP, t // Q_BLOCK)
    assert t % (n_sub * Q_BLOCK) == 0
    return pl.pallas_call(
        _attn_prompt_kernel,
        grid=(b, t // (n_sub * Q_BLOCK)),
        in_specs=[
            pl.BlockSpec((1, n_sub, A_WIDTH, Q_BLOCK), lambda i, j: (i, j, 0, 0)),
            pl.BlockSpec((1, t, A_WIDTH), lambda i, j: (i, 0, 0)),
            pl.BlockSpec((1, t // Q_BLOCK, A_WIDTH, Q_BLOCK), lambda i, j: (i, 0, 0, 0)),
            _const_spec(diag.shape),
        ],
        out_specs=pl.BlockSpec((1, n_sub * Q_BLOCK, A_WIDTH), lambda i, j: (i, j, 0)),
        out_shape=jax.ShapeDtypeStruct((b, t, A_WIDTH), BF16),
        scratch_shapes=[pltpu.VMEM((A_HEADS, N_KEY_BLOCKS + 1, Q_BLOCK, Q_BLOCK), F32)],
        compiler_params=_params(("arbitrary", "arbitrary")),
        name="attn_prompt",
    )(qt, k, vt, diag)


def _attn_sample_kernel(q_ref, ck_ref, cv_ref, nk_ref, nv_ref, bc_ref, bn_ref, o_ref):
    rows = q_ref.shape[1]
    lane = lax.broadcasted_iota(jnp.int32, (rows, LANES), 1)
    low = lane < A_HEAD_DIM
    pair_cols = [slice(hp * LANES, (hp + 1) * LANES) for hp in range(A_HEADS // 2)]
    tasks = [(s, h) for s in range(q_ref.shape[0]) for h in range(A_HEADS)]
    scores = {}
    for s, h in tasks:
        cols = pair_cols[h // 2]
        qp = q_ref[s, :, cols]
        qm = jnp.where(low if h % 2 == 0 else jnp.logical_not(low), qp, jnp.zeros_like(qp))
        scores[s, h] = [_dot_nt(qm, ck_ref[s, :, cols].astype(BF16)) + bc_ref[h],
                        _dot_nt(qm, nk_ref[s, :, cols].astype(BF16)) + bn_ref[h]]
    outs = {(s, h): _softmax_pv(scores[s, h], [cv_ref[s, :, pair_cols[h // 2]].astype(BF16),
                                               nv_ref[s, :, pair_cols[h // 2]].astype(BF16)]) for s, h in tasks}
    for s in range(q_ref.shape[0]):
        for hp, cols in enumerate(pair_cols):
            o_ref[s, :, cols] = jnp.where(low, outs[s, 2 * hp], outs[s, 2 * hp + 1]).astype(BF16)


def _attn_sample(q, cache_k, cache_v, new_k, new_v, rel_bias):
    b, t, _ = q.shape
    c = cache_k.shape[1]
    bias = _toeplitz_bias(rel_bias, t, c + LANES, c)
    bias = jnp.where((np.arange(c + LANES) < c + t)[None, None, :], bias, -jnp.inf)
    bias_c, bias_n = bias[:, :, :c], bias[:, :, c:]
    pad = ((0, 0), (0, LANES - t), (0, 0))
    nk, nv = jnp.pad(new_k, pad), jnp.pad(new_v, pad)
    ns = min(SAMPLE_ATTN_STREAMS, b)
    assert b % ns == 0
    blk = lambda r: pl.BlockSpec((ns, r, A_WIDTH), lambda i: (i, 0, 0))
    return pl.pallas_call(
        _attn_sample_kernel,
        grid=(b // ns,),
        in_specs=[blk(t), blk(c), blk(c), blk(LANES), blk(LANES),
                  _const_spec(bias_c.shape), _const_spec(bias_n.shape)],
        out_specs=blk(t),
        out_shape=jax.ShapeDtypeStruct((b, t, A_WIDTH), BF16),
        compiler_params=_params(("arbitrary",)),
        name="attn_sample",
    )(q, cache_k, cache_v, nk, nv, bias_c, bias_n)


def _block_diag(x2):
    half = x2.shape[1] // 2
    z = jnp.zeros_like(x2[:, :half])
    top = jnp.concatenate([x2[:, :half], z], axis=1)
    bot = jnp.concatenate([z, x2[:, half:]], axis=1)
    return jnp.concatenate([top, bot], axis=0)


def _quad_diag(x4):
    n = x4.shape[0]
    group = lax.broadcasted_iota(jnp.int32, x4.shape, 1) // n
    zero = jnp.zeros_like(x4)
    return jnp.concatenate([jnp.where(group == g, x4, zero) for g in range(4)], axis=0)


def _unit_lower_inverse(m, eye_half):
    L = m.shape[0]
    h = L // 2
    lane = lax.broadcasted_iota(jnp.int32, (h, 2 * L), 1)
    first_half = (lane & (L - 1)) < h
    left_head = lane < L
    diag = jnp.where(first_half, m[:h], m[h:])
    b16 = jnp.where(first_half, m[h:], 0.0).astype(BF16)
    t = eye_half + diag
    d16 = diag.astype(BF16)
    p = _dot(d16, _quad_diag(d16))
    yield
    n_levels = h.bit_length() - 2
    for lvl in range(n_levels):
        p16 = p.astype(BF16)
        t16 = t.astype(BF16)
        if lvl == n_levels - 1:
            t = t + _dot(t16, _quad_diag(p16))
        else:
            res = _dot(jnp.concatenate([t16, p16], axis=0), _quad_diag(p16))
            t = t + res[:h]
            p = res[h:]
        yield
    t16 = t.astype(BF16)
    zero16 = jnp.zeros_like(t16)
    ta_only = jnp.where(first_half, t16, zero16)
    w_a = jnp.concatenate([jnp.where(left_head, ta_only, zero16), zero16,
                           jnp.where(left_head, zero16, ta_only), zero16], axis=0)
    y16 = _dot(b16, w_a).astype(BF16)
    yield
    w_y = jnp.concatenate([zero16, jnp.where(left_head, y16, zero16),
                           zero16, jnp.where(left_head, zero16, y16)], axis=0)
    t_b = _dot(t16, w_y)
    return jnp.where(first_half, t, 0.0), jnp.where(first_half, t_b, t)


def _round_robin(generators):
    results = [None] * len(generators)
    active = list(enumerate(generators))
    while active:
        still = []
        for i, g in active:
            try:
                next(g)
                still.append((i, g))
            except StopIteration as stop:
                results[i] = stop.value
        active = still
    return results


def _delta_kernel(q_ref, k_ref, v_ref, z_ref, ab_ref, s0_ref, alog_ref, dtb_ref, nw_ref,
                  o_ref, s_ref, rows_ref, *, n_valid):
    L = DELTA_CHUNK
    ns, tb = q_ref.shape[0], q_ref.shape[1]
    nch = tb // L
    n_pairs = B_HEADS // 2

    @pl.when(pl.program_id(1) == 0)
    def _():
        s_ref[...] = s0_ref[...]

    chains = [(s, c, p) for s in range(ns) for c in range(nch) for p in range(n_pairs)]

    st = {}
    for s, c, p in chains:
        r = slice(c * L, (c + 1) * L)
        ls = slice(p * 2 * L, (p + 1) * 2 * L)
        q2 = q_ref[s, r, ls]
        k2 = k_ref[s, r, ls]
        sc = _dot_nt(jnp.concatenate([q2, k2], axis=0), _block_diag(k2))
        st[s, c, p] = dict(q2=q2.astype(F32), k2=k2.astype(F32), qk=sc[:L], kk=sc[L:])

    scal = _round_robin([_delta_scalars(ab_ref, alog_ref, dtb_ref, rows_ref, s, n_valid) for s in range(ns)])
    _delta_chunks(st, chains, scal, v_ref, z_ref, nw_ref, o_ref, s_ref)


def _delta_scalars(ab_ref, alog_ref, dtb_ref, rows_ref, s, n_valid):
    L = DELTA_CHUNK
    tb = ab_ref.shape[2]
    nch = tb // L
    rows8 = ab_ref[s, 0:2 * B_HEADS, :]
    x = rows8 + dtb_ref[:, 0:1]
    softplus = jnp.maximum(x, 0.0) + jnp.log(1.0 + jnp.exp(-jnp.abs(x)))
    g8 = -jnp.exp(alog_ref[:, 0:1]) * softplus
    beta8 = _sigmoid(rows8)
    tok = lax.broadcasted_iota(jnp.int32, (2 * B_HEADS, tb), 1)
    if n_valid < tb:
        g8 = jnp.where(tok < n_valid, g8, 0.0)
        beta8 = jnp.where(tok < n_valid, beta8, 0.0)
    pos = tok & (L - 1)
    gc8 = g8
    shift = 1
    while shift < L:
        yield
        gc8 = gc8 + jnp.where(pos >= shift, pltpu.roll(gc8, shift, 1), 0.0)
        shift *= 2
    gl8 = [gc8[:, (c + 1) * L - 1:(c + 1) * L] for c in range(nch)]
    glb8 = jnp.concatenate([jnp.broadcast_to(g, (2 * B_HEADS, L)) for g in gl8], axis=1)
    rows_ref[s] = jnp.zeros(rows_ref.shape[1:], F32)
    rows_ref[s, 0:8, :] = gc8
    rows_ref[s, 8:16, :] = beta8
    rows_ref[s, 16:24, :] = jnp.exp(gc8)
    rows_ref[s, 24:32, :] = jnp.exp(glb8 - gc8)
    cols = rows_ref[s].T
    return gc8, gl8, cols


def _delta_chunks(st, chains, scal, v_ref, z_ref, nw_ref, o_ref, s_ref):
    L = DELTA_CHUNK
    ns, tb = v_ref.shape[0], v_ref.shape[1]
    nch = tb // L
    n_pairs = B_HEADS // 2
    ri = lax.broadcasted_iota(jnp.int32, (L, 2 * L), 0)
    ci = lax.broadcasted_iota(jnp.int32, (L, 2 * L), 1) & (L - 1)
    tril = ri >= ci
    strict = ri > ci
    rh = lax.broadcasted_iota(jnp.int32, (L // 2, 2 * L), 0)
    ch = lax.broadcasted_iota(jnp.int32, (L // 2, 2 * L), 1) & (L // 2 - 1)
    eye_half = (rh == ch).astype(F32)

    def chain_ops(s, c, p):
        gc8, _, cols = scal[s]
        r = slice(c * L, (c + 1) * L)
        ls = slice(p * 2 * L, (p + 1) * 2 * L)
        h0 = 2 * p
        d = st[s, c, p]
        pair_cols = lambda i: [jnp.broadcast_to(cols[r, i + e:i + e + 1], (L, L)) for e in range(2)]
        gc_c = jnp.concatenate(pair_cols(h0), axis=1)
        beta = pair_cols(8 + B_HEADS + h0)
        egc = pair_cols(16 + h0)
        edec = pair_cols(24 + h0)
        gc_r = jnp.concatenate([jnp.broadcast_to(gc8[h0 + e:h0 + e + 1, r], (L, L)) for e in range(2)], axis=1)
        decay = jnp.exp(jnp.where(tril, gc_c - gc_r, -jnp.inf))
        m = jnp.where(strict, -(jnp.concatenate(beta, axis=1) * d.pop("kk") * decay), 0.0)
        qkd16 = (d.pop("qk") * decay).astype(BF16)
        v2 = v_ref[s, r, ls].astype(F32)
        yield
        top, bot = yield from _unit_lower_inverse(m, eye_half)
        yield
        t16 = jnp.concatenate([top, bot], axis=0).astype(BF16)
        sols = []
        for e in range(2):
            hs = slice(e * L, (e + 1) * L)
            kh = d["k2"][:, hs]
            rhs = jnp.concatenate([v2[:, hs] * beta[e], kh * (beta[e] * egc[e])], axis=1)
            sols.append(_dot(t16[:, hs], rhs.astype(BF16)).astype(BF16))
        yield
        phi, psi, qt, oin = [], [], [], []
        for e in range(2):
            hs = slice(e * L, (e + 1) * L)
            kdt = (d["k2"][:, hs] * edec[e]).T.astype(BF16)
            r2 = _dot(jnp.concatenate([qkd16[:, hs], kdt], axis=0), sols[e])
            oin.append(r2[:L, :B_DV])
            qt.append(d["q2"][:, hs] * egc[e] - r2[:L, B_DV:])
            psi.append(r2[L:, :B_DV])
            phi.append(-r2[L:, B_DV:])
        d["lhs"] = jnp.concatenate([jnp.concatenate(phi, axis=1), jnp.concatenate(qt, axis=1)], axis=0).astype(BF16)
        d["psi"] = jnp.concatenate(psi, axis=1)
        d["oin"] = jnp.concatenate(oin, axis=1)

    lanes = [slice(p * 2 * L, (p + 1) * 2 * L) for p in range(n_pairs)]
    carried = [(s, p) for s in range(ns) for p in range(n_pairs)]
    s2 = {(s, p): s_ref[s, :, lanes[p]] for s, p in carried}
    nw = nw_ref[...]
    pending = []

    def emit(c, o2):
        r = slice(c * L, (c + 1) * L)
        for (s, p), o in o2.items():
            for e in range(2):
                hs = slice((2 * p + e) * B_DV, (2 * p + e + 1) * B_DV)
                zh = z_ref[s, r, hs].astype(F32)
                o_ref[s, r, hs] = (_rms(o[:, e * B_DV:(e + 1) * B_DV], nw) * _silu_of_twice(0.5 * zh)).astype(BF16)

    def carry_state(c):
        o2 = {}
        for s, p in carried:
            d = st[s, c, p]
            gl = scal[s][1][c]
            res = _dot(d["lhs"], _block_diag(s2[s, p].astype(BF16)))
            o2[s, p] = res[L:] + d["oin"]
            egl = jnp.concatenate([jnp.broadcast_to(jnp.exp(gl[2 * p + e:2 * p + e + 1, :]), (1, B_DV))
                                   for e in range(2)], axis=1)
            s2[s, p] = s2[s, p] * egl + res[:L] + d["psi"]
        if pending:
            emit(*pending.pop())
        pending.append((c, o2))

    waiting = [(key, chain_ops(*key)) for key in sorted(chains, key=lambda k: (k[1], k[0], k[2]))]
    live, finished, next_chunk = [], [0] * nch, 0
    while waiting or live:
        live += waiting[:CHAIN_STARTS_PER_ROUND]
        waiting = waiting[CHAIN_STARTS_PER_ROUND:]
        still = []
        for key, g in live:
            try:
                next(g)
                still.append((key, g))
            except StopIteration:
                finished[key[1]] += 1
        live = still
        while next_chunk < nch and finished[next_chunk] == len(carried):
            carry_state(next_chunk)
            next_chunk += 1
    emit(*pending.pop())
    for s, p in carried:
        s_ref[s, :, lanes[p]] = s2[s, p]


def _delta(q, k, v, z, ab, s0, a_log, dt_bias, norm_w, *, block, streams, n_valid):
    b, t, _ = q.shape
    streams = min(streams, b)
    assert t % block == 0 and block % DELTA_CHUNK == 0 and b % streams == 0
    alog8 = jnp.broadcast_to(jnp.tile(a_log.astype(F32), 2)[:, None], (2 * B_HEADS, LANES))
    dtb8 = jnp.broadcast_to(jnp.tile(dt_bias.astype(F32), 2)[:, None], (2 * B_HEADS, LANES))
    row = lambda w: pl.BlockSpec((streams, block, w), lambda i, j: (i, j, 0))
    state = pl.BlockSpec((streams, B_DK, B_HEADS * B_DV), lambda i, j: (i, 0, 0))
    s0 = s0.astype(F32).transpose(0, 2, 1, 3).reshape(b, B_DK, B_HEADS * B_DV)
    o, s = pl.pallas_call(
        functools.partial(_delta_kernel, n_valid=n_valid),
        grid=(b // streams, t // block),
        in_specs=[row(B_KW), row(B_KW), row(B_VW), row(B_VW),
                  pl.BlockSpec((streams, ab.shape[1], block), lambda i, j: (i, 0, j)), state,
                  _const_spec(alog8.shape), _const_spec(dtb8.shape), _const_spec((1, B_DV))],
        out_specs=[row(B_VW), state],
        out_shape=[jax.ShapeDtypeStruct((b, t, B_VW), BF16),
                   jax.ShapeDtypeStruct((b, B_DK, B_HEADS * B_DV), F32)],
        scratch_shapes=[pltpu.VMEM((streams, LANES, block), F32)],
        compiler_params=_params(("arbitrary", "arbitrary")),
        name="delta_%d" % block,
    )(q, k, v, z, ab, s0, alog8, dtb8, norm_w.reshape(1, B_DV).astype(F32))
    return o, s.reshape(b, B_DK, B_HEADS, B_DV).transpose(0, 2, 1, 3)


def _merge_kernel(x_ref, oa_ref, ob_ref, g_ref, wpa_ref, wpb_ref, wout_ref, gain_ref, o_ref):
    tm = x_ref.shape[0]
    sub = min(ROW_TILE, tm)
    blocks = [slice(r, r + sub) for r in range(0, tm, sub)]

    def branches(rs):
        return _dot(oa_ref[rs, :], wpa_ref[...]), _dot(ob_ref[rs, :], wpb_ref[...])

    nxt = branches(blocks[0])
    for i, rs in enumerate(blocks):
        ya, yb = nxt
        if i + 1 < len(blocks):
            nxt = branches(blocks[i + 1])
        merged = g_ref[rs, :D_MODEL].astype(F32) * ya + g_ref[rs, D_MODEL:].astype(F32) * yb
        y = _dot(merged.astype(BF16), wout_ref[...])
        o_ref[rs, :] = x_ref[rs, :] + _rms(y, gain_ref[...])


def _merge(x, oa, ob, gates, wpa, wpb, wout, gain):
    n, d = x.shape
    tm = min(MERGE_ROWS, n)
    row = lambda w: pl.BlockSpec((tm, w), lambda i: (i, 0))
    return pl.pallas_call(
        _merge_kernel,
        grid=(n // tm,),
        in_specs=[row(d), row(A_WIDTH), row(B_VW), row(2 * d),
                  _const_spec(wpa.shape), _const_spec(wpb.shape), _const_spec(wout.shape), _const_spec(gain.shape)],
        out_specs=row(d),
        out_shape=jax.ShapeDtypeStruct((n, d), F32),
        compiler_params=_params(("arbitrary",)),
        name="merge_%d" % n,
    )(x, oa, ob, gates, wpa, wpb, wout, gain)


FF_SPLIT = 4


def _mlp_kernel(x_ref, gpre_ref, wup_ref, wdown_ref, gpost_ref, o_ref):
    tm = x_ref.shape[0]
    sub = min(ROW_TILE, tm)
    blocks = [slice(r, r + sub) for r in range(0, tm, sub)]
    groups = [slice(c, c + D_FF // FF_SPLIT) for c in range(0, D_FF, D_FF // FF_SPLIT)]
    hbs = [_rms(x_ref[rs, :], gpre_ref[...]).astype(BF16) for rs in blocks]
    finish = None
    for rs, hb in zip(blocks, hbs):
        f = None
        for c, cs in enumerate(groups):
            up = jnp.maximum(_dot(hb, wup_ref[:, cs]), 0.0)
            if c == 0 and finish is not None:
                finish()
            part = _dot((up * up).astype(BF16), wdown_ref[cs, :])
            f = part if f is None else f + part

        def finish(rs=rs, f=f):
            o_ref[rs, :] = x_ref[rs, :] + _rms(f, gpost_ref[...])
    finish()


def _mlp(x, gpre, wup, wdown, gpost):
    n, d = x.shape
    tm = min(MLP_ROWS, n)
    row = pl.BlockSpec((tm, d), lambda i: (i, 0))
    return pl.pallas_call(
        _mlp_kernel,
        grid=(n // tm,),
        in_specs=[row, _const_spec(gpre.shape), _const_spec(wup.shape), _const_spec(wdown.shape),
                  _const_spec(gpost.shape)],
        out_specs=row,
        out_shape=jax.ShapeDtypeStruct((n, d), F32),
        compiler_params=_params(("arbitrary",)),
        name="mlp_%d" % n,
    )(x, gpre, wup, wdown, gpost)


def _layer_weights(norm_mix_pre, w_in, conv_w, w_proj_a, w_proj_b, w_out, norm_mix_post,
                   norm_mlp_pre, w_up, w_down, norm_mlp_post):
    o_cv = 3 * A_WIDTH
    o_z = o_cv + B_CONV_CH
    o_ab = o_z + B_VW
    o_g = o_ab + 2 * B_HEADS
    wb = w_in.astype(BF16)
    wzab = jnp.concatenate([wb[:, o_z:o_g], jnp.zeros((D_MODEL, LANES - 2 * B_HEADS), BF16)], axis=1)
    vec = lambda g: g.reshape(1, -1).astype(F32)
    return dict(
        gain_in=vec(norm_mix_pre),
        wq=wb[:, :A_WIDTH], wk=wb[:, A_WIDTH:2 * A_WIDTH], wv=wb[:, 2 * A_WIDTH:o_cv],
        wcv=wb[:, o_cv:o_z], wzab=wzab, wg=wb[:, o_g:] * 0.5,
        conv_w=conv_w.astype(F32),
        wpa=w_proj_a.astype(BF16), wpb=w_proj_b.astype(BF16), wout=w_out.astype(BF16),
        gain_mix=vec(norm_mix_post), gain_pre=vec(norm_mlp_pre), gain_post=vec(norm_mlp_post),
        wup=w_up.astype(BF16), wdown=w_down.astype(BF16),
    )


def _tail(x, w, oa, ob, gates):
    b, t, d = x.shape
    x1 = _merge(x.reshape(b * t, d), oa.reshape(b * t, A_WIDTH), ob.reshape(b * t, B_VW),
                gates.reshape(b * t, 2 * d), w["wpa"], w["wpb"], w["wout"], w["gain_mix"])
    return _mlp(x1, w["gain_pre"], w["wup"], w["wdown"], w["gain_post"]).reshape(b, t, d)


def _prompt_layer(x, w, rel_bias, a_log, dt_bias, delta_norm_w):
    b, t, _ = x.shape
    qt, k, vt, dq, dk, dv, z, ab, gates, kf, vf, cvt = _inproj(
        x, w["gain_in"], w["wq"], w["wk"], w["wv"], w["wcv"], w["wzab"], w["wg"], w["conv_w"], prompt=True)
    oa = _attn_prompt(qt, k, vt, _prompt_bias_diagonals(rel_bias))
    s0 = jnp.zeros((b, B_HEADS, B_DK, B_DV), F32)
    ob, s_new = _delta(dq, dk, dv, z, ab, s0, a_log, dt_bias, delta_norm_w,
                       block=DELTA_BLOCK, streams=DELTA_STREAMS, n_valid=DELTA_BLOCK)
    y = _tail(x, w, oa, ob, gates)
    keep = min(A_REACH, t)
    new_k = kf.reshape(b, keep, A_HEADS, A_HEAD_DIM)
    new_v = vf.reshape(b, keep, A_HEADS, A_HEAD_DIM)
    new_conv = cvt[:, SUBLANES - (CONV_WIDTH - 1):, :]
    return y, new_k, new_v, new_conv, s_new


def _sample_layer(x, cache_k, cache_v, conv_state, delta_state, w, rel_bias, a_log, dt_bias, delta_norm_w):
    b, t, d = x.shape
    c = cache_k.shape[1]
    q, k, v, cv, z, ab, gates = _inproj(
        x.reshape(1, b * t, d), w["gain_in"], w["wq"], w["wk"], w["wv"], w["wcv"], w["wzab"], w["wg"],
        prompt=False)
    per_stream = lambda a: a.reshape(b, t, a.shape[-1])
    q, k, v, cv, z, gates = map(per_stream, (q, k, v, cv, z, gates))
    ab = ab.reshape(AB_ROWS, b, t).transpose(1, 0, 2)
    oa = _attn_sample(q, cache_k.reshape(b, c, A_WIDTH), cache_v.reshape(b, c, A_WIDTH), k, v, rel_bias)
    prev = jnp.pad(conv_state.astype(F32), ((0, 0), (SUBLANES - (CONV_WIDTH - 1), 0), (0, 0)))
    dq, dk, dv = _conv_sample(cv, prev, w["conv_w"])
    pad = lambda a: jnp.pad(a, ((0, 0), (0, DELTA_CHUNK - t), (0, 0)))
    ab = jnp.pad(ab, ((0, 0), (0, 0), (0, DELTA_CHUNK - t)))
    ob, s_new = _delta(pad(dq), pad(dk), pad(dv), pad(z), ab, delta_state, a_log, dt_bias, delta_norm_w,
                       block=DELTA_CHUNK, streams=SAMPLE_DELTA_STREAMS, n_valid=t)
    y = _tail(x, w, oa, ob[:, :t], gates)
    new_conv = jnp.concatenate([conv_state.astype(F32), cv], axis=1)[:, -(CONV_WIDTH - 1):]
    return (y, k.reshape(b, t, A_HEADS, A_HEAD_DIM), v.reshape(b, t, A_HEADS, A_HEAD_DIM), new_conv, s_new)


def kernel(x_prompt, x_sample, cache_attn_k, cache_attn_v, state_conv, state_delta, norm_mix_pre, w_in,
           rel_bias, conv_w, a_log, dt_bias, delta_norm_w, w_proj_a, w_proj_b, w_out, norm_mix_post,
           norm_mlp_pre, w_up, w_down, norm_mlp_post):
    depth = w_in.shape[0]
    yp, ys = x_prompt, x_sample
    outs = [[] for _ in range(8)]
    for l in range(depth):
        w = _layer_weights(norm_mix_pre[l], w_in[l], conv_w[l], w_proj_a[l], w_proj_b[l], w_out[l],
                           norm_mix_post[l], norm_mlp_pre[l], w_up[l], w_down[l], norm_mlp_post[l])
        yp, k1, v1, c1, s1 = _prompt_layer(yp, w, rel_bias[l], a_log[l], dt_bias[l], delta_norm_w[l])
        ys, k2, v2, c2, s2 = _sample_layer(ys, cache_attn_k[l], cache_attn_v[l], state_conv[l], state_delta[l],
                                           w, rel_bias[l], a_log[l], dt_bias[l], delta_norm_w[l])
        for lst, val in zip(outs, (k1, v1, c1, s1, k2, v2, c2, s2)):
            lst.append(val)
    return (yp, ys) + tuple(jnp.stack(o) for o in outs)
```
